```python
import math
import jax, jax.numpy as jnp
from jax import lax
import numpy as np

D_MODEL = 2048
BATCH = 4
SEQ = 2048
DEPTH = 4

RG_WIDTH = D_MODEL // 2
RG_BLOCKS = 16
RG_BLOCK_DIM = RG_WIDTH // RG_BLOCKS
RG_CONV = 4
RG_C = 8.0
ATTN_HEADS = 8
ATTN_HEAD_DIM = 64
ATTN_V_DIM = 2 * ATTN_HEAD_DIM
ATTN_QK_WIDTH = ATTN_HEADS * 2 * ATTN_HEAD_DIM
ATTN_WIDTH = ATTN_HEADS * ATTN_V_DIM
MIX_WIDTH = RG_WIDTH + ATTN_WIDTH
IN_WIDTH = 2 * RG_WIDTH + 2 * ATTN_QK_WIDTH + ATTN_WIDTH
IN_SPLITS = (RG_WIDTH, 2 * RG_WIDTH, 2 * RG_WIDTH + ATTN_QK_WIDTH, 2 * RG_WIDTH + 2 * ATTN_QK_WIDTH)
D_FF = 5632
FFN_CONV = 3
Q_BLOCK = 128
DEEPNORM_ALPHA = (2.0 * DEPTH) ** 0.25
DEEPNORM_BETA = (8.0 * DEPTH) ** -0.25
LN_EPS = 1e-5
RMS_EPS = 1e-5

kernel_name = "hymba_rglru_diffattn_alibi_convffn_deepnorm"


def _layernorm(x, g, b):
    xf = x.astype(jnp.float32)
    mu = jnp.mean(xf, axis=-1, keepdims=True)
    var = jnp.mean(jnp.square(xf - mu), axis=-1, keepdims=True)
    y = (xf - mu) * lax.rsqrt(var + LN_EPS) * g.astype(jnp.float32) + b.astype(jnp.float32)
    return y.astype(x.dtype)


def _causal_dwconv(x, w, b):
    K = w.shape[0]
    S = x.shape[1]
    xp = jnp.pad(x, ((0, 0), (K - 1, 0), (0, 0)))
    out = xp[:, 0:S] * w[0]
    for k in range(1, K):
        out = out + xp[:, k:k + S] * w[k]
    return out + b


def _lru_combine(left, right):
    a_l, b_l = left
    a_r, b_r = right
    return a_l * a_r, a_r * b_l + b_r


def _rglru_group(rg_x, rg_gate, conv_w, conv_b, wa, ba, wx, bx, lam):
    B, S, _ = rg_x.shape
    u = _causal_dwconv(rg_x, conv_w, conv_b)
    ub = u.reshape(B, S, RG_BLOCKS, RG_BLOCK_DIM)
    r = jax.nn.sigmoid(jnp.einsum('bsgi,gij->bsgj', ub, wa).reshape(B, S, RG_WIDTH) + ba)
    ig = jax.nn.sigmoid(jnp.einsum('bsgi,gij->bsgj', ub, wx).reshape(B, S, RG_WIDTH) + bx)
    log_a = RG_C * r.astype(jnp.float32) * jax.nn.log_sigmoid(lam.astype(jnp.float32))
    a = jnp.exp(log_a)
    mult = jnp.sqrt(-jnp.expm1(2.0 * log_a))
    bterm = mult * (ig * u).astype(jnp.float32)
    _, h = lax.associative_scan(_lru_combine, (a, bterm), axis=1)
    return h.astype(rg_x.dtype) * jax.nn.gelu(rg_gate)


def _diff_attn_group(q, k, v, layer_idx, lq1, lk1, lq2, lk2, subln_g):
    B, S, _ = q.shape
    H, d, V = ATTN_HEADS, ATTN_HEAD_DIM, ATTN_V_DIM
    kh = k.reshape(B, S, H, 2, d)
    vh = v.reshape(B, S, H, V)
    lam_init = 0.8 - 0.6 * math.exp(-0.3 * layer_idx)
    lam = (jnp.exp(jnp.sum(lq1.astype(jnp.float32) * lk1.astype(jnp.float32)))
           - jnp.exp(jnp.sum(lq2.astype(jnp.float32) * lk2.astype(jnp.float32))) + lam_init)
    slopes = jnp.exp2(-8.0 * jnp.arange(1, H + 1, dtype=jnp.float32) / H)
    scale = d ** -0.5
    nb = S // Q_BLOCK
    qb = q.reshape(B, nb, Q_BLOCK, H, 2, d).transpose(1, 0, 2, 3, 4, 5)
    starts = jnp.arange(nb, dtype=jnp.int32) * Q_BLOCK
    kpos = jnp.arange(S, dtype=jnp.int32)
    neg = jnp.finfo(jnp.float32).min

    def block(args):
        q_blk, start = args
        qpos = start + jnp.arange(Q_BLOCK, dtype=jnp.int32)
        dist = qpos[:, None] - kpos[None, :]
        bias = -slopes[:, None, None] * dist.astype(jnp.float32)[None]
        bias = jnp.where((dist >= 0)[None], bias, neg)
        s = jnp.einsum('bqhcd,bkhcd->bhcqk', q_blk, kh,
                       preferred_element_type=jnp.float32) * scale + bias[None, :, None]
        p = jax.nn.softmax(s, axis=-1)
        wgt = p[:, :, 0] - lam * p[:, :, 1]
        return jnp.einsum('bhqk,bkhv->bqhv', wgt.astype(vh.dtype), vh)

    o = lax.map(block, (qb, starts))
    o = o.transpose(1, 0, 2, 3, 4).reshape(B, S, H, V)
    of = o.astype(jnp.float32)
    of = of * lax.rsqrt(jnp.mean(jnp.square(of), axis=-1, keepdims=True) + RMS_EPS)
    of = of * subln_g.astype(jnp.float32) * (1.0 - lam_init)
    return of.astype(q.dtype).reshape(B, S, ATTN_WIDTH)


def setup_inputs(seed: int = 0) -> dict:
    key = jax.random.key(seed)
    ks = jax.random.split(key, 24)
    f32 = jnp.float32
    nrm = lambda k, shape, s: jax.random.normal(k, shape, f32) * s
    a0 = jax.random.uniform(ks[9], (DEPTH, RG_WIDTH), f32, minval=0.9, maxval=0.999)
    return {
        "x": jax.random.normal(ks[0], (BATCH, SEQ, D_MODEL), f32),
        "w_in": nrm(ks[1], (DEPTH, D_MODEL, IN_WIDTH), D_MODEL ** -0.5),
        "rg_conv_w": nrm(ks[2], (DEPTH, RG_CONV, RG_WIDTH), RG_CONV ** -0.5),
        "rg_conv_b": nrm(ks[3], (DEPTH, RG_WIDTH), 0.01),
        "rg_gate_a_w": nrm(ks[4], (DEPTH, RG_BLOCKS, RG_BLOCK_DIM, RG_BLOCK_DIM), RG_BLOCK_DIM ** -0.5),
        "rg_gate_a_b": nrm(ks[5], (DEPTH, RG_WIDTH), 0.01),
        "rg_gate_x_w": nrm(ks[6], (DEPTH, RG_BLOCKS, RG_BLOCK_DIM, RG_BLOCK_DIM), RG_BLOCK_DIM ** -0.5),
        "rg_gate_x_b": nrm(ks[7], (DEPTH, RG_WIDTH), 0.01),
        "rg_lambda": jnp.log(a0) - jnp.log1p(-a0),
        "lam_q1": nrm(ks[10], (DEPTH, ATTN_HEAD_DIM), 0.1),
        "lam_k1": nrm(ks[11], (DEPTH, ATTN_HEAD_DIM), 0.1),
        "lam_q2": nrm(ks[12], (DEPTH, ATTN_HEAD_DIM), 0.1),
        "lam_k2": nrm(ks[13], (DEPTH, ATTN_HEAD_DIM), 0.1),
        "subln_g": 1.0 + nrm(ks[14], (DEPTH, ATTN_V_DIM), 0.02),
        "w_out": nrm(ks[15], (DEPTH, MIX_WIDTH, D_MODEL), MIX_WIDTH ** -0.5 * DEEPNORM_BETA),
        "ln_mix_g": 1.0 + nrm(ks[16], (DEPTH, D_MODEL), 0.02),
        "ln_mix_b": nrm(ks[17], (DEPTH, D_MODEL), 0.02),
        "w_up": nrm(ks[18], (DEPTH, D_MODEL, 2 * D_FF), D_MODEL ** -0.5),
        "ffn_conv_w": nrm(ks[19], (DEPTH, FFN_CONV, 2 * D_FF), FFN_CONV ** -0.5),
        "ffn_conv_b": nrm(ks[20], (DEPTH, 2 * D_FF), 0.01),
        "w_down": nrm(ks[21], (DEPTH, D_FF, D_MODEL), D_FF ** -0.5 * DEEPNORM_BETA),
        "ln_ffn_g": 1.0 + nrm(ks[22], (DEPTH, D_MODEL), 0.02),
        "ln_ffn_b": nrm(ks[23], (DEPTH, D_MODEL), 0.02),
    }


def reference(x, w_in, rg_conv_w, rg_conv_b, rg_gate_a_w, rg_gate_a_b, rg_gate_x_w, rg_gate_x_b,
              rg_lambda, lam_q1, lam_k1, lam_q2, lam_k2, subln_g, w_out, ln_mix_g, ln_mix_b,
              w_up, ffn_conv_w, ffn_conv_b, w_down, ln_ffn_g, ln_ffn_b):
    for i in range(DEPTH):
        proj = jnp.einsum('bsd,de->bse', x, w_in[i])
        rg_x, rg_gate, q, k, v = jnp.split(proj, IN_SPLITS, axis=-1)
        rg_out = _rglru_group(rg_x, rg_gate, rg_conv_w[i], rg_conv_b[i], rg_gate_a_w[i],
                              rg_gate_a_b[i], rg_gate_x_w[i], rg_gate_x_b[i], rg_lambda[i])
        at_out = _diff_attn_group(q, k, v, i, lam_q1[i], lam_k1[i], lam_q2[i], lam_k2[i], subln_g[i])
        mix = jnp.einsum('bsm,md->bsd', jnp.concatenate([rg_out, at_out], axis=-1), w_out[i])
        x = _layernorm(DEEPNORM_ALPHA * x + mix, ln_mix_g[i], ln_mix_b[i])
        hu = _causal_dwconv(jnp.einsum('bsd,df->bsf', x, w_up[i]), ffn_conv_w[i], ffn_conv_b[i])
        g, u = jnp.split(hu, 2, axis=-1)
        ffn = jnp.einsum('bsf,fd->bsd', jax.nn.gelu(g) * u, w_down[i])
        x = _layernorm(DEEPNORM_ALPHA * x + ffn, ln_ffn_g[i], ln_ffn_b[i])
    return x
```

```python
import functools
import math

import jax
import jax.numpy as jnp
from jax import lax
from jax.experimental import pallas as pl
from jax.experimental.pallas import tpu as pltpu

F32 = jnp.float32
BF16 = jnp.bfloat16

V7X_LANES = 128
V7X_SUBLANES = 8
V7X_MXU_DIM = 256
V7X_VMEM_BYTES = 64 * 1024 * 1024

RG_C = 8.0
LN_EPS = 1e-5
RMS_EPS = 1e-5
MASK_VALUE = -1e30


def _vmem_limit(nbytes):
    return int(min(V7X_VMEM_BYTES - (4 << 20), max(nbytes + (8 << 20), 16 << 20)))


def _gelu_tanh(x):
    c = math.sqrt(2.0 / math.pi)
    return 0.5 * x * (1.0 + jnp.tanh(c * (x + 0.044715 * (x * x * x))))


def _causal_conv_rows(pad_ref, cw, cb, r0, rows):
    conv_k = cw.shape[0]
    pad = V7X_SUBLANES
    win = pad_ref[pl.ds(r0, rows + pad), :]
    out = cb + cw[conv_k - 1:conv_k, :] * win[pad:pad + rows, :]
    for k in range(conv_k - 1):
        lo = pad - (conv_k - 1) + k
        out = out + cw[k:k + 1, :] * win[lo:lo + rows, :]
    return out


def _matmul_kernel(x_ref, w_ref, o_ref):
    o_ref[...] = jnp.dot(x_ref[...], w_ref[...], preferred_element_type=F32).astype(o_ref.dtype)


def _matmul(x, w, *, col_start, n_cols, out_dtype, tm, tn, name):
    T, K = x.shape
    assert T % tm == 0 and n_cols % tn == 0 and col_start % tn == 0
    off = col_start // tn
    est = 2 * (tm * K * 2 + K * tn * 2 + tm * tn * 4) + tm * tn * 4
    return pl.pallas_call(
        _matmul_kernel,
        grid=(T // tm, n_cols // tn),
        in_specs=[
            pl.BlockSpec((tm, K), lambda m, n: (m, 0)),
            pl.BlockSpec((K, tn), lambda m, n: (0, n + off)),
        ],
        out_specs=pl.BlockSpec((tm, tn), lambda m, n: (m, n)),
        out_shape=jax.ShapeDtypeStruct((T, n_cols), out_dtype),
        compiler_params=pltpu.CompilerParams(
            dimension_semantics=("parallel", "parallel"), vmem_limit_bytes=_vmem_limit(est)),
        name=name,
    )(x, w)


def _rglru_kernel(x_ref, gate_ref, cw_ref, cb_ref, wg_ref, ba_ref, bx_ref, lam_ref, o_ref,
                  xpad_ref, hs_ref, *, rows, conv_k):
    S, C = x_ref.shape
    pad = V7X_SUBLANES
    xpad_ref[0:pad, :] = jnp.zeros((pad, C), F32)
    xpad_ref[pad:pad + S, :] = x_ref[...]

    lam = lam_ref[...]
    log_sig = jnp.minimum(lam, 0.0) - jnp.log1p(jnp.exp(-jnp.abs(lam)))
    cw = cw_ref[...]
    cb = cb_ref[...]
    ba = ba_ref[...]
    bx = bx_ref[...]
    wg = wg_ref[0]
    sub = lax.broadcasted_iota(jnp.int32, (rows, C), 0) % V7X_SUBLANES
    n_groups = rows // V7X_SUBLANES

    def chunk(ci, h_prev):
        t0 = pl.multiple_of(ci * rows, rows)
        u = _causal_conv_rows(xpad_ref, cw, cb, t0, rows)
        pre = jnp.dot(u.astype(BF16), wg, preferred_element_type=F32)
        r = jax.nn.sigmoid(pre[:, :C] + ba)
        ig = jax.nn.sigmoid(pre[:, C:] + bx)
        log_a = RG_C * r * log_sig
        a = jnp.exp(log_a)
        th = jnp.tanh(log_a)
        mult = jnp.sqrt(-2.0 * th / (1.0 - th))
        b = mult * (ig * u)
        for s in (1, 2, 4):
            keep = sub >= s
            a_sh = jnp.where(keep, pltpu.roll(a, s, axis=0), 1.0)
            b_sh = jnp.where(keep, pltpu.roll(b, s, axis=0), 0.0)
            b = a * b_sh + b
            a = a * a_sh
        h = h_prev
        for g in range(n_groups):
            lo = g * V7X_SUBLANES
            hg = a[lo:lo + V7X_SUBLANES, :] * h + b[lo:lo + V7X_SUBLANES, :]
            h = hg[V7X_SUBLANES - 1:V7X_SUBLANES, :]
            hs_ref[lo:lo + V7X_SUBLANES, :] = hg
        gate = gate_ref[pl.ds(t0, rows), :]
        o_ref[pl.ds(t0, rows), :] = (hs_ref[...] * _gelu_tanh(gate)).astype(o_ref.dtype)
        return h

    lax.fori_loop(0, S // rows, chunk, jnp.zeros((1, C), F32))


def _rglru(rg, cw, cb, wg, ba, bx, lam, *, batch, seq, width, out_dtype):
    C = wg.shape[1]
    nct = width // C
    rows = 128
    conv_k = cw.shape[0]
    vec = lambda: pl.BlockSpec((1, C), lambda b, c: (0, c))
    est = 2 * (2 * seq * C * 4 + seq * C * 2 + C * 2 * C * 2) + (seq + 8) * C * 4
    return pl.pallas_call(
        functools.partial(_rglru_kernel, rows=rows, conv_k=conv_k),
        grid=(batch, nct),
        in_specs=[
            pl.BlockSpec((seq, C), lambda b, c: (b, c)),
            pl.BlockSpec((seq, C), lambda b, c: (b, c + nct)),
            pl.BlockSpec((conv_k, C), lambda b, c: (0, c)),
            vec(),
            pl.BlockSpec((1, C, 2 * C), lambda b, c: (c, 0, 0)),
            vec(), vec(), vec(),
        ],
        out_specs=pl.BlockSpec((seq, C), lambda b, c: (b, c)),
        out_shape=jax.ShapeDtypeStruct((batch * seq, width), out_dtype),
        scratch_shapes=[pltpu.VMEM((seq + V7X_SUBLANES, C), F32), pltpu.VMEM((rows, C), F32)],
        compiler_params=pltpu.CompilerParams(
            dimension_semantics=("parallel", "parallel"), vmem_limit_bytes=_vmem_limit(est)),
        name="rglru",
    )(rg, rg, cw, cb, wg, ba, bx, lam)


def _attn_kernel(q_ref, k_ref, v_ref, lamv_ref, g_ref, o_ref, m_ref, l_ref, acc_ref,
                 *, tq, n_heads, head_dim, scale, lam_init):
    h = pl.program_id(1)
    qi = pl.program_id(2)
    tk = tq
    slope = jnp.exp2(jnp.full((1, 1), -8.0 / n_heads, F32) * (h + 1).astype(F32))

    q = q_ref[...] * jnp.asarray(scale, q_ref.dtype)
    lane = lax.broadcasted_iota(jnp.int32, q.shape, 1)
    zero = jnp.zeros_like(q)
    qc = (jnp.where(lane < head_dim, q, zero), jnp.where(lane >= head_dim, q, zero))

    m_ref[...] = jnp.full(m_ref.shape, MASK_VALUE, F32)
    l_ref[...] = jnp.zeros(l_ref.shape, F32)
    acc_ref[...] = jnp.zeros(acc_ref.shape, F32)

    kcol = lax.broadcasted_iota(jnp.int32, (1, tk), 1).astype(F32)

    def block(j, masked):
        k0 = pl.multiple_of(j * tk, tk)
        ks = k_ref[pl.ds(k0, tk), :]
        vs = v_ref[pl.ds(k0, tk), :]
        brow = slope * (kcol + k0.astype(F32))
        if masked:
            row = lax.broadcasted_iota(jnp.int32, (tq, tk), 0)
            col = lax.broadcasted_iota(jnp.int32, (tq, tk), 1)
            causal = col <= row
        for c in range(2):
            s = lax.dot_general(qc[c], ks, (((1,), (1,)), ((), ())), preferred_element_type=F32) + brow
            if masked:
                s = jnp.where(causal, s, MASK_VALUE)
            m_old = m_ref[c]
            m_new = jnp.maximum(m_old, jnp.max(s, axis=-1, keepdims=True))
            alpha = jnp.exp(m_old - m_new)
            p = jnp.exp(s - m_new)
            l_ref[c] = alpha * l_ref[c] + jnp.sum(p, axis=-1, keepdims=True)
            acc_ref[c] = alpha * acc_ref[c] + jnp.dot(p.astype(vs.dtype), vs, preferred_element_type=F32)
            m_ref[c] = m_new

    def body(j, carry):
        block(j, False)
        return carry

    lax.fori_loop(0, qi, body, 0)
    block(qi, True)

    lv = lamv_ref[...]
    lam = (jnp.exp(jnp.sum(lv[0:1, :] * lv[1:2, :], axis=-1, keepdims=True))
           - jnp.exp(jnp.sum(lv[2:3, :] * lv[3:4, :], axis=-1, keepdims=True)) + lam_init)
    o = acc_ref[0] / l_ref[0] - lam * (acc_ref[1] / l_ref[1])
    ms = jnp.mean(o * o, axis=-1, keepdims=True)
    o = o * lax.rsqrt(ms + RMS_EPS) * (g_ref[...] * (1.0 - lam_init))
    o_ref[...] = o.astype(o_ref.dtype)


def _diff_attn(qkv, lamv, g, *, batch, seq, n_heads, head_dim, v_dim, lam_init, out_dtype):
    tq = 256
    nq = seq // tq
    qk_w = 2 * head_dim
    assert qk_w == v_dim
    est = 2 * (tq * qk_w * 2 + 2 * seq * qk_w * 2 + tq * v_dim * 2) + 2 * tq * (v_dim + 2 * V7X_LANES) * 4 \
        + 6 * tq * tq * 4
    return pl.pallas_call(
        functools.partial(_attn_kernel, tq=tq, n_heads=n_heads, head_dim=head_dim,
                          scale=head_dim ** -0.5, lam_init=lam_init),
        grid=(batch, n_heads, nq),
        in_specs=[
            pl.BlockSpec((tq, qk_w), lambda b, h, i: (b * nq + i, h)),
            pl.BlockSpec((seq, qk_w), lambda b, h, i: (b, n_heads + h)),
            pl.BlockSpec((seq, v_dim), lambda b, h, i: (b, 2 * n_heads + h)),
            pl.BlockSpec(lamv.shape, lambda b, h, i: (0, 0)),
            pl.BlockSpec((1, v_dim), lambda b, h, i: (0, 0)),
        ],
        out_specs=pl.BlockSpec((tq, v_dim), lambda b, h, i: (b * nq + i, h)),
        out_shape=jax.ShapeDtypeStruct((batch * seq, n_heads * v_dim), out_dtype),
        scratch_shapes=[
            pltpu.VMEM((2, tq, 1), F32),
            pltpu.VMEM((2, tq, 1), F32),
            pltpu.VMEM((2, tq, v_dim), F32),
        ],
        compiler_params=pltpu.CompilerParams(
            dimension_semantics=("parallel", "parallel", "parallel"), vmem_limit_bytes=_vmem_limit(est)),
        name="diff_attn",
    )(qkv, qkv, qkv, lamv, g)


def _residual_layernorm(acc_ref, xres_ref, g_ref, b_ref, o32_ref, o16_ref, *, alpha, rows):
    tm, D = acc_ref.shape
    g = g_ref[...]
    b = b_ref[...]

    def chunk(ci, carry):
        r0 = pl.multiple_of(ci * rows, rows)
        y = alpha * xres_ref[pl.ds(r0, rows), :] + acc_ref[pl.ds(r0, rows), :]
        mu = jnp.mean(y, axis=-1, keepdims=True)
        d = y - mu
        var = jnp.mean(d * d, axis=-1, keepdims=True)
        out = d * lax.rsqrt(var + LN_EPS) * g + b
        o32_ref[pl.ds(r0, rows), :] = out
        o16_ref[pl.ds(r0, rows), :] = out.astype(o16_ref.dtype)
        return carry

    lax.fori_loop(0, tm // rows, chunk, 0)


def _outproj_kernel(rg_ref, at_ref, w_ref, xres_ref, g_ref, b_ref, o32_ref, o16_ref, acc_ref, *, alpha):
    half = rg_ref.shape[1]
    acc_ref[...] = (jnp.dot(rg_ref[...], w_ref[0:half, :], preferred_element_type=F32)
                    + jnp.dot(at_ref[...], w_ref[half:, :], preferred_element_type=F32))
    _residual_layernorm(acc_ref, xres_ref, g_ref, b_ref, o32_ref, o16_ref, alpha=alpha, rows=64)


def _outproj_ln(rg_out, at_out, w, xres, g, b, *, alpha, tm):
    T, half = rg_out.shape
    D = w.shape[1]
    est = 2 * (2 * tm * half * 2 + w.shape[0] * D * 2 + tm * D * 4 + tm * D * 4 + tm * D * 2) + 2 * tm * D * 4
    row = lambda width: pl.BlockSpec((tm, width), lambda m: (m, 0))
    full = lambda a: pl.BlockSpec(a.shape, lambda m: (0, 0))
    return pl.pallas_call(
        functools.partial(_outproj_kernel, alpha=alpha),
        grid=(T // tm,),
        in_specs=[row(half), row(half), full(w), row(D), full(g), full(b)],
        out_specs=[row(D), row(D)],
        out_shape=[jax.ShapeDtypeStruct((T, D), F32), jax.ShapeDtypeStruct((T, D), BF16)],
        scratch_shapes=[pltpu.VMEM((tm, D), F32)],
        compiler_params=pltpu.CompilerParams(
            dimension_semantics=("parallel",), vmem_limit_bytes=_vmem_limit(est)),
        name="outproj_ln",
    )(rg_out, at_out, w, xres, g, b)


def _ffn_up_kernel(x_ref, wg_ref, wu_ref, cwg_ref, cwu_ref, cbg_ref, cbu_ref, o_ref, pg_ref, pu_ref,
                   *, tiles_per_seq, conv_k, rows):
    tm, tn = o_ref.shape
    pad = V7X_SUBLANES
    m = pl.program_id(1)

    @pl.when(m % tiles_per_seq == 0)
    def _():
        pg_ref[0:pad, :] = jnp.zeros((pad, tn), F32)
        pu_ref[0:pad, :] = jnp.zeros((pad, tn), F32)

    x = x_ref[...]
    pg_ref[pad:pad + tm, :] = jnp.dot(x, wg_ref[...], preferred_element_type=F32)
    pu_ref[pad:pad + tm, :] = jnp.dot(x, wu_ref[...], preferred_element_type=F32)

    cwg = cwg_ref[...]
    cwu = cwu_ref[...]
    cbg = cbg_ref[...]
    cbu = cbu_ref[...]

    def chunk(ci, carry):
        r0 = pl.multiple_of(ci * rows, rows)
        gv = _causal_conv_rows(pg_ref, cwg, cbg, r0, rows)
        uv = _causal_conv_rows(pu_ref, cwu, cbu, r0, rows)
        o_ref[pl.ds(r0, rows), :] = (_gelu_tanh(gv) * uv).astype(o_ref.dtype)
        return carry

    lax.fori_loop(0, tm // rows, chunk, 0)
    pg_ref[0:pad, :] = pg_ref[tm:tm + pad, :]
    pu_ref[0:pad, :] = pu_ref[tm:tm + pad, :]


def _ffn_up(x, w, cw, cb, *, seq, d_ff, tm, tn, out_dtype):
    T, K = x.shape
    nt = d_ff // tn
    conv_k = cw.shape[0]
    assert seq % tm == 0 and d_ff % tn == 0
    est = 2 * (tm * K * 2 + 2 * K * tn * 2 + tm * tn * 2) + 2 * (tm + 8) * tn * 4 + 2 * tm * tn * 4
    vecg = lambda r: pl.BlockSpec((r, tn), lambda n, m: (0, n))
    vecu = lambda r: pl.BlockSpec((r, tn), lambda n, m: (0, n + nt))
    return pl.pallas_call(
        functools.partial(_ffn_up_kernel, tiles_per_seq=seq // tm, conv_k=conv_k, rows=128),
        grid=(nt, T // tm),
        in_specs=[
            pl.BlockSpec((tm, K), lambda n, m: (m, 0)),
            pl.BlockSpec((K, tn), lambda n, m: (0, n)),
            pl.BlockSpec((K, tn), lambda n, m: (0, n + nt)),
            vecg(conv_k), vecu(conv_k), vecg(1), vecu(1),
        ],
        out_specs=pl.BlockSpec((tm, tn), lambda n, m: (m, n)),
        out_shape=jax.ShapeDtypeStruct((T, d_ff), out_dtype),
        scratch_shapes=[pltpu.VMEM((tm + V7X_SUBLANES, tn), F32), pltpu.VMEM((tm + V7X_SUBLANES, tn), F32)],
        compiler_params=pltpu.CompilerParams(
            dimension_semantics=("parallel", "arbitrary"), vmem_limit_bytes=_vmem_limit(est)),
        name="ffn_up",
    )(x, w, w, cw, cw, cb, cb)


def _ffn_down_kernel(h_ref, w_ref, xres_ref, g_ref, b_ref, o32_ref, o16_ref, acc_ref, *, alpha):
    k = pl.program_id(1)
    part = jnp.dot(h_ref[...], w_ref[...], preferred_element_type=F32)

    @pl.when(k == 0)
    def _():
        acc_ref[...] = part

    @pl.when(k > 0)
    def _():
        acc_ref[...] += part

    @pl.when(k == pl.num_programs(1) - 1)
    def _():
        _residual_layernorm(acc_ref, xres_ref, g_ref, b_ref, o32_ref, o16_ref, alpha=alpha, rows=64)


def _ffn_down_ln(h, w, xres, g, b, *, alpha, tm, tk):
    T, K = h.shape
    D = w.shape[1]
    assert T % tm == 0 and K % tk == 0
    est = 2 * (tm * tk * 2 + tk * D * 2 + tm * D * 4 + tm * D * 4 + tm * D * 2) + 2 * tm * D * 4
    row = lambda: pl.BlockSpec((tm, D), lambda m, k: (m, 0))
    full = lambda a: pl.BlockSpec(a.shape, lambda m, k: (0, 0))
    return pl.pallas_call(
        functools.partial(_ffn_down_kernel, alpha=alpha),
        grid=(T // tm, K // tk),
        in_specs=[
            pl.BlockSpec((tm, tk), lambda m, k: (m, k)),
            pl.BlockSpec((tk, D), lambda m, k: (k, 0)),
            row(), full(g), full(b),
        ],
        out_specs=[row(), row()],
        out_shape=[jax.ShapeDtypeStruct((T, D), F32), jax.ShapeDtypeStruct((T, D), BF16)],
        scratch_shapes=[pltpu.VMEM((tm, D), F32)],
        compiler_params=pltpu.CompilerParams(
            dimension_semantics=("parallel", "arbitrary"), vmem_limit_bytes=_vmem_limit(est)),
        name="ffn_down_ln",
    )(h, w, xres, g, b)


def _block_diag(w, tile):
    G, d, _ = w.shape
    per = tile // d
    w4 = w.reshape(G // per, per, d, d)
    eye = jnp.eye(per, dtype=w.dtype)
    return jnp.einsum('jipq,ik->jipkq', w4, eye).reshape(G // per, tile, tile)


def kernel(x, w_in, rg_conv_w, rg_conv_b, rg_gate_a_w, rg_gate_a_b, rg_gate_x_w, rg_gate_x_b, rg_lambda,
           lam_q1, lam_k1, lam_q2, lam_k2, subln_g, w_out, ln_mix_g, ln_mix_b, w_up, ffn_conv_w,
           ffn_conv_b, w_down, ln_ffn_g, ln_ffn_b):
    B, S, D = x.shape
    depth = w_in.shape[0]
    rg_w = rg_conv_w.shape[2]
    head_dim = lam_q1.shape[1]
    v_dim = subln_g.shape[1]
    mix_w = w_out.shape[1]
    attn_w = mix_w - rg_w
    n_heads = attn_w // v_dim
    qkv_w = w_in.shape[2] - 2 * rg_w
    d_ff = w_down.shape[1]
    T = B * S
    alpha = (2.0 * depth) ** 0.25

    x32 = x.reshape(T, D)
    x16 = x32.astype(BF16)
    for i in range(depth):
        w_in_i = w_in[i].astype(BF16)
        w_out_i = w_out[i].astype(BF16)
        w_up_i = w_up[i].astype(BF16)
        w_down_i = w_down[i].astype(BF16)
        wg = jnp.concatenate([_block_diag(rg_gate_a_w[i], V7X_MXU_DIM),
                              _block_diag(rg_gate_x_w[i], V7X_MXU_DIM)], axis=-1).astype(BF16)
        lam_init = 0.8 - 0.6 * math.exp(-0.3 * i)
        lamv = jnp.stack([lam_q1[i], lam_k1[i], lam_q2[i], lam_k2[i]])

        rg = _matmul(x16, w_in_i, col_start=0, n_cols=2 * rg_w, out_dtype=F32, tm=512, tn=1024,
                     name="in_proj_rg")
        qkv = _matmul(x16, w_in_i, col_start=2 * rg_w, n_cols=qkv_w, out_dtype=BF16, tm=512, tn=1024,
                      name="in_proj_qkv")
        rg_out = _rglru(rg, rg_conv_w[i], rg_conv_b[i][None], wg, rg_gate_a_b[i][None],
                        rg_gate_x_b[i][None], rg_lambda[i][None], batch=B, seq=S, width=rg_w,
                        out_dtype=BF16)
        at_out = _diff_attn(qkv, lamv, subln_g[i][None], batch=B, seq=S, n_heads=n_heads,
                            head_dim=head_dim, v_dim=v_dim, lam_init=lam_init, out_dtype=BF16)
        x32, x16 = _outproj_ln(rg_out, at_out, w_out_i, x32, ln_mix_g[i][None], ln_mix_b[i][None],
                               alpha=alpha, tm=512)
        hmid = _ffn_up(x16, w_up_i, ffn_conv_w[i], ffn_conv_b[i][None], seq=S, d_ff=d_ff, tm=min(1024, S),
                       tn=512, out_dtype=BF16)
        x32, x16 = _ffn_down_ln(hmid, w_down_i, x32, ln_ffn_g[i][None], ln_ffn_b[i][None], alpha=alpha,
                                tm=512, tk=512)
    return x32.reshape(B, S, D)
```

```python
import functools
import math

import jax
import jax.numpy as jnp
from jax import lax
from jax.experimental import pallas as pl
from jax.experimental.pallas import tpu as pltpu

F32 = jnp.float32
BF16 = jnp.bfloat16

V7X_LANES = 128
V7X_SUBLANES = 8
V7X_MXU_DIM = 256
V7X_VMEM_BYTES = 64 * 1024 * 1024

RG_C = 8.0
LN_EPS = 1e-5
RMS_EPS = 1e-5
MASK_VALUE = -1e30


def _vmem_limit(nbytes):
    return int(min(V7X_VMEM_BYTES - (4 << 20), max(nbytes + (8 << 20), 16 << 20)))


def _gelu_tanh(x):
    c = math.sqrt(2.0 / math.pi)
    return 0.5 * x * (1.0 + jnp.tanh(c * (x + 0.044715 * (x * x * x))))


def _causal_conv_rows(pad_ref, cw, cb, r0, rows):
    conv_k = cw.shape[0]
    pad = V7X_SUBLANES
    win = pad_ref[pl.ds(r0, rows + pad), :]
    out = cb + cw[conv_k - 1:conv_k, :] * win[pad:pad + rows, :]
    for k in range(conv_k - 1):
        lo = pad - (conv_k - 1) + k
        out = out + cw[k:k + 1, :] * win[lo:lo + rows, :]
    return out


def _matmul_kernel(x_ref, w_ref, o_ref):
    o_ref[...] = jnp.dot(x_ref[...], w_ref[...], preferred_element_type=F32).astype(o_ref.dtype)


def _matmul(x, w, *, col_start, n_cols, out_dtype, tm, tn, name):
    T, K = x.shape
    assert T % tm == 0 and n_cols % tn == 0 and col_start % tn == 0
    off = col_start // tn
    est = 2 * (tm * K * 2 + K * tn * 2 + tm * tn * 4) + tm * tn * 4
    return pl.pallas_call(
        _matmul_kernel,
        grid=(T // tm, n_cols // tn),
        in_specs=[
            pl.BlockSpec((tm, K), lambda m, n: (m, 0)),
            pl.BlockSpec((K, tn), lambda m, n: (0, n + off)),
        ],
        out_specs=pl.BlockSpec((tm, tn), lambda m, n: (m, n)),
        out_shape=jax.ShapeDtypeStruct((T, n_cols), out_dtype),
        compiler_params=pltpu.CompilerParams(
            dimension_semantics=("parallel", "parallel"), vmem_limit_bytes=_vmem_limit(est)),
        name=name,
    )(x, w)


def _rglru_kernel(x_ref, gate_ref, cw_ref, cb_ref, wg_ref, ba_ref, bx_ref, lam_ref, o_ref,
                  xpad_ref, hs_ref, *, rows, conv_k):
    S, C = x_ref.shape
    pad = V7X_SUBLANES
    xpad_ref[0:pad, :] = jnp.zeros((pad, C), F32)
    xpad_ref[pad:pad + S, :] = x_ref[...]

    lam = lam_ref[...]
    log_sig = jnp.minimum(lam, 0.0) - jnp.log1p(jnp.exp(-jnp.abs(lam)))
    cw = cw_ref[...]
    cb = cb_ref[...]
    ba = ba_ref[...]
    bx = bx_ref[...]
    wg = wg_ref[0]
    sub = lax.broadcasted_iota(jnp.int32, (rows, C), 0) % V7X_SUBLANES
    n_groups = rows // V7X_SUBLANES

    def chunk(ci, h_prev):
        t0 = pl.multiple_of(ci * rows, rows)
        u = _causal_conv_rows(xpad_ref, cw, cb, t0, rows)
        pre = jnp.dot(u.astype(BF16), wg, preferred_element_type=F32)
        r = jax.nn.sigmoid(pre[:, :C] + ba)
        ig = jax.nn.sigmoid(pre[:, C:] + bx)
        log_a = RG_C * r * log_sig
        a = jnp.exp(log_a)
        th = jnp.tanh(log_a)
        mult = jnp.sqrt(-2.0 * th / (1.0 - th))
        b = mult * (ig * u)
        for s in (1, 2, 4):
            keep = sub >= s
            a_sh = jnp.where(keep, pltpu.roll(a, s, axis=0), 1.0)
            b_sh = jnp.where(keep, pltpu.roll(b, s, axis=0), 0.0)
            b = a * b_sh + b
            a = a * a_sh
        h = h_prev
        for g in range(n_groups):
            lo = g * V7X_SUBLANES
            hg = a[lo:lo + V7X_SUBLANES, :] * h + b[lo:lo + V7X_SUBLANES, :]
            h = hg[V7X_SUBLANES - 1:V7X_SUBLANES, :]
            hs_ref[lo:lo + V7X_SUBLANES, :] = hg
        gate = gate_ref[pl.ds(t0, rows), :]
        o_ref[pl.ds(t0, rows), :] = (hs_ref[...] * _gelu_tanh(gate)).astype(o_ref.dtype)
        return h

    lax.fori_loop(0, S // rows, chunk, jnp.zeros((1, C), F32))


def _rglru(rg, cw, cb, wg, ba, bx, lam, *, batch, seq, width, out_dtype):
    C = wg.shape[1]
    nct = width // C
    rows = 128
    conv_k = cw.shape[0]
    vec = lambda: pl.BlockSpec((1, C), lambda b, c: (0, c))
    est = 2 * (2 * seq * C * 4 + seq * C * 2 + C * 2 * C * 2) + (seq + 8) * C * 4
    return pl.pallas_call(
        functools.partial(_rglru_kernel, rows=rows, conv_k=conv_k),
        grid=(batch, nct),
        in_specs=[
            pl.BlockSpec((seq, C), lambda b, c: (b, c)),
            pl.BlockSpec((seq, C), lambda b, c: (b, c + nct)),
            pl.BlockSpec((conv_k, C), lambda b, c: (0, c)),
            vec(),
            pl.BlockSpec((1, C, 2 * C), lambda b, c: (c, 0, 0)),
            vec(), vec(), vec(),
        ],
        out_specs=pl.BlockSpec((seq, C), lambda b, c: (b, c)),
        out_shape=jax.ShapeDtypeStruct((batch * seq, width), out_dtype),
        scratch_shapes=[pltpu.VMEM((seq + V7X_SUBLANES, C), F32), pltpu.VMEM((rows, C), F32)],
        compiler_params=pltpu.CompilerParams(
            dimension_semantics=("parallel", "parallel"), vmem_limit_bytes=_vmem_limit(est)),
        name="rglru",
    )(rg, rg, cw, cb, wg, ba, bx, lam)


def _attn_kernel(q_ref, k_ref, v_ref, lamv_ref, g_ref, o_ref, vt_ref, acc_ref,
                 *, tq, tk, n_heads, head_dim, scale, lam_init):
    h = pl.program_id(1)
    qi = pl.program_id(2)
    n_kv = vt_ref.shape[0]

    @pl.when(qi == 0)
    def _():
        for jb in range(n_kv):
            vt_ref[jb] = v_ref[jb * tk:(jb + 1) * tk, :].astype(F32).T.astype(vt_ref.dtype)

    slope = jnp.exp2(jnp.full((1, 1), -8.0 / n_heads, F32) * (h + 1).astype(F32))

    q = q_ref[...] * jnp.asarray(scale, q_ref.dtype)
    lane = lax.broadcasted_iota(jnp.int32, q.shape, 1)
    zero = jnp.zeros_like(q)
    qcat = jnp.concatenate([jnp.where(lane < head_dim, q, zero), jnp.where(lane >= head_dim, q, zero)],
                           axis=0)

    acc_ref[...] = jnp.zeros(acc_ref.shape, F32)

    key_bias = slope * lax.broadcasted_iota(jnp.int32, (tk, 2 * tq), 0).astype(F32)
    q0 = qi * tq

    def block(j, masked, stats):
        m_old, l_old = stats
        vt = vt_ref[j]
        k0 = pl.multiple_of(j * tk, tk)
        ks = k_ref[pl.ds(k0, tk), :]
        off = slope * k0.astype(F32)
        s = lax.dot_general(ks, qcat, (((1,), (1,)), ((), ())), preferred_element_type=F32) + key_bias
        if masked:
            key_pos = lax.broadcasted_iota(jnp.int32, (tk, 2 * tq), 0) + k0
            col = lax.broadcasted_iota(jnp.int32, (tk, 2 * tq), 1)
            query_pos = jnp.where(col >= tq, col - tq, col) + q0
            s = jnp.where(key_pos <= query_pos, s, MASK_VALUE)
        m_new = jnp.maximum(m_old, jnp.max(s, axis=0, keepdims=True) + off)
        alpha = jnp.exp(m_old - m_new)
        p = jnp.exp(s - (m_new - off))
        l_new = alpha * l_old + jnp.sum(p, axis=0, keepdims=True)
        acc_ref[...] = alpha * acc_ref[...] + jnp.dot(vt, p.astype(vt.dtype), preferred_element_type=F32)
        return m_new, l_new

    n_full = q0 // tk
    init = (jnp.full((1, 2 * tq), MASK_VALUE, F32), jnp.zeros((1, 2 * tq), F32))
    stats = lax.fori_loop(0, n_full, lambda j, st: block(j, False, st), init)
    for t in range(tq // tk):
        stats = block(n_full + t, True, stats)
    l_fin = stats[1]

    lv = lamv_ref[...]
    lam = (jnp.exp(jnp.sum(lv[0:1, :] * lv[1:2, :], axis=-1, keepdims=True))
           - jnp.exp(jnp.sum(lv[2:3, :] * lv[3:4, :], axis=-1, keepdims=True)) + lam_init)
    ot = acc_ref[:, 0:tq] / l_fin[:, 0:tq] - lam * (acc_ref[:, tq:2 * tq] / l_fin[:, tq:2 * tq])
    ms = jnp.mean(ot * ot, axis=0, keepdims=True)
    o = (ot * lax.rsqrt(ms + RMS_EPS)).T * (g_ref[...] * (1.0 - lam_init))
    o_ref[...] = o.astype(o_ref.dtype)


def _diff_attn(qkv, lamv, g, *, batch, seq, n_heads, head_dim, v_dim, lam_init, out_dtype, tq, tk):
    nq = seq // tq
    qk_w = 2 * head_dim
    assert qk_w == v_dim and seq % tq == 0 and tq % tk == 0
    est = 2 * (tq * qk_w * 2 + 2 * seq * qk_w * 2 + tq * v_dim * 2) + seq * v_dim * 2 \
        + v_dim * 2 * tq * 4 + 5 * tk * 2 * tq * 4
    return pl.pallas_call(
        functools.partial(_attn_kernel, tq=tq, tk=tk, n_heads=n_heads, head_dim=head_dim,
                          scale=head_dim ** -0.5, lam_init=lam_init),
        grid=(batch, n_heads, nq),
        in_specs=[
            pl.BlockSpec((tq, qk_w), lambda b, h, i: (b * nq + i, h)),
            pl.BlockSpec((seq, qk_w), lambda b, h, i: (b, n_heads + h)),
            pl.BlockSpec((seq, v_dim), lambda b, h, i: (b, 2 * n_heads + h)),
            pl.BlockSpec(lamv.shape, lambda b, h, i: (0, 0)),
            pl.BlockSpec((1, v_dim), lambda b, h, i: (0, 0)),
        ],
        out_specs=pl.BlockSpec((tq, v_dim), lambda b, h, i: (b * nq + i, h)),
        out_shape=jax.ShapeDtypeStruct((batch * seq, n_heads * v_dim), out_dtype),
        scratch_shapes=[
            pltpu.VMEM((seq // tk, v_dim, tk), BF16),
            pltpu.VMEM((v_dim, 2 * tq), F32),
        ],
        compiler_params=pltpu.CompilerParams(
            dimension_semantics=("parallel", "parallel", "arbitrary"), vmem_limit_bytes=_vmem_limit(est)),
        name="diff_attn",
    )(qkv, qkv, qkv, lamv, g)


def _residual_layernorm(acc_ref, xres_ref, g_ref, b_ref, o32_ref, o16_ref, *, alpha, rows):
    tm, D = acc_ref.shape
    g = g_ref[...]
    b = b_ref[...]

    def chunk(ci, carry):
        r0 = pl.multiple_of(ci * rows, rows)
        y = alpha * xres_ref[pl.ds(r0, rows), :] + acc_ref[pl.ds(r0, rows), :]
        mu = jnp.mean(y, axis=-1, keepdims=True)
        d = y - mu
        var = jnp.mean(d * d, axis=-1, keepdims=True)
        out = d * lax.rsqrt(var + LN_EPS) * g + b
        o32_ref[pl.ds(r0, rows), :] = out
        o16_ref[pl.ds(r0, rows), :] = out.astype(o16_ref.dtype)
        return carry

    lax.fori_loop(0, tm // rows, chunk, 0)


def _outproj_kernel(rg_ref, at_ref, w_ref, xres_ref, g_ref, b_ref, o32_ref, o16_ref, acc_ref, *, alpha):
    half = rg_ref.shape[1]
    acc_ref[...] = (jnp.dot(rg_ref[...], w_ref[0:half, :], preferred_element_type=F32)
                    + jnp.dot(at_ref[...], w_ref[half:, :], preferred_element_type=F32))
    _residual_layernorm(acc_ref, xres_ref, g_ref, b_ref, o32_ref, o16_ref, alpha=alpha, rows=64)


def _outproj_ln(rg_out, at_out, w, xres, g, b, *, alpha, tm):
    T, half = rg_out.shape
    D = w.shape[1]
    est = 2 * (2 * tm * half * 2 + w.shape[0] * D * 2 + tm * D * 4 + tm * D * 4 + tm * D * 2) + 2 * tm * D * 4
    row = lambda width: pl.BlockSpec((tm, width), lambda m: (m, 0))
    full = lambda a: pl.BlockSpec(a.shape, lambda m: (0, 0))
    return pl.pallas_call(
        functools.partial(_outproj_kernel, alpha=alpha),
        grid=(T // tm,),
        in_specs=[row(half), row(half), full(w), row(D), full(g), full(b)],
        out_specs=[row(D), row(D)],
        out_shape=[jax.ShapeDtypeStruct((T, D), F32), jax.ShapeDtypeStruct((T, D), BF16)],
        scratch_shapes=[pltpu.VMEM((tm, D), F32)],
        compiler_params=pltpu.CompilerParams(
            dimension_semantics=("parallel",), vmem_limit_bytes=_vmem_limit(est)),
        name="outproj_ln",
    )(rg_out, at_out, w, xres, g, b)


def _ffn_up_kernel(x_ref, wg_ref, wu_ref, cwg_ref, cwu_ref, cbg_ref, cbu_ref, o_ref, pg_ref, pu_ref,
                   *, tiles_per_seq, conv_k, rows):
    tm, tn = o_ref.shape
    pad = V7X_SUBLANES
    m = pl.program_id(1)

    @pl.when(m % tiles_per_seq == 0)
    def _():
        pg_ref[0:pad, :] = jnp.zeros((pad, tn), F32)
        pu_ref[0:pad, :] = jnp.zeros((pad, tn), F32)

    x = x_ref[...]
    pg_ref[pad:pad + tm, :] = jnp.dot(x, wg_ref[...], preferred_element_type=F32)
    pu_ref[pad:pad + tm, :] = jnp.dot(x, wu_ref[...], preferred_element_type=F32)

    cwg = cwg_ref[...]
    cwu = cwu_ref[...]
    cbg = cbg_ref[...]
    cbu = cbu_ref[...]

    def chunk(ci, carry):
        r0 = pl.multiple_of(ci * rows, rows)
        gv = _causal_conv_rows(pg_ref, cwg, cbg, r0, rows)
        uv = _causal_conv_rows(pu_ref, cwu, cbu, r0, rows)
        o_ref[pl.ds(r0, rows), :] = (_gelu_tanh(gv) * uv).astype(o_ref.dtype)
        return carry

    lax.fori_loop(0, tm // rows, chunk, 0)
    pg_ref[0:pad, :] = pg_ref[tm:tm + pad, :]
    pu_ref[0:pad, :] = pu_ref[tm:tm + pad, :]


def _ffn_up(x, w, cw, cb, *, seq, d_ff, tm, tn, out_dtype):
    T, K = x.shape
    nt = d_ff // tn
    conv_k = cw.shape[0]
    assert seq % tm == 0 and d_ff % tn == 0
    est = 2 * (tm * K * 2 + 2 * K * tn * 2 + tm * tn * 2) + 2 * (tm + 8) * tn * 4 + 2 * tm * tn * 4
    vecg = lambda r: pl.BlockSpec((r, tn), lambda n, m: (0, n))
    vecu = lambda r: pl.BlockSpec((r, tn), lambda n, m: (0, n + nt))
    return pl.pallas_call(
        functools.partial(_ffn_up_kernel, tiles_per_seq=seq // tm, conv_k=conv_k, rows=128),
        grid=(nt, T // tm),
        in_specs=[
            pl.BlockSpec((tm, K), lambda n, m: (m, 0)),
            pl.BlockSpec((K, tn), lambda n, m: (0, n)),
            pl.BlockSpec((K, tn), lambda n, m: (0, n + nt)),
            vecg(conv_k), vecu(conv_k), vecg(1), vecu(1),
        ],
        out_specs=pl.BlockSpec((tm, tn), lambda n, m: (m, n)),
        out_shape=jax.ShapeDtypeStruct((T, d_ff), out_dtype),
        scratch_shapes=[pltpu.VMEM((tm + V7X_SUBLANES, tn), F32), pltpu.VMEM((tm + V7X_SUBLANES, tn), F32)],
        compiler_params=pltpu.CompilerParams(
            dimension_semantics=("parallel", "arbitrary"), vmem_limit_bytes=_vmem_limit(est)),
        name="ffn_up",
    )(x, w, w, cw, cw, cb, cb)


def _ffn_down_kernel(h_ref, w_ref, xres_ref, g_ref, b_ref, o32_ref, o16_ref, acc_ref, *, alpha):
    k = pl.program_id(1)
    part = jnp.dot(h_ref[...], w_ref[...], preferred_element_type=F32)

    @pl.when(k == 0)
    def _():
        acc_ref[...] = part

    @pl.when(k > 0)
    def _():
        acc_ref[...] += part

    @pl.when(k == pl.num_programs(1) - 1)
    def _():
        _residual_layernorm(acc_ref, xres_ref, g_ref, b_ref, o32_ref, o16_ref, alpha=alpha, rows=64)


def _ffn_down_ln(h, w, xres, g, b, *, alpha, tm, tk):
    T, K = h.shape
    D = w.shape[1]
    assert T % tm == 0 and K % tk == 0
    est = 2 * (tm * tk * 2 + tk * D * 2 + tm * D * 4 + tm * D * 4 + tm * D * 2) + 2 * tm * D * 4
    row = lambda: pl.BlockSpec((tm, D), lambda m, k: (m, 0))
    full = lambda a: pl.BlockSpec(a.shape, lambda m, k: (0, 0))
    return pl.pallas_call(
        functools.partial(_ffn_down_kernel, alpha=alpha),
        grid=(T // tm, K // tk),
        in_specs=[
            pl.BlockSpec((tm, tk), lambda m, k: (m, k)),
            pl.BlockSpec((tk, D), lambda m, k: (k, 0)),
            row(), full(g), full(b),
        ],
        out_specs=[row(), row()],
        out_shape=[jax.ShapeDtypeStruct((T, D), F32), jax.ShapeDtypeStruct((T, D), BF16)],
        scratch_shapes=[pltpu.VMEM((tm, D), F32)],
        compiler_params=pltpu.CompilerParams(
            dimension_semantics=("parallel", "arbitrary"), vmem_limit_bytes=_vmem_limit(est)),
        name="ffn_down_ln",
    )(h, w, xres, g, b)


def _block_diag(w, tile):
    G, d, _ = w.shape
    per = tile // d
    w4 = w.reshape(G // per, per, d, d)
    eye = jnp.eye(per, dtype=w.dtype)
    return jnp.einsum('jipq,ik->jipkq', w4, eye).reshape(G // per, tile, tile)


def kernel(x, w_in, rg_conv_w, rg_conv_b, rg_gate_a_w, rg_gate_a_b, rg_gate_x_w, rg_gate_x_b, rg_lambda,
           lam_q1, lam_k1, lam_q2, lam_k2, subln_g, w_out, ln_mix_g, ln_mix_b, w_up, ffn_conv_w,
           ffn_conv_b, w_down, ln_ffn_g, ln_ffn_b):
    B, S, D = x.shape
    depth = w_in.shape[0]
    rg_w = rg_conv_w.shape[2]
    head_dim = lam_q1.shape[1]
    v_dim = subln_g.shape[1]
    mix_w = w_out.shape[1]
    attn_w = mix_w - rg_w
    n_heads = attn_w // v_dim
    qkv_w = w_in.shape[2] - 2 * rg_w
    d_ff = w_down.shape[1]
    T = B * S
    alpha = (2.0 * depth) ** 0.25

    x32 = x.reshape(T, D)
    x16 = x32.astype(BF16)
    for i in range(depth):
        w_in_i = w_in[i].astype(BF16)
        w_out_i = w_out[i].astype(BF16)
        w_up_i = w_up[i].astype(BF16)
        w_down_i = w_down[i].astype(BF16)
        wg = jnp.concatenate([_block_diag(rg_gate_a_w[i], V7X_MXU_DIM),
                              _block_diag(rg_gate_x_w[i], V7X_MXU_DIM)], axis=-1).astype(BF16)
        lam_init = 0.8 - 0.6 * math.exp(-0.3 * i)
        lamv = jnp.stack([lam_q1[i], lam_k1[i], lam_q2[i], lam_k2[i]])

        rg = _matmul(x16, w_in_i, col_start=0, n_cols=2 * rg_w, out_dtype=F32, tm=512, tn=1024,
                     name="in_proj_rg")
        qkv = _matmul(x16, w_in_i, col_start=2 * rg_w, n_cols=qkv_w, out_dtype=BF16, tm=512, tn=1024,
                      name="in_proj_qkv")
        rg_out = _rglru(rg, rg_conv_w[i], rg_conv_b[i][None], wg, rg_gate_a_b[i][None],
                        rg_gate_x_b[i][None], rg_lambda[i][None], batch=B, seq=S, width=rg_w,
                        out_dtype=BF16)
        at_out = _diff_attn(qkv, lamv, subln_g[i][None], batch=B, seq=S, n_heads=n_heads,
                            head_dim=head_dim, v_dim=v_dim, lam_init=lam_init, out_dtype=BF16,
                            tq=min(512, S), tk=min(512, S))
        x32, x16 = _outproj_ln(rg_out, at_out, w_out_i, x32, ln_mix_g[i][None], ln_mix_b[i][None],
                               alpha=alpha, tm=512)
        hmid = _ffn_up(x16, w_up_i, ffn_conv_w[i], ffn_conv_b[i][None], seq=S, d_ff=d_ff, tm=min(1024, S),
                       tn=512, out_dtype=BF16)
        x32, x16 = _ffn_down_ln(hmid, w_down_i, x32, ln_ffn_g[i][None], ln_ffn_b[i][None], alpha=alpha,
                                tm=512, tk=512)
    return x32.reshape(B, S, D)
```

```python
import functools
import math

import jax
import jax.numpy as jnp
from jax import lax
from jax.experimental import pallas as pl
from jax.experimental.pallas import tpu as pltpu

F32 = jnp.float32
BF16 = jnp.bfloat16

V7X_LANES = 128
V7X_SUBLANES = 8
V7X_MXU_DIM = 256
V7X_VMEM_BYTES = 64 * 1024 * 1024

RG_C = 8.0
LN_EPS = 1e-5
RMS_EPS = 1e-5
MASK_VALUE = -1e30


def _vmem_limit(nbytes):
    return int(min(V7X_VMEM_BYTES - (4 << 20), max(nbytes + (8 << 20), 16 << 20)))


def _gelu_tanh(x):
    c = math.sqrt(2.0 / math.pi)
    return 0.5 * x * (1.0 + jnp.tanh(c * (x + 0.044715 * (x * x * x))))


def _causal_conv_rows(pad_ref, cw, cb, r0, rows):
    conv_k = cw.shape[0]
    pad = V7X_SUBLANES
    win = pad_ref[pl.ds(r0, rows + pad), :]
    out = cb + cw[conv_k - 1:conv_k, :] * win[pad:pad + rows, :]
    for k in range(conv_k - 1):
        lo = pad - (conv_k - 1) + k
        out = out + cw[k:k + 1, :] * win[lo:lo + rows, :]
    return out


def _in_proj_kernel(x_ref, w_ref, o_ref, wb_ref):
    @pl.when(pl.program_id(1) == 0)
    def _():
        wb_ref[...] = w_ref[...].astype(wb_ref.dtype)

    o_ref[...] = jnp.dot(x_ref[...], wb_ref[...], preferred_element_type=F32).astype(o_ref.dtype)


def _in_proj(x, w, layer, *, col_start, n_cols, out_dtype, tm, tn, name):
    T, K = x.shape
    assert T % tm == 0 and n_cols % tn == 0 and col_start % tn == 0
    off = col_start // tn
    est = 2 * (tm * K * 2 + K * tn * 4 + tm * tn * 4) + K * tn * 2 + tm * tn * 4
    return pl.pallas_call(
        _in_proj_kernel,
        grid=(n_cols // tn, T // tm),
        in_specs=[
            pl.BlockSpec((tm, K), lambda n, m: (m, 0)),
            pl.BlockSpec((None, K, tn), lambda n, m: (layer, 0, n + off)),
        ],
        out_specs=pl.BlockSpec((tm, tn), lambda n, m: (m, n)),
        out_shape=jax.ShapeDtypeStruct((T, n_cols), out_dtype),
        scratch_shapes=[pltpu.VMEM((K, tn), BF16)],
        compiler_params=pltpu.CompilerParams(
            dimension_semantics=("parallel", "arbitrary"), vmem_limit_bytes=_vmem_limit(est)),
        name=name,
    )(x, w)


def _rglru_kernel(x_ref, gate_ref, cw_ref, cb_ref, wg_ref, ba_ref, bx_ref, lam_ref, o_ref,
                  xpad_ref, hs_ref, *, rows, conv_k):
    S, C = x_ref.shape
    pad = V7X_SUBLANES
    xpad_ref[0:pad, :] = jnp.zeros((pad, C), F32)
    xpad_ref[pad:pad + S, :] = x_ref[...]

    lam = lam_ref[...]
    log_sig = jnp.minimum(lam, 0.0) - jnp.log1p(jnp.exp(-jnp.abs(lam)))
    cw = cw_ref[...]
    cb = cb_ref[...]
    ba = ba_ref[...]
    bx = bx_ref[...]
    wg = wg_ref[0]
    sub = lax.broadcasted_iota(jnp.int32, (rows, C), 0) % V7X_SUBLANES
    n_groups = rows // V7X_SUBLANES

    def chunk(ci, h_prev):
        t0 = pl.multiple_of(ci * rows, rows)
        u = _causal_conv_rows(xpad_ref, cw, cb, t0, rows)
        pre = jnp.dot(u.astype(BF16), wg, preferred_element_type=F32)
        r = jax.nn.sigmoid(pre[:, :C] + ba)
        ig = jax.nn.sigmoid(pre[:, C:] + bx)
        log_a = RG_C * r * log_sig
        a = jnp.exp(log_a)
        th = jnp.tanh(log_a)
        mult = jnp.sqrt(-2.0 * th / (1.0 - th))
        b = mult * (ig * u)
        for s in (1, 2, 4):
            keep = sub >= s
            a_sh = jnp.where(keep, pltpu.roll(a, s, axis=0), 1.0)
            b_sh = jnp.where(keep, pltpu.roll(b, s, axis=0), 0.0)
            b = a * b_sh + b
            a = a * a_sh
        h = h_prev
        for g in range(n_groups):
            lo = g * V7X_SUBLANES
            hg = a[lo:lo + V7X_SUBLANES, :] * h + b[lo:lo + V7X_SUBLANES, :]
            h = hg[V7X_SUBLANES - 1:V7X_SUBLANES, :]
            hs_ref[lo:lo + V7X_SUBLANES, :] = hg
        gate = gate_ref[pl.ds(t0, rows), :]
        o_ref[pl.ds(t0, rows), :] = (hs_ref[...] * _gelu_tanh(gate)).astype(o_ref.dtype)
        return h

    lax.fori_loop(0, S // rows, chunk, jnp.zeros((1, C), F32))


def _rglru(rg, cw, cb, wg, ba, bx, lam, *, batch, seq, width, out_dtype):
    C = wg.shape[1]
    nct = width // C
    rows = 128
    conv_k = cw.shape[0]
    vec = lambda: pl.BlockSpec((1, C), lambda b, c: (0, c))
    est = 2 * (2 * seq * C * 4 + seq * C * 2 + C * 2 * C * 2) + (seq + 8) * C * 4
    return pl.pallas_call(
        functools.partial(_rglru_kernel, rows=rows, conv_k=conv_k),
        grid=(batch, nct),
        in_specs=[
            pl.BlockSpec((seq, C), lambda b, c: (b, c)),
            pl.BlockSpec((seq, C), lambda b, c: (b, c + nct)),
            pl.BlockSpec((conv_k, C), lambda b, c: (0, c)),
            vec(),
            pl.BlockSpec((1, C, 2 * C), lambda b, c: (c, 0, 0)),
            vec(), vec(), vec(),
        ],
        out_specs=pl.BlockSpec((seq, C), lambda b, c: (b, c)),
        out_shape=jax.ShapeDtypeStruct((batch * seq, width), out_dtype),
        scratch_shapes=[pltpu.VMEM((seq + V7X_SUBLANES, C), F32), pltpu.VMEM((rows, C), F32)],
        compiler_params=pltpu.CompilerParams(
            dimension_semantics=("parallel", "parallel"), vmem_limit_bytes=_vmem_limit(est)),
        name="rglru",
    )(rg, rg, cw, cb, wg, ba, bx, lam)


POS_SPLIT = 256
ONES_ROWS = 16


def _attn_kernel(q_ref, k_ref, v_ref, lamv_ref, g_ref, o_ref, ka_ref, vt_ref, acc_ref, s0_ref, s1_ref,
                 *, tq, tk, n_heads, head_dim, scale, lam_init):
    h = pl.program_id(1)
    S, d2 = q_ref.shape
    V = v_ref.shape[1]
    n_kv = S // tk
    n_q = S // tq
    s_bufs = (s0_ref, s1_ref)

    def extra_cols(shape, c, lo_val, hi_val):
        lane = lax.broadcasted_iota(jnp.int32, shape, 1)
        base = head_dim * (1 - c)
        return jnp.where(lane == base, lo_val, jnp.where(lane == base + 1, hi_val, 0.0))

    def own_half(shape, c):
        lane = lax.broadcasted_iota(jnp.int32, shape, 1)
        return (lane < head_dim) if c == 0 else (lane >= head_dim)

    for jb in range(n_kv):
        rows = slice(jb * tk, (jb + 1) * tk)
        vt_ref[jb, 0:V, :] = v_ref[rows, :].astype(F32).T.astype(vt_ref.dtype)
        vt_ref[jb, V:V + ONES_ROWS, :] = jnp.ones((ONES_ROWS, tk), vt_ref.dtype)
        kb = k_ref[rows, :]
        pos = lax.broadcasted_iota(jnp.int32, (tk, d2), 0) + jb * tk
        j_lo = (pos % POS_SPLIT).astype(F32)
        j_hi = (pos // POS_SPLIT).astype(F32)
        for c in range(2):
            ka_ref[c, rows, :] = jnp.where(own_half((tk, d2), c), kb,
                                           extra_cols((tk, d2), c, j_lo, j_hi).astype(kb.dtype))

    slope = jnp.exp2(jnp.full((1, 1), -8.0 / n_heads, F32) * (h + 1).astype(F32))
    lv = lamv_ref[...]
    lam = (jnp.exp(jnp.sum(lv[0:1, :] * lv[1:2, :], axis=-1, keepdims=True))
           - jnp.exp(jnp.sum(lv[2:3, :] * lv[3:4, :], axis=-1, keepdims=True)) + lam_init)
    gain = g_ref[...] * (1.0 - lam_init)

    def scores(qi, j, s_ref):
        q = q_ref[qi * tq:(qi + 1) * tq, :] * jnp.asarray(scale, q_ref.dtype)
        for c in range(2):
            qa = jnp.where(own_half(q.shape, c), q,
                           extra_cols(q.shape, c, slope, slope * POS_SPLIT).astype(q.dtype))
            s_ref[:, c * tq:(c + 1) * tq] = lax.dot_general(
                ka_ref[c, j * tk:(j + 1) * tk, :], qa, (((1,), (1,)), ((), ())), preferred_element_type=F32)

    def softmax_pv(qi, j, s_ref, m_old):
        s = s_ref[...]
        if (j + 1) * tk - 1 > qi * tq:
            key_pos = lax.broadcasted_iota(jnp.int32, (tk, 2 * tq), 0) + j * tk
            col = lax.broadcasted_iota(jnp.int32, (tk, 2 * tq), 1)
            query_pos = jnp.where(col >= tq, col - tq, col) + qi * tq
            s = jnp.where(key_pos <= query_pos, s, MASK_VALUE)
        p_dtype = vt_ref.dtype
        if m_old is None:
            m_new = jnp.max(s, axis=0, keepdims=True)
            acc_ref[...] = jnp.dot(vt_ref[j], jnp.exp(s - m_new).astype(p_dtype), preferred_element_type=F32)
        else:
            m_new = jnp.maximum(m_old, jnp.max(s, axis=0, keepdims=True))
            alpha = jnp.exp(m_old - m_new)
            acc_ref[...] = alpha * acc_ref[...] + jnp.dot(vt_ref[j], jnp.exp(s - m_new).astype(p_dtype),
                                                          preferred_element_type=F32)
        return m_new

    def finish(qi):
        ot = (acc_ref[0:V, 0:tq] / acc_ref[V:V + 1, 0:tq]
              - lam * (acc_ref[0:V, tq:2 * tq] / acc_ref[V:V + 1, tq:2 * tq]))
        ms = jnp.mean(ot * ot, axis=0, keepdims=True)
        o = (ot * lax.rsqrt(ms + RMS_EPS)).T * gain
        o_ref[qi * tq:(qi + 1) * tq, :] = o.astype(o_ref.dtype)

    pairs = [(qi, j) for qi in range(n_q) for j in range(((qi + 1) * tq + tk - 1) // tk)]
    scores(*pairs[0], s_bufs[0])
    m_run = None
    for t, (qi, j) in enumerate(pairs):
        if t + 1 < len(pairs):
            scores(*pairs[t + 1], s_bufs[(t + 1) % 2])
        m_run = softmax_pv(qi, j, s_bufs[t % 2], m_run)
        if t + 1 == len(pairs) or pairs[t + 1][0] != qi:
            finish(qi)
            m_run = None


def _diff_attn(qkv, lamv, g, *, batch, seq, n_heads, head_dim, v_dim, lam_init, out_dtype, tq, tk):
    qk_w = 2 * head_dim
    assert qk_w == v_dim and seq % tq == 0 and seq % tk == 0 and seq % POS_SPLIT == 0
    assert seq // POS_SPLIT <= POS_SPLIT
    est = 2 * (3 * seq * qk_w * 2 + seq * v_dim * 2) + 2 * seq * qk_w * 2 \
        + seq * (v_dim + ONES_ROWS) * 2 + (v_dim + ONES_ROWS) * 2 * tq * 4 + 6 * tk * 2 * tq * 4
    return pl.pallas_call(
        functools.partial(_attn_kernel, tq=tq, tk=tk, n_heads=n_heads, head_dim=head_dim,
                          scale=head_dim ** -0.5, lam_init=lam_init),
        grid=(batch, n_heads),
        in_specs=[
            pl.BlockSpec((seq, qk_w), lambda b, h: (b, h)),
            pl.BlockSpec((seq, qk_w), lambda b, h: (b, n_heads + h)),
            pl.BlockSpec((seq, v_dim), lambda b, h: (b, 2 * n_heads + h)),
            pl.BlockSpec(lamv.shape, lambda b, h: (0, 0)),
            pl.BlockSpec((1, v_dim), lambda b, h: (0, 0)),
        ],
        out_specs=pl.BlockSpec((seq, v_dim), lambda b, h: (b, h)),
        out_shape=jax.ShapeDtypeStruct((batch * seq, n_heads * v_dim), out_dtype),
        scratch_shapes=[
            pltpu.VMEM((2, seq, qk_w), BF16),
            pltpu.VMEM((seq // tk, v_dim + ONES_ROWS, tk), BF16),
            pltpu.VMEM((v_dim + ONES_ROWS, 2 * tq), F32),
            pltpu.VMEM((tk, 2 * tq), F32),
            pltpu.VMEM((tk, 2 * tq), F32),
        ],
        compiler_params=pltpu.CompilerParams(
            dimension_semantics=("parallel", "parallel"), vmem_limit_bytes=_vmem_limit(est)),
        name="diff_attn",
    )(qkv, qkv, qkv, lamv, g)


def _residual_layernorm(acc_ref, xres_ref, g, b, o32_ref, o16_ref, *, alpha, r0, n_rows, rows):
    for c in range(n_rows // rows):
        lo = r0 + c * rows
        y = alpha * xres_ref[lo:lo + rows, :] + acc_ref[lo:lo + rows, :]
        mu = jnp.mean(y, axis=-1, keepdims=True)
        d = y - mu
        var = jnp.mean(d * d, axis=-1, keepdims=True)
        out = d * lax.rsqrt(var + LN_EPS) * g + b
        o32_ref[lo:lo + rows, :] = out
        o16_ref[lo:lo + rows, :] = out.astype(o16_ref.dtype)


LN_ROWS = 64


def _outproj_kernel(rg_ref, at_ref, w_ref, xres_ref, g_ref, b_ref, o32_ref, o16_ref, acc_ref, *, alpha, sub):
    tm, half = rg_ref.shape
    g = g_ref[...]
    b = b_ref[...]
    for t in range(tm // sub):
        r0 = t * sub
        acc_ref[r0:r0 + sub, :] = (
            jnp.dot(rg_ref[r0:r0 + sub, :], w_ref[0:half, :], preferred_element_type=F32)
            + jnp.dot(at_ref[r0:r0 + sub, :], w_ref[half:, :], preferred_element_type=F32))
        _residual_layernorm(acc_ref, xres_ref, g, b, o32_ref, o16_ref, alpha=alpha, r0=r0, n_rows=sub,
                            rows=LN_ROWS)


def _outproj_ln(rg_out, at_out, w, layer, xres, g, b, *, alpha, tm, sub):
    T, half = rg_out.shape
    _, Kw, D = w.shape
    est = 2 * (2 * tm * half * 2 + tm * D * 4 + tm * D * 4 + tm * D * 2) + Kw * D * 2 + 2 * tm * D * 4
    row = lambda width: pl.BlockSpec((tm, width), lambda m: (m, 0))
    full = lambda a: pl.BlockSpec(a.shape, lambda m: (0, 0))
    return pl.pallas_call(
        functools.partial(_outproj_kernel, alpha=alpha, sub=sub),
        grid=(T // tm,),
        in_specs=[row(half), row(half),
                  pl.BlockSpec((None, Kw, D), lambda m: (layer, 0, 0), pipeline_mode=pl.Buffered(1)),
                  row(D), full(g), full(b)],
        out_specs=[row(D), row(D)],
        out_shape=[jax.ShapeDtypeStruct((T, D), F32), jax.ShapeDtypeStruct((T, D), BF16)],
        scratch_shapes=[pltpu.VMEM((tm, D), F32)],
        compiler_params=pltpu.CompilerParams(
            dimension_semantics=("parallel",), vmem_limit_bytes=_vmem_limit(est)),
        name="outproj_ln",
    )(rg_out, at_out, w, xres, g, b)


def _ffn_up_kernel(x_ref, wg32_ref, wu32_ref, cwg_ref, cwu_ref, cbg_ref, cbu_ref, o_ref, pg_ref, pu_ref,
                   wg_ref, wu_ref, *, tiles_per_seq, conv_k, rows):
    tm, tn = o_ref.shape
    pad = V7X_SUBLANES
    m = pl.program_id(1)

    @pl.when(m == 0)
    def _():
        wg_ref[...] = wg32_ref[...].astype(wg_ref.dtype)
        wu_ref[...] = wu32_ref[...].astype(wu_ref.dtype)

    @pl.when(m % tiles_per_seq == 0)
    def _():
        pg_ref[0:pad, :] = jnp.zeros((pad, tn), F32)
        pu_ref[0:pad, :] = jnp.zeros((pad, tn), F32)

    cwg = cwg_ref[...]
    cwu = cwu_ref[...]
    cbg = cbg_ref[...]
    cbu = cbu_ref[...]
    for c in range(tm // rows):
        r0 = c * rows
        xs = x_ref[r0:r0 + rows, :]
        pg_ref[pad + r0:pad + r0 + rows, :] = jnp.dot(xs, wg_ref[...], preferred_element_type=F32)
        pu_ref[pad + r0:pad + r0 + rows, :] = jnp.dot(xs, wu_ref[...], preferred_element_type=F32)
        gv = _causal_conv_rows(pg_ref, cwg, cbg, r0, rows)
        uv = _causal_conv_rows(pu_ref, cwu, cbu, r0, rows)
        o_ref[r0:r0 + rows, :] = (_gelu_tanh(gv) * uv).astype(o_ref.dtype)
    pg_ref[0:pad, :] = pg_ref[tm:tm + pad, :]
    pu_ref[0:pad, :] = pu_ref[tm:tm + pad, :]


def _ffn_up(x, w, layer, cw, cb, *, seq, d_ff, tm, tn, out_dtype):
    T, K = x.shape
    nt = d_ff // tn
    conv_k = cw.shape[0]
    assert seq % tm == 0 and d_ff % tn == 0
    est = 2 * (tm * K * 2 + 2 * K * tn * 4 + tm * tn * 2) + 2 * K * tn * 2 + 2 * (tm + 8) * tn * 4 \
        + 2 * tm * tn * 4
    vecg = lambda r: pl.BlockSpec((r, tn), lambda n, m: (0, n))
    vecu = lambda r: pl.BlockSpec((r, tn), lambda n, m: (0, n + nt))
    return pl.pallas_call(
        functools.partial(_ffn_up_kernel, tiles_per_seq=seq // tm, conv_k=conv_k, rows=min(256, tm)),
        grid=(nt, T // tm),
        in_specs=[
            pl.BlockSpec((tm, K), lambda n, m: (m, 0)),
            pl.BlockSpec((None, K, tn), lambda n, m: (layer, 0, n)),
            pl.BlockSpec((None, K, tn), lambda n, m: (layer, 0, n + nt)),
            vecg(conv_k), vecu(conv_k), vecg(1), vecu(1),
        ],
        out_specs=pl.BlockSpec((tm, tn), lambda n, m: (m, n)),
        out_shape=jax.ShapeDtypeStruct((T, d_ff), out_dtype),
        scratch_shapes=[pltpu.VMEM((tm + V7X_SUBLANES, tn), F32), pltpu.VMEM((tm + V7X_SUBLANES, tn), F32),
                        pltpu.VMEM((K, tn), BF16), pltpu.VMEM((K, tn), BF16)],
        compiler_params=pltpu.CompilerParams(
            dimension_semantics=("parallel", "arbitrary"), vmem_limit_bytes=_vmem_limit(est)),
        name="ffn_up",
    )(x, w, w, cw, cw, cb, cb)


def _ffn_down_kernel(h_ref, w_ref, xres_ref, g_ref, b_ref, o32_ref, o16_ref, acc_ref, *, alpha):
    tm = h_ref.shape[0]
    acc_ref[...] = jnp.dot(h_ref[...], w_ref[...], preferred_element_type=F32)
    _residual_layernorm(acc_ref, xres_ref, g_ref[...], b_ref[...], o32_ref, o16_ref, alpha=alpha, r0=0,
                        n_rows=tm, rows=LN_ROWS)


def _ffn_down_ln(h, w, layer, xres, g, b, *, alpha, tm):
    T, K = h.shape
    D = w.shape[2]
    assert T % tm == 0
    est = 2 * (tm * K * 2 + tm * D * 4 + tm * D * 4 + tm * D * 2) + K * D * 2 + 2 * tm * D * 4
    row = lambda width: pl.BlockSpec((tm, width), lambda m: (m, 0))
    full = lambda a: pl.BlockSpec(a.shape, lambda m: (0, 0))
    return pl.pallas_call(
        functools.partial(_ffn_down_kernel, alpha=alpha),
        grid=(T // tm,),
        in_specs=[row(K), pl.BlockSpec((None, K, D), lambda m: (layer, 0, 0), pipeline_mode=pl.Buffered(1)),
                  row(D), full(g), full(b)],
        out_specs=[row(D), row(D)],
        out_shape=[jax.ShapeDtypeStruct((T, D), F32), jax.ShapeDtypeStruct((T, D), BF16)],
        scratch_shapes=[pltpu.VMEM((tm, D), F32)],
        compiler_params=pltpu.CompilerParams(
            dimension_semantics=("parallel",), vmem_limit_bytes=_vmem_limit(est)),
        name="ffn_down_ln",
    )(h, w, xres, g, b)


def _block_diag(w, tile):
    G, d, _ = w.shape
    per = tile // d
    w4 = w.reshape(G // per, per, d, d)
    eye = jnp.eye(per, dtype=w.dtype)
    return jnp.einsum('jipq,ik->jipkq', w4, eye).reshape(G // per, tile, tile)


def kernel(x, w_in, rg_conv_w, rg_conv_b, rg_gate_a_w, rg_gate_a_b, rg_gate_x_w, rg_gate_x_b, rg_lambda,
           lam_q1, lam_k1, lam_q2, lam_k2, subln_g, w_out, ln_mix_g, ln_mix_b, w_up, ffn_conv_w,
           ffn_conv_b, w_down, ln_ffn_g, ln_ffn_b):
    B, S, D = x.shape
    depth = w_in.shape[0]
    rg_w = rg_conv_w.shape[2]
    head_dim = lam_q1.shape[1]
    v_dim = subln_g.shape[1]
    mix_w = w_out.shape[1]
    attn_w = mix_w - rg_w
    n_heads = attn_w // v_dim
    qkv_w = w_in.shape[2] - 2 * rg_w
    d_ff = w_down.shape[1]
    T = B * S
    alpha = (2.0 * depth) ** 0.25

    x32 = x.reshape(T, D)
    x16 = x32.astype(BF16)
    w_out16 = w_out.astype(BF16)
    w_down16 = w_down.astype(BF16)
    for i in range(depth):
        wg = jnp.concatenate([_block_diag(rg_gate_a_w[i], V7X_MXU_DIM),
                              _block_diag(rg_gate_x_w[i], V7X_MXU_DIM)], axis=-1).astype(BF16)
        lam_init = 0.8 - 0.6 * math.exp(-0.3 * i)
        lamv = jnp.stack([lam_q1[i], lam_k1[i], lam_q2[i], lam_k2[i]])

        rg = _in_proj(x16, w_in, i, col_start=0, n_cols=2 * rg_w, out_dtype=F32, tm=512, tn=1024,
                      name="in_proj_rg")
        qkv = _in_proj(x16, w_in, i, col_start=2 * rg_w, n_cols=qkv_w, out_dtype=BF16, tm=512, tn=1024,
                       name="in_proj_qkv")
        rg_out = _rglru(rg, rg_conv_w[i], rg_conv_b[i][None], wg, rg_gate_a_b[i][None],
                        rg_gate_x_b[i][None], rg_lambda[i][None], batch=B, seq=S, width=rg_w,
                        out_dtype=BF16)
        at_out = _diff_attn(qkv, lamv, subln_g[i][None], batch=B, seq=S, n_heads=n_heads,
                            head_dim=head_dim, v_dim=v_dim, lam_init=lam_init, out_dtype=BF16,
                            tq=min(512, S), tk=min(512, S))
        x32, x16 = _outproj_ln(rg_out, at_out, w_out16, i, x32, ln_mix_g[i][None], ln_mix_b[i][None],
                               alpha=alpha, tm=512, sub=256)
        hmid = _ffn_up(x16, w_up, i, ffn_conv_w[i], ffn_conv_b[i][None], seq=S, d_ff=d_ff, tm=S,
                       tn=512, out_dtype=BF16)
        x32, x16 = _ffn_down_ln(hmid, w_down16, i, x32, ln_ffn_g[i][None], ln_ffn_b[i][None], alpha=alpha,
                                tm=256)
    return x32.reshape(B, S, D)
```

```python
import functools
import math

import jax
import jax.numpy as jnp
from jax import lax
from jax.experimental import pallas as pl
from jax.experimental.pallas import tpu as pltpu

F32 = jnp.float32
BF16 = jnp.bfloat16

V7X_LANES = 128
V7X_SUBLANES = 8
V7X_MXU_DIM = 256
V7X_VMEM_BYTES = 64 * 1024 * 1024

RG_C = 8.0
LN_EPS = 1e-5
RMS_EPS = 1e-5
MASK_VALUE = -1e30


def _vmem_limit(nbytes):
    return int(min(V7X_VMEM_BYTES - (4 << 20), max(nbytes + (8 << 20), 16 << 20)))


def _gelu_tanh(x):
    c = math.sqrt(2.0 / math.pi)
    return 0.5 * x * (1.0 + jnp.tanh(c * (x + 0.044715 * (x * x * x))))


def _causal_conv_rows(pad_ref, cw, cb, r0, rows):
    conv_k = cw.shape[0]
    pad = V7X_SUBLANES
    win = pad_ref[pl.ds(r0, rows + pad), :]
    out = cb + cw[conv_k - 1:conv_k, :] * win[pad:pad + rows, :]
    for k in range(conv_k - 1):
        lo = pad - (conv_k - 1) + k
        out = out + cw[k:k + 1, :] * win[lo:lo + rows, :]
    return out


def _in_proj_kernel(x_ref, w_ref, o_ref, wb_ref):
    @pl.when(pl.program_id(1) == 0)
    def _():
        wb_ref[...] = w_ref[...].astype(wb_ref.dtype)

    o_ref[...] = jnp.dot(x_ref[...], wb_ref[...], preferred_element_type=F32).astype(o_ref.dtype)


def _in_proj(x, w, layer, *, col_start, n_cols, out_dtype, tm, tn, name):
    T, K = x.shape
    assert T % tm == 0 and n_cols % tn == 0 and col_start % tn == 0
    off = col_start // tn
    est = 2 * (tm * K * 2 + K * tn * 4 + tm * tn * 4) + K * tn * 2 + tm * tn * 4
    return pl.pallas_call(
        _in_proj_kernel,
        grid=(n_cols // tn, T // tm),
        in_specs=[
            pl.BlockSpec((tm, K), lambda n, m: (m, 0)),
            pl.BlockSpec((None, K, tn), lambda n, m: (layer, 0, n + off)),
        ],
        out_specs=pl.BlockSpec((tm, tn), lambda n, m: (m, n)),
        out_shape=jax.ShapeDtypeStruct((T, n_cols), out_dtype),
        scratch_shapes=[pltpu.VMEM((K, tn), BF16)],
        compiler_params=pltpu.CompilerParams(
            dimension_semantics=("parallel", "arbitrary"), vmem_limit_bytes=_vmem_limit(est)),
        name=name,
    )(x, w)


def _rglru_qkv_kernel(x_ref, gate_ref, cw_ref, cb_ref, wg_ref, ba_ref, bx_ref, lam_ref, x16_ref, wq_ref,
                      o_ref, qkv_ref, xpad_ref, hs_ref, *, rows, n_split):
    S, C = x_ref.shape
    pad = V7X_SUBLANES
    xpad_ref[0:pad, :] = jnp.zeros((pad, C), F32)
    xpad_ref[pad:pad + S, :] = x_ref[...]

    lam = lam_ref[...]
    log_sig = jnp.minimum(lam, 0.0) - jnp.log1p(jnp.exp(-jnp.abs(lam)))
    cw = cw_ref[...]
    cb = cb_ref[...]
    ba = ba_ref[...]
    bx = bx_ref[...]
    wg = wg_ref[0]
    n_groups = rows // V7X_SUBLANES
    sub3 = lax.broadcasted_iota(jnp.int32, (n_groups, V7X_SUBLANES, C), 1)
    n_chunks = S // rows
    nq = qkv_ref.shape[1] // n_split

    def chunk(t0, h_prev):
        u = _causal_conv_rows(xpad_ref, cw, cb, t0, rows)
        pre = jnp.dot(u.astype(BF16), wg, preferred_element_type=F32)
        r = jax.nn.sigmoid(pre[:, :C] + ba)
        ig = jax.nn.sigmoid(pre[:, C:] + bx)
        log_a = RG_C * r * log_sig
        a = jnp.exp(log_a)
        th = jnp.tanh(log_a)
        m2 = -2.0 * th / (1.0 - th)
        mult = jnp.where(m2 > 0.0, m2 * lax.rsqrt(m2), 0.0)
        b = mult * (ig * u)
        a = a.reshape(n_groups, V7X_SUBLANES, C)
        b = b.reshape(n_groups, V7X_SUBLANES, C)
        for s in (1, 2, 4):
            keep = sub3 >= s
            a_sh = jnp.where(keep, pltpu.roll(a, s, axis=1), 1.0)
            b_sh = jnp.where(keep, pltpu.roll(b, s, axis=1), 0.0)
            b = a * b_sh + b
            a = a * a_sh
        a = a.reshape(rows, C)
        b = b.reshape(rows, C)
        h = h_prev
        for g in range(n_groups):
            lo = g * V7X_SUBLANES
            hg = a[lo:lo + V7X_SUBLANES, :] * h + b[lo:lo + V7X_SUBLANES, :]
            h = hg[V7X_SUBLANES - 1:V7X_SUBLANES, :]
            hs_ref[t0 + lo:t0 + lo + V7X_SUBLANES, :] = hg
        gate = gate_ref[t0:t0 + rows, :]
        o_ref[t0:t0 + rows, :] = (hs_ref[t0:t0 + rows, :] * _gelu_tanh(gate)).astype(o_ref.dtype)
        return h

    h = jnp.zeros((1, C), F32)
    part_at = {(part * n_chunks) // n_split: part for part in range(n_split)}
    for ci in range(n_chunks):
        if ci in part_at:
            part = part_at[ci]
            qkv_ref[:, part * nq:(part + 1) * nq] = jnp.dot(
                x16_ref[...], wq_ref[:, part * nq:(part + 1) * nq],
                preferred_element_type=F32).astype(qkv_ref.dtype)
        h = chunk(ci * rows, h)


def _rglru_qkv(rg, cw, cb, wg, ba, bx, lam, x16, wq, layer, *, batch, seq, width, out_dtype):
    C = wg.shape[1]
    nct = width // C
    rows = 128
    T, D = x16.shape
    Nq = wq.shape[2]
    tm = T // (batch * nct)
    conv_k = cw.shape[0]
    vec = lambda: pl.BlockSpec((1, C), lambda b, c: (0, c))
    est = 2 * (2 * seq * C * 4 + seq * C * 2 + C * 2 * C * 2 + tm * D * 2 + tm * Nq * 2) + D * Nq * 2 \
        + (2 * seq + 8) * C * 4 + tm * Nq * 4
    return pl.pallas_call(
        functools.partial(_rglru_qkv_kernel, rows=rows, n_split=3),
        grid=(batch, nct),
        in_specs=[
            pl.BlockSpec((seq, C), lambda b, c: (b, c)),
            pl.BlockSpec((seq, C), lambda b, c: (b, c + nct)),
            pl.BlockSpec((conv_k, C), lambda b, c: (0, c)),
            vec(),
            pl.BlockSpec((1, C, 2 * C), lambda b, c: (c, 0, 0)),
            vec(), vec(), vec(),
            pl.BlockSpec((tm, D), lambda b, c: (b * nct + c, 0)),
            pl.BlockSpec((None, D, Nq), lambda b, c: (layer, 0, 0), pipeline_mode=pl.Buffered(1)),
        ],
        out_specs=[pl.BlockSpec((seq, C), lambda b, c: (b, c)),
                   pl.BlockSpec((tm, Nq), lambda b, c: (b * nct + c, 0))],
        out_shape=[jax.ShapeDtypeStruct((batch * seq, width), out_dtype),
                   jax.ShapeDtypeStruct((T, Nq), BF16)],
        scratch_shapes=[pltpu.VMEM((seq + V7X_SUBLANES, C), F32), pltpu.VMEM((seq, C), F32)],
        compiler_params=pltpu.CompilerParams(
            dimension_semantics=("parallel", "parallel"), vmem_limit_bytes=_vmem_limit(est)),
        name="rglru_qkv",
    )(rg, rg, cw, cb, wg, ba, bx, lam, x16, wq)


POS_SPLIT = 256
ONES_ROWS = 16


def _attn_kernel(q_ref, k_ref, v_ref, lamv_ref, g_ref, o_ref, ka_ref, vt_ref, acc_ref, s0_ref, s1_ref,
                 *, tq, tk, n_heads, head_dim, scale, lam_init):
    h = pl.program_id(1)
    S, d2 = q_ref.shape
    V = v_ref.shape[1]
    n_kv = S // tk
    n_q = S // tq
    s_bufs = (s0_ref, s1_ref)

    def extra_cols(shape, c, lo_val, hi_val):
        lane = lax.broadcasted_iota(jnp.int32, shape, 1)
        base = head_dim * (1 - c)
        return jnp.where(lane == base, lo_val, jnp.where(lane == base + 1, hi_val, 0.0))

    def own_half(shape, c):
        lane = lax.broadcasted_iota(jnp.int32, shape, 1)
        return (lane < head_dim) if c == 0 else (lane >= head_dim)

    for jb in range(n_kv):
        rows = slice(jb * tk, (jb + 1) * tk)
        vt_ref[jb, 0:V, :] = v_ref[rows, :].astype(F32).T.astype(vt_ref.dtype)
        vt_ref[jb, V:V + ONES_ROWS, :] = jnp.ones((ONES_ROWS, tk), vt_ref.dtype)
        kb = k_ref[rows, :]
        pos = lax.broadcasted_iota(jnp.int32, (tk, d2), 0) + jb * tk
        j_lo = (pos % POS_SPLIT).astype(F32)
        j_hi = (pos // POS_SPLIT).astype(F32)
        for c in range(2):
            ka_ref[c, rows, :] = jnp.where(own_half((tk, d2), c), kb,
                                           extra_cols((tk, d2), c, j_lo, j_hi).astype(kb.dtype))

    slope = jnp.exp2(jnp.full((1, 1), -8.0 / n_heads, F32) * (h + 1).astype(F32))
    lv = lamv_ref[...]
    lam = (jnp.exp(jnp.sum(lv[0:1, :] * lv[1:2, :], axis=-1, keepdims=True))
           - jnp.exp(jnp.sum(lv[2:3, :] * lv[3:4, :], axis=-1, keepdims=True)) + lam_init)
    gain = g_ref[...] * (1.0 - lam_init)

    def scores(qi, j, s_ref):
        q = q_ref[qi * tq:(qi + 1) * tq, :] * jnp.asarray(scale, q_ref.dtype)
        for c in range(2):
            qa = jnp.where(own_half(q.shape, c), q,
                           extra_cols(q.shape, c, slope, slope * POS_SPLIT).astype(q.dtype))
            s_ref[:, c * tq:(c + 1) * tq] = lax.dot_general(
                ka_ref[c, j * tk:(j + 1) * tk, :], qa, (((1,), (1,)), ((), ())), preferred_element_type=F32)

    def softmax_pv(qi, j, s_ref, m_old):
        s = s_ref[...]
        if (j + 1) * tk - 1 > qi * tq:
            key_pos = lax.broadcasted_iota(jnp.int32, (tk, 2 * tq), 0) + j * tk
            col = lax.broadcasted_iota(jnp.int32, (tk, 2 * tq), 1)
            query_pos = jnp.where(col >= tq, col - tq, col) + qi * tq
            s = jnp.where(key_pos <= query_pos, s, MASK_VALUE)
        p_dtype = vt_ref.dtype
        if m_old is None:
            m_new = jnp.max(s, axis=0, keepdims=True)
            acc_ref[...] = jnp.dot(vt_ref[j], jnp.exp(s - m_new).astype(p_dtype), preferred_element_type=F32)
        else:
            m_new = jnp.maximum(m_old, jnp.max(s, axis=0, keepdims=True))
            alpha = jnp.exp(m_old - m_new)
            acc_ref[...] = alpha * acc_ref[...] + jnp.dot(vt_ref[j], jnp.exp(s - m_new).astype(p_dtype),
                                                          preferred_element_type=F32)
        return m_new

    def finish(qi):
        ot = (acc_ref[0:V, 0:tq] / acc_ref[V:V + 1, 0:tq]
              - lam * (acc_ref[0:V, tq:2 * tq] / acc_ref[V:V + 1, tq:2 * tq]))
        ms = jnp.mean(ot * ot, axis=0, keepdims=True)
        o = (ot * lax.rsqrt(ms + RMS_EPS)).T * gain
        o_ref[qi * tq:(qi + 1) * tq, :] = o.astype(o_ref.dtype)

    pairs = [(qi, j) for qi in range(n_q) for j in range(((qi + 1) * tq + tk - 1) // tk)]
    scores(*pairs[0], s_bufs[0])
    m_run = None
    for t, (qi, j) in enumerate(pairs):
        if t + 1 < len(pairs):
            scores(*pairs[t + 1], s_bufs[(t + 1) % 2])
        m_run = softmax_pv(qi, j, s_bufs[t % 2], m_run)
        if t + 1 == len(pairs) or pairs[t + 1][0] != qi:
            finish(qi)
            m_run = None


def _diff_attn(qkv, lamv, g, *, batch, seq, n_heads, head_dim, v_dim, lam_init, out_dtype, tq, tk):
    qk_w = 2 * head_dim
    assert qk_w == v_dim and seq % tq == 0 and seq % tk == 0 and seq % POS_SPLIT == 0
    assert seq // POS_SPLIT <= POS_SPLIT
    est = 2 * (3 * seq * qk_w * 2 + seq * v_dim * 2) + 2 * seq * qk_w * 2 \
        + seq * (v_dim + ONES_ROWS) * 2 + (v_dim + ONES_ROWS) * 2 * tq * 4 + 6 * tk * 2 * tq * 4
    return pl.pallas_call(
        functools.partial(_attn_kernel, tq=tq, tk=tk, n_heads=n_heads, head_dim=head_dim,
                          scale=head_dim ** -0.5, lam_init=lam_init),
        grid=(batch, n_heads),
        in_specs=[
            pl.BlockSpec((seq, qk_w), lambda b, h: (b, h)),
            pl.BlockSpec((seq, qk_w), lambda b, h: (b, n_heads + h)),
            pl.BlockSpec((seq, v_dim), lambda b, h: (b, 2 * n_heads + h)),
            pl.BlockSpec(lamv.shape, lambda b, h: (0, 0)),
            pl.BlockSpec((1, v_dim), lambda b, h: (0, 0)),
        ],
        out_specs=pl.BlockSpec((seq, v_dim), lambda b, h: (b, h)),
        out_shape=jax.ShapeDtypeStruct((batch * seq, n_heads * v_dim), out_dtype),
        scratch_shapes=[
            pltpu.VMEM((2, seq, qk_w), BF16),
            pltpu.VMEM((seq // tk, v_dim + ONES_ROWS, tk), BF16),
            pltpu.VMEM((v_dim + ONES_ROWS, 2 * tq), F32),
            pltpu.VMEM((tk, 2 * tq), F32),
            pltpu.VMEM((tk, 2 * tq), F32),
        ],
        compiler_params=pltpu.CompilerParams(
            dimension_semantics=("parallel", "parallel"), vmem_limit_bytes=_vmem_limit(est)),
        name="diff_attn",
    )(qkv, qkv, qkv, lamv, g)


def _residual_layernorm(acc_ref, xres_ref, g, b, o32_ref, o16_ref, *, alpha, r0, n_rows, rows):
    for c in range(n_rows // rows):
        lo = r0 + c * rows
        y = alpha * xres_ref[lo:lo + rows, :] + acc_ref[lo:lo + rows, :]
        mu = jnp.mean(y, axis=-1, keepdims=True)
        d = y - mu
        var = jnp.mean(d * d, axis=-1, keepdims=True)
        out = d * lax.rsqrt(var + LN_EPS) * g + b
        o32_ref[lo:lo + rows, :] = out
        o16_ref[lo:lo + rows, :] = out.astype(o16_ref.dtype)


LN_ROWS = 64


def _outproj_kernel(rg_ref, at_ref, w_ref, xres_ref, g_ref, b_ref, o32_ref, o16_ref, acc_ref, *, alpha, sub):
    tm, half = rg_ref.shape
    g = g_ref[...]
    b = b_ref[...]
    for t in range(tm // sub):
        r0 = t * sub
        acc_ref[r0:r0 + sub, :] = (
            jnp.dot(rg_ref[r0:r0 + sub, :], w_ref[0:half, :], preferred_element_type=F32)
            + jnp.dot(at_ref[r0:r0 + sub, :], w_ref[half:, :], preferred_element_type=F32))
        _residual_layernorm(acc_ref, xres_ref, g, b, o32_ref, o16_ref, alpha=alpha, r0=r0, n_rows=sub,
                            rows=LN_ROWS)


def _outproj_ln(rg_out, at_out, w, layer, xres, g, b, *, alpha, tm, sub):
    T, half = rg_out.shape
    _, Kw, D = w.shape
    est = 2 * (2 * tm * half * 2 + tm * D * 4 + tm * D * 4 + tm * D * 2) + Kw * D * 2 + 2 * tm * D * 4
    row = lambda width: pl.BlockSpec((tm, width), lambda m: (m, 0))
    full = lambda a: pl.BlockSpec(a.shape, lambda m: (0, 0))
    return pl.pallas_call(
        functools.partial(_outproj_kernel, alpha=alpha, sub=sub),
        grid=(T // tm,),
        in_specs=[row(half), row(half),
                  pl.BlockSpec((None, Kw, D), lambda m: (layer, 0, 0), pipeline_mode=pl.Buffered(1)),
                  row(D), full(g), full(b)],
        out_specs=[row(D), row(D)],
        out_shape=[jax.ShapeDtypeStruct((T, D), F32), jax.ShapeDtypeStruct((T, D), BF16)],
        scratch_shapes=[pltpu.VMEM((tm, D), F32)],
        compiler_params=pltpu.CompilerParams(
            dimension_semantics=("parallel",), vmem_limit_bytes=_vmem_limit(est)),
        name="outproj_ln",
    )(rg_out, at_out, w, xres, g, b)


def _ffn_up_kernel(x_ref, wg32_ref, wu32_ref, cwg_ref, cwu_ref, cbg_ref, cbu_ref, o_ref, pg_ref, pu_ref,
                   wg_ref, wu_ref, *, tiles_per_seq, conv_k, rows):
    tm, tn = o_ref.shape
    pad = V7X_SUBLANES
    m = pl.program_id(1)

    @pl.when(m == 0)
    def _():
        wg_ref[...] = wg32_ref[...].astype(wg_ref.dtype)
        wu_ref[...] = wu32_ref[...].astype(wu_ref.dtype)

    @pl.when(m % tiles_per_seq == 0)
    def _():
        pg_ref[0:pad, :] = jnp.zeros((pad, tn), F32)
        pu_ref[0:pad, :] = jnp.zeros((pad, tn), F32)

    cwg = cwg_ref[...]
    cwu = cwu_ref[...]
    cbg = cbg_ref[...]
    cbu = cbu_ref[...]
    for c in range(tm // rows):
        r0 = c * rows
        xs = x_ref[r0:r0 + rows, :]
        pg_ref[pad + r0:pad + r0 + rows, :] = jnp.dot(xs, wg_ref[...], preferred_element_type=F32)
        pu_ref[pad + r0:pad + r0 + rows, :] = jnp.dot(xs, wu_ref[...], preferred_element_type=F32)
        gv = _causal_conv_rows(pg_ref, cwg, cbg, r0, rows)
        uv = _causal_conv_rows(pu_ref, cwu, cbu, r0, rows)
        o_ref[r0:r0 + rows, :] = (_gelu_tanh(gv) * uv).astype(o_ref.dtype)
    pg_ref[0:pad, :] = pg_ref[tm:tm + pad, :]
    pu_ref[0:pad, :] = pu_ref[tm:tm + pad, :]


def _ffn_up(x, w, layer, cw, cb, *, seq, d_ff, tm, tn, out_dtype):
    T, K = x.shape
    nt = d_ff // tn
    conv_k = cw.shape[0]
    assert seq % tm == 0 and d_ff % tn == 0
    est = 2 * (tm * K * 2 + 2 * K * tn * 4 + tm * tn * 2) + 2 * K * tn * 2 + 2 * (tm + 8) * tn * 4 \
        + 2 * tm * tn * 4
    vecg = lambda r: pl.BlockSpec((r, tn), lambda n, m: (0, n))
    vecu = lambda r: pl.BlockSpec((r, tn), lambda n, m: (0, n + nt))
    return pl.pallas_call(
        functools.partial(_ffn_up_kernel, tiles_per_seq=seq // tm, conv_k=conv_k, rows=min(256, tm)),
        grid=(nt, T // tm),
        in_specs=[
            pl.BlockSpec((tm, K), lambda n, m: (m, 0)),
            pl.BlockSpec((None, K, tn), lambda n, m: (layer, 0, n)),
            pl.BlockSpec((None, K, tn), lambda n, m: (layer, 0, n + nt)),
            vecg(conv_k), vecu(conv_k), vecg(1), vecu(1),
        ],
        out_specs=pl.BlockSpec((tm, tn), lambda n, m: (m, n)),
        out_shape=jax.ShapeDtypeStruct((T, d_ff), out_dtype),
        scratch_shapes=[pltpu.VMEM((tm + V7X_SUBLANES, tn), F32), pltpu.VMEM((tm + V7X_SUBLANES, tn), F32),
                        pltpu.VMEM((K, tn), BF16), pltpu.VMEM((K, tn), BF16)],
        compiler_params=pltpu.CompilerParams(
            dimension_semantics=("parallel", "arbitrary"), vmem_limit_bytes=_vmem_limit(est)),
        name="ffn_up",
    )(x, w, w, cw, cw, cb, cb)


def _ffn_down_kernel(h_ref, w_ref, xres_ref, g_ref, b_ref, o32_ref, o16_ref, acc_ref, *, alpha):
    tm = h_ref.shape[0]
    acc_ref[...] = jnp.dot(h_ref[...], w_ref[...], preferred_element_type=F32)
    _residual_layernorm(acc_ref, xres_ref, g_ref[...], b_ref[...], o32_ref, o16_ref, alpha=alpha, r0=0,
                        n_rows=tm, rows=LN_ROWS)


def _ffn_down_ln(h, w, layer, xres, g, b, *, alpha, tm):
    T, K = h.shape
    D = w.shape[2]
    assert T % tm == 0
    est = 2 * (tm * K * 2 + tm * D * 4 + tm * D * 4 + tm * D * 2) + K * D * 2 + 2 * tm * D * 4
    row = lambda width: pl.BlockSpec((tm, width), lambda m: (m, 0))
    full = lambda a: pl.BlockSpec(a.shape, lambda m: (0, 0))
    return pl.pallas_call(
        functools.partial(_ffn_down_kernel, alpha=alpha),
        grid=(T // tm,),
        in_specs=[row(K), pl.BlockSpec((None, K, D), lambda m: (layer, 0, 0), pipeline_mode=pl.Buffered(1)),
                  row(D), full(g), full(b)],
        out_specs=[row(D), row(D)],
        out_shape=[jax.ShapeDtypeStruct((T, D), F32), jax.ShapeDtypeStruct((T, D), BF16)],
        scratch_shapes=[pltpu.VMEM((tm, D), F32)],
        compiler_params=pltpu.CompilerParams(
            dimension_semantics=("parallel",), vmem_limit_bytes=_vmem_limit(est)),
        name="ffn_down_ln",
    )(h, w, xres, g, b)


def _block_diag(w, tile):
    G, d, _ = w.shape
    per = tile // d
    w4 = w.reshape(G // per, per, d, d)
    eye = jnp.eye(per, dtype=w.dtype)
    return jnp.einsum('jipq,ik->jipkq', w4, eye).reshape(G // per, tile, tile)


def kernel(x, w_in, rg_conv_w, rg_conv_b, rg_gate_a_w, rg_gate_a_b, rg_gate_x_w, rg_gate_x_b, rg_lambda,
           lam_q1, lam_k1, lam_q2, lam_k2, subln_g, w_out, ln_mix_g, ln_mix_b, w_up, ffn_conv_w,
           ffn_conv_b, w_down, ln_ffn_g, ln_ffn_b):
    B, S, D = x.shape
    depth = w_in.shape[0]
    rg_w = rg_conv_w.shape[2]
    head_dim = lam_q1.shape[1]
    v_dim = subln_g.shape[1]
    mix_w = w_out.shape[1]
    attn_w = mix_w - rg_w
    n_heads = attn_w // v_dim
    qkv_w = w_in.shape[2] - 2 * rg_w
    d_ff = w_down.shape[1]
    T = B * S
    alpha = (2.0 * depth) ** 0.25

    x32 = x.reshape(T, D)
    x16 = x32.astype(BF16)
    w_out16 = w_out.astype(BF16)
    w_down16 = w_down.astype(BF16)
    wq16 = w_in[:, :, 2 * rg_w:].astype(BF16)
    for i in range(depth):
        wg = jnp.concatenate([_block_diag(rg_gate_a_w[i], V7X_MXU_DIM),
                              _block_diag(rg_gate_x_w[i], V7X_MXU_DIM)], axis=-1).astype(BF16)
        lam_init = 0.8 - 0.6 * math.exp(-0.3 * i)
        lamv = jnp.stack([lam_q1[i], lam_k1[i], lam_q2[i], lam_k2[i]])

        rg = _in_proj(x16, w_in, i, col_start=0, n_cols=2 * rg_w, out_dtype=F32, tm=512, tn=1024,
                      name="in_proj_rg")
        rg_out, qkv = _rglru_qkv(rg, rg_conv_w[i], rg_conv_b[i][None], wg, rg_gate_a_b[i][None],
                                 rg_gate_x_b[i][None], rg_lambda[i][None], x16, wq16, i, batch=B, seq=S,
                                 width=rg_w, out_dtype=BF16)
        at_out = _diff_attn(qkv, lamv, subln_g[i][None], batch=B, seq=S, n_heads=n_heads,
                            head_dim=head_dim, v_dim=v_dim, lam_init=lam_init, out_dtype=BF16,
                            tq=min(512, S), tk=min(512, S))
        x32, x16 = _outproj_ln(rg_out, at_out, w_out16, i, x32, ln_mix_g[i][None], ln_mix_b[i][None],
                               alpha=alpha, tm=512, sub=256)
        hmid = _ffn_up(x16, w_up, i, ffn_conv_w[i], ffn_conv_b[i][None], seq=S, d_ff=d_ff, tm=S,
                       tn=512, out_dtype=BF16)
        x32, x16 = _ffn_down_ln(hmid, w_down16, i, x32, ln_ffn_g[i][None], ln_ffn_b[i][None], alpha=alpha,
                                tm=256)
    return x32.reshape(B, S, D)
```

```python
import functools
import math

import jax
import jax.numpy as jnp
from jax import lax
from jax.experimental import pallas as pl
from jax.experimental.pallas import tpu as pltpu

F32 = jnp.float32
BF16 = jnp.bfloat16

V7X_LANES = 128
V7X_SUBLANES = 8
V7X_MXU_DIM = 256
V7X_VMEM_BYTES = 64 * 1024 * 1024

RG_C = 8.0
LN_EPS = 1e-5
RMS_EPS = 1e-5
MASK_VALUE = -1e30


def _vmem_limit(nbytes):
    return int(min(V7X_VMEM_BYTES - (4 << 20), max(nbytes + (8 << 20), 16 << 20)))


def _gelu_tanh_times(x, half_y):
    c = math.sqrt(2.0 / math.pi)
    t = jnp.tanh(x * (c + (c * 0.044715) * (x * x)))
    return (x + x * t) * half_y


def _causal_conv_rows(pad_ref, cw, cb, r0, rows):
    conv_k = cw.shape[0]
    pad = V7X_SUBLANES
    win = pad_ref[pl.ds(r0, rows + pad), :]
    out = cb + cw[conv_k - 1:conv_k, :] * win[pad:pad + rows, :]
    for k in range(conv_k - 1):
        lo = pad - (conv_k - 1) + k
        out = out + cw[k:k + 1, :] * win[lo:lo + rows, :]
    return out


def _inproj_rglru_kernel(x16_ref, wx_ref, wgt_ref, wq0_ref, wq1_ref, wq2_ref, cw_ref, cb_ref, wg_ref,
                         ba_ref, bx_ref, lam_ref, o_ref, qkv_ref, xpad_ref, gate_ref, hs_ref,
                         *, rows, sub_rows, tm):
    S, C = o_ref.shape
    pad = V7X_SUBLANES
    xpad_ref[0:pad, :] = jnp.zeros((pad, C), F32)

    lam = lam_ref[...]
    log_sig = jnp.minimum(lam, 0.0) - jnp.log1p(jnp.exp(-jnp.abs(lam)))
    cw = cw_ref[...]
    cb = cb_ref[...]
    ba = ba_ref[...]
    bx = bx_ref[...]
    wg = wg_ref[0]
    n_groups = rows // V7X_SUBLANES
    sub3 = lax.broadcasted_iota(jnp.int32, (n_groups, V7X_SUBLANES, C), 1)
    n_sub = S // sub_rows
    wq_refs = (wq0_ref, wq1_ref, wq2_ref)
    nq = wq0_ref.shape[1]

    def chunk(t0, h_prev):
        u = _causal_conv_rows(xpad_ref, cw, cb, t0, rows)
        pre = jnp.dot(u.astype(BF16), wg, preferred_element_type=F32)
        r = jax.nn.sigmoid(pre[:, :C] + ba)
        ig = jax.nn.sigmoid(pre[:, C:] + bx)
        log_a = RG_C * r * log_sig
        a = jnp.exp(log_a)
        th = jnp.tanh(log_a)
        m2 = -2.0 * th / (1.0 - th)
        mult = jnp.where(m2 > 0.0, m2 * lax.rsqrt(m2), 0.0)
        b = mult * (ig * u)
        a = a.reshape(n_groups, V7X_SUBLANES, C)
        b = b.reshape(n_groups, V7X_SUBLANES, C)
        for s in (1, 2, 4):
            keep = sub3 >= s
            a_sh = jnp.where(keep, pltpu.roll(a, s, axis=1), 1.0)
            b_sh = jnp.where(keep, pltpu.roll(b, s, axis=1), 0.0)
            b = a * b_sh + b
            a = a * a_sh
        a = a.reshape(rows, C)
        b = b.reshape(rows, C)
        h = h_prev
        for g in range(n_groups):
            lo = g * V7X_SUBLANES
            hg = a[lo:lo + V7X_SUBLANES, :] * h + b[lo:lo + V7X_SUBLANES, :]
            h = hg[V7X_SUBLANES - 1:V7X_SUBLANES, :]
            hs_ref[t0 + lo:t0 + lo + V7X_SUBLANES, :] = hg
        gate = gate_ref[t0:t0 + rows, :]
        o_ref[t0:t0 + rows, :] = _gelu_tanh_times(gate, 0.5 * hs_ref[t0:t0 + rows, :]).astype(o_ref.dtype)
        return h

    q_row0 = pl.multiple_of(pl.program_id(1) * tm, tm)
    n_parts = len(wq_refs)
    h = jnp.zeros((1, C), F32)
    for si in range(n_sub):
        for part in [p for p in range(n_parts) if (p * n_sub) // n_parts == si]:
            qkv_ref[:, part * nq:(part + 1) * nq] = jnp.dot(
                x16_ref[pl.ds(q_row0, tm), :], wq_refs[part][...],
                preferred_element_type=F32).astype(qkv_ref.dtype)
        r0 = si * sub_rows
        xs = x16_ref[r0:r0 + sub_rows, :]
        xpad_ref[pad + r0:pad + r0 + sub_rows, :] = jnp.dot(xs, wx_ref[...], preferred_element_type=F32)
        gate_ref[r0:r0 + sub_rows, :] = jnp.dot(xs, wgt_ref[...], preferred_element_type=F32)
        for ci in range(sub_rows // rows):
            h = chunk(r0 + ci * rows, h)


def _inproj_rglru(x16, w_in16, layer, cw, cb, wg, ba, bx, lam, *, batch, seq, width, qkv_width, out_dtype):
    C = wg.shape[1]
    nct = width // C
    T, D = x16.shape
    tm = seq // nct
    nq = qkv_width // 3
    q_off = 2 * width // nq
    conv_k = cw.shape[0]
    assert qkv_width % 3 == 0 and (2 * width) % nq == 0 and seq % nct == 0
    vec = lambda: pl.BlockSpec((1, C), lambda b, c: (0, c))
    wq_spec = lambda p: pl.BlockSpec((None, D, nq), lambda b, c: (layer, 0, q_off + p),
                                     pipeline_mode=pl.Buffered(1))
    est = 2 * (seq * D * 2 + 2 * D * C * 2 + seq * C * 2 + tm * qkv_width * 2 + C * 2 * C * 2) \
        + D * qkv_width * 2 + (3 * seq + 8) * C * 4 + tm * nq * 4
    return pl.pallas_call(
        functools.partial(_inproj_rglru_kernel, rows=128, sub_rows=256, tm=tm),
        grid=(batch, nct),
        in_specs=[
            pl.BlockSpec((seq, D), lambda b, c: (b, 0)),
            pl.BlockSpec((None, D, C), lambda b, c: (layer, 0, c)),
            pl.BlockSpec((None, D, C), lambda b, c: (layer, 0, nct + c)),
            wq_spec(0), wq_spec(1), wq_spec(2),
            pl.BlockSpec((conv_k, C), lambda b, c: (0, c)),
            vec(),
            pl.BlockSpec((1, C, 2 * C), lambda b, c: (c, 0, 0)),
            vec(), vec(), vec(),
        ],
        out_specs=[pl.BlockSpec((seq, C), lambda b, c: (b, c)),
                   pl.BlockSpec((tm, qkv_width), lambda b, c: (b * nct + c, 0))],
        out_shape=[jax.ShapeDtypeStruct((batch * seq, width), out_dtype),
                   jax.ShapeDtypeStruct((T, qkv_width), BF16)],
        scratch_shapes=[pltpu.VMEM((seq + V7X_SUBLANES, C), F32), pltpu.VMEM((seq, C), F32),
                        pltpu.VMEM((seq, C), F32)],
        compiler_params=pltpu.CompilerParams(
            dimension_semantics=("parallel", "parallel"), vmem_limit_bytes=_vmem_limit(est)),
        name="inproj_rglru",
    )(x16, w_in16, w_in16, w_in16, w_in16, w_in16, cw, cb, wg, ba, bx, lam)


POS_SPLIT = 256
ONES_ROWS = 16


def _attn_kernel(q_ref, k_ref, v_ref, lamv_ref, g_ref, o_ref, ka_ref, vt_ref, acc_ref, s0_ref, s1_ref,
                 *, tq, tk, n_heads, head_dim, scale, lam_init):
    h = pl.program_id(1)
    S, d2 = q_ref.shape
    V = v_ref.shape[1]
    n_kv = S // tk
    n_q = S // tq
    s_bufs = (s0_ref, s1_ref)

    def extra_cols(shape, c, lo_val, hi_val):
        lane = lax.broadcasted_iota(jnp.int32, shape, 1)
        base = head_dim * (1 - c)
        return jnp.where(lane == base, lo_val, jnp.where(lane == base + 1, hi_val, 0.0))

    def own_half(shape, c):
        lane = lax.broadcasted_iota(jnp.int32, shape, 1)
        return (lane < head_dim) if c == 0 else (lane >= head_dim)

    for jb in range(n_kv):
        rows = slice(jb * tk, (jb + 1) * tk)
        vt_ref[jb, 0:V, :] = v_ref[rows, :].astype(F32).T.astype(vt_ref.dtype)
        vt_ref[jb, V:V + ONES_ROWS, :] = jnp.ones((ONES_ROWS, tk), vt_ref.dtype)
        kb = k_ref[rows, :]
        pos = lax.broadcasted_iota(jnp.int32, (tk, d2), 0) + jb * tk
        j_lo = (pos % POS_SPLIT).astype(F32)
        j_hi = (pos // POS_SPLIT).astype(F32)
        for c in range(2):
            ka_ref[c, rows, :] = jnp.where(own_half((tk, d2), c), kb,
                                           extra_cols((tk, d2), c, j_lo, j_hi).astype(kb.dtype))

    slope = jnp.exp2(jnp.full((1, 1), -8.0 / n_heads, F32) * (h + 1).astype(F32))
    lv = lamv_ref[...]
    lam = (jnp.exp(jnp.sum(lv[0:1, :] * lv[1:2, :], axis=-1, keepdims=True))
           - jnp.exp(jnp.sum(lv[2:3, :] * lv[3:4, :], axis=-1, keepdims=True)) + lam_init)
    gain = g_ref[...] * (1.0 - lam_init)

    q_aug = {}

    def augmented_queries(qi):
        if qi not in q_aug:
            q = q_ref[qi * tq:(qi + 1) * tq, :] * jnp.asarray(scale, q_ref.dtype)
            q_aug[qi] = [jnp.where(own_half(q.shape, c), q,
                                   extra_cols(q.shape, c, slope, slope * POS_SPLIT).astype(q.dtype))
                         for c in range(2)]
        return q_aug[qi]

    def scores(qi, j, s_ref):
        qa = augmented_queries(qi)
        for c in range(2):
            s_ref[:, c * tq:(c + 1) * tq] = lax.dot_general(
                ka_ref[c, j * tk:(j + 1) * tk, :], qa[c], (((1,), (1,)), ((), ())),
                preferred_element_type=F32)

    def softmax_pv(qi, j, s_ref, m_old):
        s = s_ref[...]
        if (j + 1) * tk - 1 > qi * tq:
            key_pos = lax.broadcasted_iota(jnp.int32, (tk, 2 * tq), 0) + j * tk
            col = lax.broadcasted_iota(jnp.int32, (tk, 2 * tq), 1)
            query_pos = jnp.where(col >= tq, col - tq, col) + qi * tq
            s = jnp.where(key_pos <= query_pos, s, MASK_VALUE)
        p_dtype = vt_ref.dtype
        if m_old is None:
            m_new = jnp.max(s, axis=0, keepdims=True)
            acc_ref[...] = jnp.dot(vt_ref[j], jnp.exp(s - m_new).astype(p_dtype), preferred_element_type=F32)
        else:
            m_new = jnp.maximum(m_old, jnp.max(s, axis=0, keepdims=True))
            alpha = jnp.exp(m_old - m_new)
            acc_ref[...] = alpha * acc_ref[...] + jnp.dot(vt_ref[j], jnp.exp(s - m_new).astype(p_dtype),
                                                          preferred_element_type=F32)
        return m_new

    def finish(qi):
        ot = (acc_ref[0:V, 0:tq] / acc_ref[V:V + 1, 0:tq]
              - lam * (acc_ref[0:V, tq:2 * tq] / acc_ref[V:V + 1, tq:2 * tq]))
        ms = jnp.mean(ot * ot, axis=0, keepdims=True)
        o = (ot * lax.rsqrt(ms + RMS_EPS)).T * gain
        o_ref[qi * tq:(qi + 1) * tq, :] = o.astype(o_ref.dtype)

    pairs = [(qi, j) for qi in range(n_q) for j in range(((qi + 1) * tq + tk - 1) // tk)]
    scores(*pairs[0], s_bufs[0])
    m_run = None
    for t, (qi, j) in enumerate(pairs):
        if t + 1 < len(pairs):
            scores(*pairs[t + 1], s_bufs[(t + 1) % 2])
        m_run = softmax_pv(qi, j, s_bufs[t % 2], m_run)
        if t + 1 == len(pairs) or pairs[t + 1][0] != qi:
            finish(qi)
            m_run = None


def _diff_attn(qkv, lamv, g, *, batch, seq, n_heads, head_dim, v_dim, lam_init, out_dtype, tq, tk):
    qk_w = 2 * head_dim
    assert qk_w == v_dim and seq % tq == 0 and seq % tk == 0 and seq % POS_SPLIT == 0
    assert seq // POS_SPLIT <= POS_SPLIT
    est = 2 * (3 * seq * qk_w * 2 + seq * v_dim * 2) + 2 * seq * qk_w * 2 \
        + seq * (v_dim + ONES_ROWS) * 2 + (v_dim + ONES_ROWS) * 2 * tq * 4 + 6 * tk * 2 * tq * 4
    return pl.pallas_call(
        functools.partial(_attn_kernel, tq=tq, tk=tk, n_heads=n_heads, head_dim=head_dim,
                          scale=head_dim ** -0.5, lam_init=lam_init),
        grid=(batch, n_heads),
        in_specs=[
            pl.BlockSpec((seq, qk_w), lambda b, h: (b, h)),
            pl.BlockSpec((seq, qk_w), lambda b, h: (b, n_heads + h)),
            pl.BlockSpec((seq, v_dim), lambda b, h: (b, 2 * n_heads + h)),
            pl.BlockSpec(lamv.shape, lambda b, h: (0, 0)),
            pl.BlockSpec((1, v_dim), lambda b, h: (0, 0)),
        ],
        out_specs=pl.BlockSpec((seq, v_dim), lambda b, h: (b, h)),
        out_shape=jax.ShapeDtypeStruct((batch * seq, n_heads * v_dim), out_dtype),
        scratch_shapes=[
            pltpu.VMEM((2, seq, qk_w), BF16),
            pltpu.VMEM((seq // tk, v_dim + ONES_ROWS, tk), BF16),
            pltpu.VMEM((v_dim + ONES_ROWS, 2 * tq), F32),
            pltpu.VMEM((tk, 2 * tq), F32),
            pltpu.VMEM((tk, 2 * tq), F32),
        ],
        compiler_params=pltpu.CompilerParams(
            dimension_semantics=("parallel", "parallel"), vmem_limit_bytes=_vmem_limit(est)),
        name="diff_attn",
    )(qkv, qkv, qkv, lamv, g)


def _residual_layernorm(acc_ref, xres_ref, g, b, o32_ref, o16_ref, *, alpha, r0, n_rows, rows):
    for c in range(n_rows // rows):
        lo = r0 + c * rows
        y = alpha * xres_ref[lo:lo + rows, :] + acc_ref[lo:lo + rows, :]
        mu = jnp.mean(y, axis=-1, keepdims=True)
        d = y - mu
        var = jnp.mean(d * d, axis=-1, keepdims=True)
        out = d * lax.rsqrt(var + LN_EPS) * g + b
        o32_ref[lo:lo + rows, :] = out
        o16_ref[lo:lo + rows, :] = out.astype(o16_ref.dtype)


LN_ROWS = 64


def _outproj_kernel(rg_ref, at_ref, w_ref, xres_ref, g_ref, b_ref, o32_ref, o16_ref, acc_ref, *, alpha, sub):
    tm, half = rg_ref.shape
    g = g_ref[...]
    b = b_ref[...]
    for t in range(tm // sub):
        r0 = t * sub
        acc_ref[r0:r0 + sub, :] = (
            jnp.dot(rg_ref[r0:r0 + sub, :], w_ref[0:half, :], preferred_element_type=F32)
            + jnp.dot(at_ref[r0:r0 + sub, :], w_ref[half:, :], preferred_element_type=F32))
        _residual_layernorm(acc_ref, xres_ref, g, b, o32_ref, o16_ref, alpha=alpha, r0=r0, n_rows=sub,
                            rows=LN_ROWS)


def _outproj_ln(rg_out, at_out, w, layer, xres, g, b, *, alpha, tm, sub):
    T, half = rg_out.shape
    _, Kw, D = w.shape
    est = 2 * (2 * tm * half * 2 + tm * D * 4 + tm * D * 4 + tm * D * 2) + Kw * D * 2 + 2 * tm * D * 4
    row = lambda width: pl.BlockSpec((tm, width), lambda m: (m, 0))
    full = lambda a: pl.BlockSpec(a.shape, lambda m: (0, 0))
    return pl.pallas_call(
        functools.partial(_outproj_kernel, alpha=alpha, sub=sub),
        grid=(T // tm,),
        in_specs=[row(half), row(half),
                  pl.BlockSpec((None, Kw, D), lambda m: (layer, 0, 0), pipeline_mode=pl.Buffered(1)),
                  row(D), full(g), full(b)],
        out_specs=[row(D), row(D)],
        out_shape=[jax.ShapeDtypeStruct((T, D), F32), jax.ShapeDtypeStruct((T, D), BF16)],
        scratch_shapes=[pltpu.VMEM((tm, D), F32)],
        compiler_params=pltpu.CompilerParams(
            dimension_semantics=("parallel",), vmem_limit_bytes=_vmem_limit(est)),
        name="outproj_ln",
    )(rg_out, at_out, w, xres, g, b)


def _ffn_up_kernel(x_ref, wg32_ref, wu32_ref, cwg_ref, cwu_ref, cbg_ref, cbu_ref, o_ref, pg_ref, pu_ref,
                   wg_ref, wu_ref, *, tiles_per_seq, conv_k, rows):
    tm, tn = o_ref.shape
    pad = V7X_SUBLANES
    m = pl.program_id(1)

    @pl.when(m == 0)
    def _():
        wg_ref[...] = wg32_ref[...].astype(wg_ref.dtype)
        wu_ref[...] = wu32_ref[...].astype(wu_ref.dtype)

    @pl.when(m % tiles_per_seq == 0)
    def _():
        pg_ref[0:pad, :] = jnp.zeros((pad, tn), F32)
        pu_ref[0:pad, :] = jnp.zeros((pad, tn), F32)

    cwg = cwg_ref[...]
    cbg = cbg_ref[...]
    cwu = 0.5 * cwu_ref[...]
    cbu = 0.5 * cbu_ref[...]
    for c in range(tm // rows):
        r0 = c * rows
        xs = x_ref[r0:r0 + rows, :]
        pg_ref[pad + r0:pad + r0 + rows, :] = jnp.dot(xs, wg_ref[...], preferred_element_type=F32)
        pu_ref[pad + r0:pad + r0 + rows, :] = jnp.dot(xs, wu_ref[...], preferred_element_type=F32)
        gv = _causal_conv_rows(pg_ref, cwg, cbg, r0, rows)
        uv = _causal_conv_rows(pu_ref, cwu, cbu, r0, rows)
        o_ref[r0:r0 + rows, :] = _gelu_tanh_times(gv, uv).astype(o_ref.dtype)
    pg_ref[0:pad, :] = pg_ref[tm:tm + pad, :]
    pu_ref[0:pad, :] = pu_ref[tm:tm + pad, :]


def _ffn_up(x, w, layer, cw, cb, *, seq, d_ff, tm, tn, out_dtype):
    T, K = x.shape
    nt = d_ff // tn
    conv_k = cw.shape[0]
    assert seq % tm == 0 and d_ff % tn == 0
    est = 2 * (tm * K * 2 + 2 * K * tn * 4 + tm * tn * 2) + 2 * K * tn * 2 + 2 * (tm + 8) * tn * 4 \
        + 2 * tm * tn * 4
    vecg = lambda r: pl.BlockSpec((r, tn), lambda n, m: (0, n))
    vecu = lambda r: pl.BlockSpec((r, tn), lambda n, m: (0, n + nt))
    return pl.pallas_call(
        functools.partial(_ffn_up_kernel, tiles_per_seq=seq // tm, conv_k=conv_k, rows=min(256, tm)),
        grid=(nt, T // tm),
        in_specs=[
            pl.BlockSpec((tm, K), lambda n, m: (m, 0)),
            pl.BlockSpec((None, K, tn), lambda n, m: (layer, 0, n)),
            pl.BlockSpec((None, K, tn), lambda n, m: (layer, 0, n + nt)),
            vecg(conv_k), vecu(conv_k), vecg(1), vecu(1),
        ],
        out_specs=pl.BlockSpec((tm, tn), lambda n, m: (m, n)),
        out_shape=jax.ShapeDtypeStruct((T, d_ff), out_dtype),
        scratch_shapes=[pltpu.VMEM((tm + V7X_SUBLANES, tn), F32), pltpu.VMEM((tm + V7X_SUBLANES, tn), F32),
                        pltpu.VMEM((K, tn), BF16), pltpu.VMEM((K, tn), BF16)],
        compiler_params=pltpu.CompilerParams(
            dimension_semantics=("parallel", "arbitrary"), vmem_limit_bytes=_vmem_limit(est)),
        name="ffn_up",
    )(x, w, w, cw, cw, cb, cb)


def _ffn_down_kernel(h_ref, w_ref, xres_ref, g_ref, b_ref, o32_ref, o16_ref, acc_ref, *, alpha):
    tm = h_ref.shape[0]
    acc_ref[...] = jnp.dot(h_ref[...], w_ref[...], preferred_element_type=F32)
    _residual_layernorm(acc_ref, xres_ref, g_ref[...], b_ref[...], o32_ref, o16_ref, alpha=alpha, r0=0,
                        n_rows=tm, rows=LN_ROWS)


def _ffn_down_ln(h, w, layer, xres, g, b, *, alpha, tm):
    T, K = h.shape
    D = w.shape[2]
    assert T % tm == 0
    est = 2 * (tm * K * 2 + tm * D * 4 + tm * D * 4 + tm * D * 2) + K * D * 2 + 2 * tm * D * 4
    row = lambda width: pl.BlockSpec((tm, width), lambda m: (m, 0))
    full = lambda a: pl.BlockSpec(a.shape, lambda m: (0, 0))
    return pl.pallas_call(
        functools.partial(_ffn_down_kernel, alpha=alpha),
        grid=(T // tm,),
        in_specs=[row(K), pl.BlockSpec((None, K, D), lambda m: (layer, 0, 0), pipeline_mode=pl.Buffered(1)),
                  row(D), full(g), full(b)],
        out_specs=[row(D), row(D)],
        out_shape=[jax.ShapeDtypeStruct((T, D), F32), jax.ShapeDtypeStruct((T, D), BF16)],
        scratch_shapes=[pltpu.VMEM((tm, D), F32)],
        compiler_params=pltpu.CompilerParams(
            dimension_semantics=("parallel",), vmem_limit_bytes=_vmem_limit(est)),
        name="ffn_down_ln",
    )(h, w, xres, g, b)


def _block_diag(w, tile):
    G, d, _ = w.shape
    per = tile // d
    w4 = w.reshape(G // per, per, d, d)
    eye = jnp.eye(per, dtype=w.dtype)
    return jnp.einsum('jipq,ik->jipkq', w4, eye).reshape(G // per, tile, tile)


def kernel(x, w_in, rg_conv_w, rg_conv_b, rg_gate_a_w, rg_gate_a_b, rg_gate_x_w, rg_gate_x_b, rg_lambda,
           lam_q1, lam_k1, lam_q2, lam_k2, subln_g, w_out, ln_mix_g, ln_mix_b, w_up, ffn_conv_w,
           ffn_conv_b, w_down, ln_ffn_g, ln_ffn_b):
    B, S, D = x.shape
    depth = w_in.shape[0]
    rg_w = rg_conv_w.shape[2]
    head_dim = lam_q1.shape[1]
    v_dim = subln_g.shape[1]
    mix_w = w_out.shape[1]
    attn_w = mix_w - rg_w
    n_heads = attn_w // v_dim
    qkv_w = w_in.shape[2] - 2 * rg_w
    d_ff = w_down.shape[1]
    T = B * S
    alpha = (2.0 * depth) ** 0.25

    x32 = x.reshape(T, D)
    x16 = x32.astype(BF16)
    w_out16 = w_out.astype(BF16)
    w_down16 = w_down.astype(BF16)
    w_in16 = w_in.astype(BF16)
    for i in range(depth):
        wg = jnp.concatenate([_block_diag(rg_gate_a_w[i], V7X_MXU_DIM),
                              _block_diag(rg_gate_x_w[i], V7X_MXU_DIM)], axis=-1).astype(BF16)
        lam_init = 0.8 - 0.6 * math.exp(-0.3 * i)
        lamv = jnp.stack([lam_q1[i], lam_k1[i], lam_q2[i], lam_k2[i]])

        rg_out, qkv = _inproj_rglru(x16, w_in16, i, rg_conv_w[i], rg_conv_b[i][None], wg,
                                    rg_gate_a_b[i][None], rg_gate_x_b[i][None], rg_lambda[i][None],
                                    batch=B, seq=S, width=rg_w, qkv_width=qkv_w, out_dtype=BF16)
        at_out = _diff_attn(qkv, lamv, subln_g[i][None], batch=B, seq=S, n_heads=n_heads,
                            head_dim=head_dim, v_dim=v_dim, lam_init=lam_init, out_dtype=BF16,
                            tq=min(512, S), tk=min(512, S))
        x32, x16 = _outproj_ln(rg_out, at_out, w_out16, i, x32, ln_mix_g[i][None], ln_mix_b[i][None],
                               alpha=alpha, tm=512, sub=256)
        hmid = _ffn_up(x16, w_up, i, ffn_conv_w[i], ffn_conv_b[i][None], seq=S, d_ff=d_ff, tm=S,
                       tn=512, out_dtype=BF16)
        x32, x16 = _ffn_down_ln(hmid, w_down16, i, x32, ln_ffn_g[i][None], ln_ffn_b[i][None], alpha=alpha,
                                tm=256)
    return x32.reshape(B, S, D)
```

```python
import functools
import math

import jax
import jax.numpy as jnp
from jax import lax
from jax.experimental import pallas as pl
from jax.experimental.pallas import tpu as pltpu

F32 = jnp.float32
BF16 = jnp.bfloat16

V7X_LANES = 128
V7X_SUBLANES = 8
V7X_MXU_DIM = 256
V7X_VMEM_BYTES = 64 * 1024 * 1024

RG_C = 8.0
LN_EPS = 1e-5
RMS_EPS = 1e-5
MASK_VALUE = -1e30


def _vmem_limit(nbytes):
    return int(min(V7X_VMEM_BYTES - (4 << 20), max(nbytes + (8 << 20), 16 << 20)))


def _gelu_tanh_times(x, half_y):
    c = math.sqrt(2.0 / math.pi)
    t = jnp.tanh(x * (c + (c * 0.044715) * (x * x)))
    return (x + x * t) * half_y


def _causal_conv_rows(pad_ref, cw, cb, r0, rows):
    conv_k = cw.shape[0]
    pad = V7X_SUBLANES
    strips = []
    for s in range(pad_ref.shape[0]):
        lanes = slice(s * V7X_LANES, (s + 1) * V7X_LANES)
        out = cb[:, lanes] + cw[conv_k - 1:conv_k, lanes] * pad_ref[s, pl.ds(r0 + pad, rows), :]
        for k in range(conv_k - 1):
            start = r0 + pad - (conv_k - 1) + k
            out = out + cw[k:k + 1, lanes] * pad_ref[s, pl.ds(start, rows, stride=1), :]
        strips.append(out)
    return strips[0] if len(strips) == 1 else jnp.concatenate(strips, axis=1)


def _store_slabs(pad_ref, r0, value):
    rows = value.shape[0]
    for s in range(pad_ref.shape[0]):
        pad_ref[s, V7X_SUBLANES + r0:V7X_SUBLANES + r0 + rows, :] = value[:, s * V7X_LANES:(s + 1) * V7X_LANES]


def _inproj_rglru_kernel(x16_ref, wx_ref, wgt_ref, wq0_ref, wq1_ref, wq2_ref, cw_ref, cb_ref, wg_ref,
                         ba_ref, bx_ref, lam_ref, o_ref, qkv_ref, xpad_ref, gate_ref, hs_ref,
                         *, rows, sub_rows, tm):
    S, C = o_ref.shape
    pad = V7X_SUBLANES
    xpad_ref[:, 0:pad, :] = jnp.zeros((xpad_ref.shape[0], pad, V7X_LANES), F32)

    lam = lam_ref[...]
    log_sig = jnp.minimum(lam, 0.0) - jnp.log1p(jnp.exp(-jnp.abs(lam)))
    cw = cw_ref[...]
    cb = cb_ref[...]
    ba = ba_ref[...]
    bx = bx_ref[...]
    wg = wg_ref[0]
    n_groups = rows // V7X_SUBLANES
    sub3 = lax.broadcasted_iota(jnp.int32, (n_groups, V7X_SUBLANES, C), 1)
    n_sub = S // sub_rows
    wq_refs = (wq0_ref, wq1_ref, wq2_ref)
    nq = wq0_ref.shape[1]

    def chunk(t0, h_prev):
        u = _causal_conv_rows(xpad_ref, cw, cb, t0, rows)
        pre = jnp.dot(u.astype(BF16), wg, preferred_element_type=F32)
        r = jax.nn.sigmoid(pre[:, :C] + ba)
        ig = jax.nn.sigmoid(pre[:, C:] + bx)
        log_a = RG_C * r * log_sig
        a = jnp.exp(log_a)
        th = jnp.tanh(log_a)
        m2 = -2.0 * th / (1.0 - th)
        mult = jnp.where(m2 > 0.0, m2 * lax.rsqrt(m2), 0.0)
        b = mult * (ig * u)
        a = a.reshape(n_groups, V7X_SUBLANES, C)
        b = b.reshape(n_groups, V7X_SUBLANES, C)
        for s in (1, 2, 4):
            keep = sub3 >= s
            a_sh = jnp.where(keep, pltpu.roll(a, s, axis=1), 1.0)
            b_sh = jnp.where(keep, pltpu.roll(b, s, axis=1), 0.0)
            b = a * b_sh + b
            a = a * a_sh
        a = a.reshape(rows, C)
        b = b.reshape(rows, C)
        h = h_prev
        for g in range(n_groups):
            lo = g * V7X_SUBLANES
            hg = a[lo:lo + V7X_SUBLANES, :] * h + b[lo:lo + V7X_SUBLANES, :]
            h = hg[V7X_SUBLANES - 1:V7X_SUBLANES, :]
            hs_ref[t0 + lo:t0 + lo + V7X_SUBLANES, :] = hg
        gate = gate_ref[t0:t0 + rows, :]
        o_ref[t0:t0 + rows, :] = _gelu_tanh_times(gate, 0.5 * hs_ref[t0:t0 + rows, :]).astype(o_ref.dtype)
        return h

    q_row0 = pl.multiple_of(pl.program_id(1) * tm, tm)
    n_parts = len(wq_refs)
    h = jnp.zeros((1, C), F32)
    for si in range(n_sub):
        for part in [p for p in range(n_parts) if (p * n_sub) // n_parts == si]:
            qkv_ref[:, part * nq:(part + 1) * nq] = jnp.dot(
                x16_ref[pl.ds(q_row0, tm), :], wq_refs[part][...],
                preferred_element_type=F32).astype(qkv_ref.dtype)
        r0 = si * sub_rows
        xs = x16_ref[r0:r0 + sub_rows, :]
        _store_slabs(xpad_ref, r0, jnp.dot(xs, wx_ref[...], preferred_element_type=F32))
        gate_ref[r0:r0 + sub_rows, :] = jnp.dot(xs, wgt_ref[...], preferred_element_type=F32)
        for ci in range(sub_rows // rows):
            h = chunk(r0 + ci * rows, h)


def _inproj_rglru(x16, w_in16, layer, cw, cb, wg, ba, bx, lam, *, batch, seq, width, qkv_width, out_dtype):
    C = wg.shape[1]
    nct = width // C
    T, D = x16.shape
    tm = seq // nct
    nq = qkv_width // 3
    q_off = 2 * width // nq
    conv_k = cw.shape[0]
    assert qkv_width % 3 == 0 and (2 * width) % nq == 0 and seq % nct == 0
    vec = lambda: pl.BlockSpec((1, C), lambda b, c: (0, c))
    wq_spec = lambda p: pl.BlockSpec((None, D, nq), lambda b, c: (layer, 0, q_off + p),
                                     pipeline_mode=pl.Buffered(1))
    est = 2 * (seq * D * 2 + 2 * D * C * 2 + seq * C * 2 + tm * qkv_width * 2 + C * 2 * C * 2) \
        + D * qkv_width * 2 + (3 * seq + 8) * C * 4 + tm * nq * 4
    return pl.pallas_call(
        functools.partial(_inproj_rglru_kernel, rows=128, sub_rows=256, tm=tm),
        grid=(batch, nct),
        in_specs=[
            pl.BlockSpec((seq, D), lambda b, c: (b, 0)),
            pl.BlockSpec((None, D, C), lambda b, c: (layer, 0, c)),
            pl.BlockSpec((None, D, C), lambda b, c: (layer, 0, nct + c)),
            wq_spec(0), wq_spec(1), wq_spec(2),
            pl.BlockSpec((conv_k, C), lambda b, c: (0, c)),
            vec(),
            pl.BlockSpec((1, C, 2 * C), lambda b, c: (c, 0, 0)),
            vec(), vec(), vec(),
        ],
        out_specs=[pl.BlockSpec((seq, C), lambda b, c: (b, c)),
                   pl.BlockSpec((tm, qkv_width), lambda b, c: (b * nct + c, 0))],
        out_shape=[jax.ShapeDtypeStruct((batch * seq, width), out_dtype),
                   jax.ShapeDtypeStruct((T, qkv_width), BF16)],
        scratch_shapes=[pltpu.VMEM((C // V7X_LANES, seq + V7X_SUBLANES, V7X_LANES), F32),
                        pltpu.VMEM((seq, C), F32), pltpu.VMEM((seq, C), F32)],
        compiler_params=pltpu.CompilerParams(
            dimension_semantics=("parallel", "parallel"), vmem_limit_bytes=_vmem_limit(est)),
        name="inproj_rglru",
    )(x16, w_in16, w_in16, w_in16, w_in16, w_in16, cw, cb, wg, ba, bx, lam)


POS_SPLIT = 256
ONES_ROWS = 16


def _attn_kernel(q_ref, k_ref, v_ref, lamv_ref, g_ref, o_ref, ka_ref, vt_ref, acc_ref, s0_ref, s1_ref,
                 *, tq, tk, n_heads, head_dim, scale, lam_init):
    h = pl.program_id(1)
    S, d2 = q_ref.shape
    V = v_ref.shape[1]
    n_kv = S // tk
    n_q = S // tq
    s_bufs = (s0_ref, s1_ref)

    def extra_cols(shape, c, lo_val, hi_val):
        lane = lax.broadcasted_iota(jnp.int32, shape, 1)
        base = head_dim * (1 - c)
        return jnp.where(lane == base, lo_val, jnp.where(lane == base + 1, hi_val, 0.0))

    def own_half(shape, c):
        lane = lax.broadcasted_iota(jnp.int32, shape, 1)
        return (lane < head_dim) if c == 0 else (lane >= head_dim)

    for jb in range(n_kv):
        rows = slice(jb * tk, (jb + 1) * tk)
        vt_ref[jb, 0:V, :] = v_ref[rows, :].astype(F32).T.astype(vt_ref.dtype)
        vt_ref[jb, V:V + ONES_ROWS, :] = jnp.ones((ONES_ROWS, tk), vt_ref.dtype)
        kb = k_ref[rows, :]
        pos = lax.broadcasted_iota(jnp.int32, (tk, d2), 0) + jb * tk
        j_lo = (pos % POS_SPLIT).astype(F32)
        j_hi = (pos // POS_SPLIT).astype(F32)
        for c in range(2):
            ka_ref[c, rows, :] = jnp.where(own_half((tk, d2), c), kb,
                                           extra_cols((tk, d2), c, j_lo, j_hi).astype(kb.dtype))

    slope = jnp.exp2(jnp.full((1, 1), -8.0 / n_heads, F32) * (h + 1).astype(F32))
    lv = lamv_ref[...]
    lam = (jnp.exp(jnp.sum(lv[0:1, :] * lv[1:2, :], axis=-1, keepdims=True))
           - jnp.exp(jnp.sum(lv[2:3, :] * lv[3:4, :], axis=-1, keepdims=True)) + lam_init)
    gain = g_ref[...] * (1.0 - lam_init)

    q_aug = {}

    def augmented_queries(qi):
        if qi not in q_aug:
            q = q_ref[qi * tq:(qi + 1) * tq, :] * jnp.asarray(scale, q_ref.dtype)
            q_aug[qi] = [jnp.where(own_half(q.shape, c), q,
                                   extra_cols(q.shape, c, slope, slope * POS_SPLIT).astype(q.dtype))
                         for c in range(2)]
        return q_aug[qi]

    def scores(qi, j, s_ref):
        qa = augmented_queries(qi)
        for c in range(2):
            s_ref[:, c * tq:(c + 1) * tq] = lax.dot_general(
                ka_ref[c, j * tk:(j + 1) * tk, :], qa[c], (((1,), (1,)), ((), ())),
                preferred_element_type=F32)

    def softmax_pv(qi, j, s_ref, m_old):
        s = s_ref[...]
        if (j + 1) * tk - 1 > qi * tq:
            key_pos = lax.broadcasted_iota(jnp.int32, (tk, 2 * tq), 0) + j * tk
            col = lax.broadcasted_iota(jnp.int32, (tk, 2 * tq), 1)
            query_pos = jnp.where(col >= tq, col - tq, col) + qi * tq
            s = jnp.where(key_pos <= query_pos, s, MASK_VALUE)
        p_dtype = vt_ref.dtype
        if m_old is None:
            m_new = jnp.max(s, axis=0, keepdims=True)
            acc_ref[...] = jnp.dot(vt_ref[j], jnp.exp(s - m_new).astype(p_dtype), preferred_element_type=F32)
        else:
            m_new = jnp.maximum(m_old, jnp.max(s, axis=0, keepdims=True))
            alpha = jnp.exp(m_old - m_new)
            acc_ref[...] = alpha * acc_ref[...] + jnp.dot(vt_ref[j], jnp.exp(s - m_new).astype(p_dtype),
                                                          preferred_element_type=F32)
        return m_new

    def finish(qi):
        ot = (acc_ref[0:V, 0:tq] / acc_ref[V:V + 1, 0:tq]
              - lam * (acc_ref[0:V, tq:2 * tq] / acc_ref[V:V + 1, tq:2 * tq]))
        ms = jnp.mean(ot * ot, axis=0, keepdims=True)
        o = (ot * lax.rsqrt(ms + RMS_EPS)).T * gain
        o_ref[qi * tq:(qi + 1) * tq, :] = o.astype(o_ref.dtype)

    pairs = [(qi, j) for qi in range(n_q) for j in range(((qi + 1) * tq + tk - 1) // tk)]
    scores(*pairs[0], s_bufs[0])
    m_run = None
    for t, (qi, j) in enumerate(pairs):
        if t + 1 < len(pairs):
            scores(*pairs[t + 1], s_bufs[(t + 1) % 2])
        m_run = softmax_pv(qi, j, s_bufs[t % 2], m_run)
        if t + 1 == len(pairs) or pairs[t + 1][0] != qi:
            finish(qi)
            m_run = None


def _diff_attn(qkv, lamv, g, *, batch, seq, n_heads, head_dim, v_dim, lam_init, out_dtype, tq, tk):
    qk_w = 2 * head_dim
    assert qk_w == v_dim and seq % tq == 0 and seq % tk == 0 and seq % POS_SPLIT == 0
    assert seq // POS_SPLIT <= POS_SPLIT
    est = 2 * (3 * seq * qk_w * 2 + seq * v_dim * 2) + 2 * seq * qk_w * 2 \
        + seq * (v_dim + ONES_ROWS) * 2 + (v_dim + ONES_ROWS) * 2 * tq * 4 + 6 * tk * 2 * tq * 4
    return pl.pallas_call(
        functools.partial(_attn_kernel, tq=tq, tk=tk, n_heads=n_heads, head_dim=head_dim,
                          scale=head_dim ** -0.5, lam_init=lam_init),
        grid=(batch, n_heads),
        in_specs=[
            pl.BlockSpec((seq, qk_w), lambda b, h: (b, h)),
            pl.BlockSpec((seq, qk_w), lambda b, h: (b, n_heads + h)),
            pl.BlockSpec((seq, v_dim), lambda b, h: (b, 2 * n_heads + h)),
            pl.BlockSpec(lamv.shape, lambda b, h: (0, 0)),
            pl.BlockSpec((1, v_dim), lambda b, h: (0, 0)),
        ],
        out_specs=pl.BlockSpec((seq, v_dim), lambda b, h: (b, h)),
        out_shape=jax.ShapeDtypeStruct((batch * seq, n_heads * v_dim), out_dtype),
        scratch_shapes=[
            pltpu.VMEM((2, seq, qk_w), BF16),
            pltpu.VMEM((seq // tk, v_dim + ONES_ROWS, tk), BF16),
            pltpu.VMEM((v_dim + ONES_ROWS, 2 * tq), F32),
            pltpu.VMEM((tk, 2 * tq), F32),
            pltpu.VMEM((tk, 2 * tq), F32),
        ],
        compiler_params=pltpu.CompilerParams(
            dimension_semantics=("parallel", "parallel"), vmem_limit_bytes=_vmem_limit(est)),
        name="diff_attn",
    )(qkv, qkv, qkv, lamv, g)


def _residual_layernorm(acc_ref, xres_ref, g, b, o32_ref, o16_ref, *, alpha, r0, n_rows, rows):
    for c in range(n_rows // rows):
        lo = r0 + c * rows
        y = alpha * xres_ref[lo:lo + rows, :] + acc_ref[lo:lo + rows, :]
        mu = jnp.mean(y, axis=-1, keepdims=True)
        d = y - mu
        var = jnp.mean(d * d, axis=-1, keepdims=True)
        out = d * lax.rsqrt(var + LN_EPS) * g + b
        o32_ref[lo:lo + rows, :] = out
        o16_ref[lo:lo + rows, :] = out.astype(o16_ref.dtype)


LN_ROWS = 64


def _outproj_kernel(rg_ref, at_ref, w_ref, xres_ref, g_ref, b_ref, o32_ref, o16_ref, acc_ref, *, alpha, sub):
    tm, half = rg_ref.shape
    g = g_ref[...]
    b = b_ref[...]
    for t in range(tm // sub):
        r0 = t * sub
        acc_ref[r0:r0 + sub, :] = (
            jnp.dot(rg_ref[r0:r0 + sub, :], w_ref[0:half, :], preferred_element_type=F32)
            + jnp.dot(at_ref[r0:r0 + sub, :], w_ref[half:, :], preferred_element_type=F32))
        _residual_layernorm(acc_ref, xres_ref, g, b, o32_ref, o16_ref, alpha=alpha, r0=r0, n_rows=sub,
                            rows=LN_ROWS)


def _outproj_ln(rg_out, at_out, w, layer, xres, g, b, *, alpha, tm, sub):
    T, half = rg_out.shape
    _, Kw, D = w.shape
    est = 2 * (2 * tm * half * 2 + tm * D * 4 + tm * D * 4 + tm * D * 2) + Kw * D * 2 + 2 * tm * D * 4
    row = lambda width: pl.BlockSpec((tm, width), lambda m: (m, 0))
    full = lambda a: pl.BlockSpec(a.shape, lambda m: (0, 0))
    return pl.pallas_call(
        functools.partial(_outproj_kernel, alpha=alpha, sub=sub),
        grid=(T // tm,),
        in_specs=[row(half), row(half),
                  pl.BlockSpec((None, Kw, D), lambda m: (layer, 0, 0), pipeline_mode=pl.Buffered(1)),
                  row(D), full(g), full(b)],
        out_specs=[row(D), row(D)],
        out_shape=[jax.ShapeDtypeStruct((T, D), F32), jax.ShapeDtypeStruct((T, D), BF16)],
        scratch_shapes=[pltpu.VMEM((tm, D), F32)],
        compiler_params=pltpu.CompilerParams(
            dimension_semantics=("parallel",), vmem_limit_bytes=_vmem_limit(est)),
        name="outproj_ln",
    )(rg_out, at_out, w, xres, g, b)


def _ffn_up_kernel(x_ref, wg32_ref, wu32_ref, cwg_ref, cwu_ref, cbg_ref, cbu_ref, o_ref, pg_ref, pu_ref,
                   wg_ref, wu_ref, *, tiles_per_seq, conv_k, rows):
    tm, tn = o_ref.shape
    pad = V7X_SUBLANES
    m = pl.program_id(1)

    @pl.when(m == 0)
    def _():
        wg_ref[...] = wg32_ref[...].astype(wg_ref.dtype)
        wu_ref[...] = wu32_ref[...].astype(wu_ref.dtype)

    @pl.when(m % tiles_per_seq == 0)
    def _():
        pg_ref[:, 0:pad, :] = jnp.zeros((pg_ref.shape[0], pad, V7X_LANES), F32)
        pu_ref[:, 0:pad, :] = jnp.zeros((pu_ref.shape[0], pad, V7X_LANES), F32)

    cwg = cwg_ref[...]
    cbg = cbg_ref[...]
    cwu = 0.5 * cwu_ref[...]
    cbu = 0.5 * cbu_ref[...]
    for c in range(tm // rows):
        r0 = c * rows
        xs = x_ref[r0:r0 + rows, :]
        _store_slabs(pg_ref, r0, jnp.dot(xs, wg_ref[...], preferred_element_type=F32))
        _store_slabs(pu_ref, r0, jnp.dot(xs, wu_ref[...], preferred_element_type=F32))
        gv = _causal_conv_rows(pg_ref, cwg, cbg, r0, rows)
        uv = _causal_conv_rows(pu_ref, cwu, cbu, r0, rows)
        o_ref[r0:r0 + rows, :] = _gelu_tanh_times(gv, uv).astype(o_ref.dtype)
    pg_ref[:, 0:pad, :] = pg_ref[:, tm:tm + pad, :]
    pu_ref[:, 0:pad, :] = pu_ref[:, tm:tm + pad, :]


def _ffn_up(x, w, layer, cw, cb, *, seq, d_ff, tm, tn, out_dtype):
    T, K = x.shape
    nt = d_ff // tn
    conv_k = cw.shape[0]
    assert seq % tm == 0 and d_ff % tn == 0
    est = 2 * (tm * K * 2 + 2 * K * tn * 4 + tm * tn * 2) + 2 * K * tn * 2 + 2 * (tm + 8) * tn * 4 \
        + 2 * tm * tn * 4
    vecg = lambda r: pl.BlockSpec((r, tn), lambda n, m: (0, n))
    vecu = lambda r: pl.BlockSpec((r, tn), lambda n, m: (0, n + nt))
    return pl.pallas_call(
        functools.partial(_ffn_up_kernel, tiles_per_seq=seq // tm, conv_k=conv_k, rows=min(256, tm)),
        grid=(nt, T // tm),
        in_specs=[
            pl.BlockSpec((tm, K), lambda n, m: (m, 0)),
            pl.BlockSpec((None, K, tn), lambda n, m: (layer, 0, n)),
            pl.BlockSpec((None, K, tn), lambda n, m: (layer, 0, n + nt)),
            vecg(conv_k), vecu(conv_k), vecg(1), vecu(1),
        ],
        out_specs=pl.BlockSpec((tm, tn), lambda n, m: (m, n)),
        out_shape=jax.ShapeDtypeStruct((T, d_ff), out_dtype),
        scratch_shapes=[pltpu.VMEM((tn // V7X_LANES, tm + V7X_SUBLANES, V7X_LANES), F32),
                        pltpu.VMEM((tn // V7X_LANES, tm + V7X_SUBLANES, V7X_LANES), F32),
                        pltpu.VMEM((K, tn), BF16), pltpu.VMEM((K, tn), BF16)],
        compiler_params=pltpu.CompilerParams(
            dimension_semantics=("parallel", "arbitrary"), vmem_limit_bytes=_vmem_limit(est)),
        name="ffn_up",
    )(x, w, w, cw, cw, cb, cb)


def _ffn_down_kernel(h_ref, w_ref, xres_ref, g_ref, b_ref, o32_ref, o16_ref, acc_ref, *, alpha):
    tm = h_ref.shape[0]
    acc_ref[...] = jnp.dot(h_ref[...], w_ref[...], preferred_element_type=F32)
    _residual_layernorm(acc_ref, xres_ref, g_ref[...], b_ref[...], o32_ref, o16_ref, alpha=alpha, r0=0,
                        n_rows=tm, rows=LN_ROWS)


def _ffn_down_ln(h, w, layer, xres, g, b, *, alpha, tm):
    T, K = h.shape
    D = w.shape[2]
    assert T % tm == 0
    est = 2 * (tm * K * 2 + tm * D * 4 + tm * D * 4 + tm * D * 2) + K * D * 2 + 2 * tm * D * 4
    row = lambda width: pl.BlockSpec((tm, width), lambda m: (m, 0))
    full = lambda a: pl.BlockSpec(a.shape, lambda m: (0, 0))
    return pl.pallas_call(
        functools.partial(_ffn_down_kernel, alpha=alpha),
        grid=(T // tm,),
        in_specs=[row(K), pl.BlockSpec((None, K, D), lambda m: (layer, 0, 0), pipeline_mode=pl.Buffered(1)),
                  row(D), full(g), full(b)],
        out_specs=[row(D), row(D)],
        out_shape=[jax.ShapeDtypeStruct((T, D), F32), jax.ShapeDtypeStruct((T, D), BF16)],
        scratch_shapes=[pltpu.VMEM((tm, D), F32)],
        compiler_params=pltpu.CompilerParams(
            dimension_semantics=("parallel",), vmem_limit_bytes=_vmem_limit(est)),
        name="ffn_down_ln",
    )(h, w, xres, g, b)


def _block_diag(w, tile):
    G, d, _ = w.shape
    per = tile // d
    w4 = w.reshape(G // per, per, d, d)
    eye = jnp.eye(per, dtype=w.dtype)
    return jnp.einsum('jipq,ik->jipkq', w4, eye).reshape(G // per, tile, tile)


def kernel(x, w_in, rg_conv_w, rg_conv_b, rg_gate_a_w, rg_gate_a_b, rg_gate_x_w, rg_gate_x_b, rg_lambda,
           lam_q1, lam_k1, lam_q2, lam_k2, subln_g, w_out, ln_mix_g, ln_mix_b, w_up, ffn_conv_w,
           ffn_conv_b, w_down, ln_ffn_g, ln_ffn_b):
    B, S, D = x.shape
    depth = w_in.shape[0]
    rg_w = rg_conv_w.shape[2]
    head_dim = lam_q1.shape[1]
    v_dim = subln_g.shape[1]
    mix_w = w_out.shape[1]
    attn_w = mix_w - rg_w
    n_heads = attn_w // v_dim
    qkv_w = w_in.shape[2] - 2 * rg_w
    d_ff = w_down.shape[1]
    T = B * S
    alpha = (2.0 * depth) ** 0.25

    x32 = x.reshape(T, D)
    x16 = x32.astype(BF16)
    w_out16 = w_out.astype(BF16)
    w_down16 = w_down.astype(BF16)
    w_in16 = w_in.astype(BF16)
    for i in range(depth):
        wg = jnp.concatenate([_block_diag(rg_gate_a_w[i], V7X_MXU_DIM),
                              _block_diag(rg_gate_x_w[i], V7X_MXU_DIM)], axis=-1).astype(BF16)
        lam_init = 0.8 - 0.6 * math.exp(-0.3 * i)
        lamv = jnp.stack([lam_q1[i], lam_k1[i], lam_q2[i], lam_k2[i]])

        rg_out, qkv = _inproj_rglru(x16, w_in16, i, rg_conv_w[i], rg_conv_b[i][None], wg,
                                    rg_gate_a_b[i][None], rg_gate_x_b[i][None], rg_lambda[i][None],
                                    batch=B, seq=S, width=rg_w, qkv_width=qkv_w, out_dtype=BF16)
        at_out = _diff_attn(qkv, lamv, subln_g[i][None], batch=B, seq=S, n_heads=n_heads,
                            head_dim=head_dim, v_dim=v_dim, lam_init=lam_init, out_dtype=BF16,
                            tq=min(512, S), tk=min(512, S))
        x32, x16 = _outproj_ln(rg_out, at_out, w_out16, i, x32, ln_mix_g[i][None], ln_mix_b[i][None],
                               alpha=alpha, tm=512, sub=256)
        hmid = _ffn_up(x16, w_up, i, ffn_conv_w[i], ffn_conv_b[i][None], seq=S, d_ff=d_ff, tm=S,
                       tn=512, out_dtype=BF16)
        x32, x16 = _ffn_down_ln(hmid, w_down16, i, x32, ln_ffn_g[i][None], ln_ffn_b[i][None], alpha=alpha,
                                tm=256)
    return x32.reshape(B, S, D)
```

```python
import functools
import math

import jax
import jax.numpy as jnp
from jax import lax
from jax.experimental import pallas as pl
from jax.experimental.pallas import tpu as pltpu

F32 = jnp.float32
BF16 = jnp.bfloat16

V7X_LANES = 128
V7X_SUBLANES = 8
V7X_MXU_DIM = 256
V7X_VMEM_BYTES = 64 * 1024 * 1024

RG_C = 8.0
LN_EPS = 1e-5
RMS_EPS = 1e-5
MASK_VALUE = -1e30


def _vmem_limit(nbytes):
    return int(min(V7X_VMEM_BYTES - (4 << 20), max(nbytes + (8 << 20), 16 << 20)))


def _gelu_tanh_times(x, half_y):
    c = math.sqrt(2.0 / math.pi)
    t = jnp.tanh(x * (c + (c * 0.044715) * (x * x)))
    return (x + x * t) * half_y


def _causal_conv_rows(pad_ref, cw, cb, r0, rows):
    conv_k = cw.shape[0]
    pad = V7X_SUBLANES
    strips = []
    for s in range(pad_ref.shape[0]):
        lanes = slice(s * V7X_LANES, (s + 1) * V7X_LANES)
        out = cb[:, lanes] + cw[conv_k - 1:conv_k, lanes] * pad_ref[s, pl.ds(r0 + pad, rows), :]
        for k in range(conv_k - 1):
            start = r0 + pad - (conv_k - 1) + k
            out = out + cw[k:k + 1, lanes] * pad_ref[s, pl.ds(start, rows, stride=1), :]
        strips.append(out)
    return strips[0] if len(strips) == 1 else jnp.concatenate(strips, axis=1)


def _store_slabs(pad_ref, r0, value):
    rows = value.shape[0]
    for s in range(pad_ref.shape[0]):
        pad_ref[s, V7X_SUBLANES + r0:V7X_SUBLANES + r0 + rows, :] = value[:, s * V7X_LANES:(s + 1) * V7X_LANES]


def _inproj_rglru_kernel(x16_ref, wx32_ref, wgt32_ref, wq32a_ref, wq32b_ref, wq32c_ref, cw_ref, cb_ref, wg_ref,
                         ba_ref, bx_ref, lam_ref, o_ref, qkv_ref, xpad_ref, gate_ref, hs_ref,
                         wx_ref, wgt_ref, wq_ref, *, rows, sub_rows):
    S, C = o_ref.shape
    pad = V7X_SUBLANES
    xpad_ref[:, 0:pad, :] = jnp.zeros((xpad_ref.shape[0], pad, V7X_LANES), F32)
    wx_ref[...] = wx32_ref[...].astype(wx_ref.dtype)
    wgt_ref[...] = wgt32_ref[...].astype(wgt_ref.dtype)
    for part, ref in enumerate((wq32a_ref, wq32b_ref, wq32c_ref)):
        wq_ref[:, part * C:(part + 1) * C] = ref[...].astype(wq_ref.dtype)

    lam = lam_ref[...]
    log_sig = jnp.minimum(lam, 0.0) - jnp.log1p(jnp.exp(-jnp.abs(lam)))
    cw = cw_ref[...]
    cb = cb_ref[...]
    ba = ba_ref[...]
    bx = bx_ref[...]
    wg = wg_ref[0]
    n_groups = rows // V7X_SUBLANES
    sub3 = lax.broadcasted_iota(jnp.int32, (n_groups, V7X_SUBLANES, C), 1)
    n_sub = S // sub_rows

    def chunk(t0, h_prev):
        u = _causal_conv_rows(xpad_ref, cw, cb, t0, rows)
        pre = jnp.dot(u.astype(BF16), wg, preferred_element_type=F32)
        r = jax.nn.sigmoid(pre[:, :C] + ba)
        ig = jax.nn.sigmoid(pre[:, C:] + bx)
        log_a = RG_C * r * log_sig
        a = jnp.exp(log_a)
        th = jnp.tanh(log_a)
        m2 = -2.0 * th / (1.0 - th)
        mult = jnp.where(m2 > 0.0, m2 * lax.rsqrt(m2), 0.0)
        b = mult * (ig * u)
        a = a.reshape(n_groups, V7X_SUBLANES, C)
        b = b.reshape(n_groups, V7X_SUBLANES, C)
        for s in (1, 2, 4):
            keep = sub3 >= s
            a_sh = jnp.where(keep, pltpu.roll(a, s, axis=1), 1.0)
            b_sh = jnp.where(keep, pltpu.roll(b, s, axis=1), 0.0)
            b = a * b_sh + b
            a = a * a_sh
        a = a.reshape(rows, C)
        b = b.reshape(rows, C)
        h = h_prev
        for g in range(n_groups):
            lo = g * V7X_SUBLANES
            hg = a[lo:lo + V7X_SUBLANES, :] * h + b[lo:lo + V7X_SUBLANES, :]
            h = hg[V7X_SUBLANES - 1:V7X_SUBLANES, :]
            hs_ref[t0 + lo:t0 + lo + V7X_SUBLANES, :] = hg
        gate = gate_ref[t0:t0 + rows, :]
        o_ref[t0:t0 + rows, :] = _gelu_tanh_times(gate, 0.5 * hs_ref[t0:t0 + rows, :]).astype(o_ref.dtype)
        return h

    h = jnp.zeros((1, C), F32)
    for si in range(n_sub):
        r0 = si * sub_rows
        xs = x16_ref[r0:r0 + sub_rows, :]
        _store_slabs(xpad_ref, r0, jnp.dot(xs, wx_ref[...], preferred_element_type=F32))
        gate_ref[r0:r0 + sub_rows, :] = jnp.dot(xs, wgt_ref[...], preferred_element_type=F32)
        qkv_ref[r0:r0 + sub_rows, :] = jnp.dot(xs, wq_ref[...],
                                               preferred_element_type=F32).astype(qkv_ref.dtype)
        for ci in range(sub_rows // rows):
            h = chunk(r0 + ci * rows, h)


def _inproj_rglru(x16, w_in, layer, cw, cb, wg, ba, bx, lam, *, batch, seq, width, qkv_width, out_dtype):
    C = wg.shape[1]
    nct = width // C
    T, D = x16.shape
    nqc = qkv_width // nct
    q_parts = nqc // C
    q_off = 2 * width // C
    conv_k = cw.shape[0]
    assert qkv_width % nct == 0 and nqc == 3 * C
    vec = lambda: pl.BlockSpec((1, C), lambda b, c: (0, c))
    wq_spec = lambda part: pl.BlockSpec((None, D, C), lambda b, c: (layer, 0, q_off + q_parts * c + part))
    est = seq * D * 2 + 2 * (2 * D * C * 4 + D * nqc * 4 + seq * C * 2 + seq * nqc * 2 + C * 2 * C * 2) \
        + 2 * D * C * 2 + D * nqc * 2 + (3 * seq + 8) * C * 4 + 256 * nqc * 4
    return pl.pallas_call(
        functools.partial(_inproj_rglru_kernel, rows=128, sub_rows=256),
        grid=(batch, nct),
        in_specs=[
            pl.BlockSpec((seq, D), lambda b, c: (b, 0), pipeline_mode=pl.Buffered(1)),
            pl.BlockSpec((None, D, C), lambda b, c: (layer, 0, c)),
            pl.BlockSpec((None, D, C), lambda b, c: (layer, 0, nct + c)),
            wq_spec(0), wq_spec(1), wq_spec(2),
            pl.BlockSpec((conv_k, C), lambda b, c: (0, c)),
            vec(),
            pl.BlockSpec((1, C, 2 * C), lambda b, c: (c, 0, 0)),
            vec(), vec(), vec(),
        ],
        out_specs=[pl.BlockSpec((seq, C), lambda b, c: (b, c)),
                   pl.BlockSpec((seq, nqc), lambda b, c: (b, c))],
        out_shape=[jax.ShapeDtypeStruct((batch * seq, width), out_dtype),
                   jax.ShapeDtypeStruct((T, qkv_width), BF16)],
        scratch_shapes=[pltpu.VMEM((C // V7X_LANES, seq + V7X_SUBLANES, V7X_LANES), F32),
                        pltpu.VMEM((seq, C), F32), pltpu.VMEM((seq, C), F32),
                        pltpu.VMEM((D, C), BF16), pltpu.VMEM((D, C), BF16), pltpu.VMEM((D, nqc), BF16)],
        compiler_params=pltpu.CompilerParams(
            dimension_semantics=("parallel", "parallel"), vmem_limit_bytes=_vmem_limit(est)),
        name="inproj_rglru",
    )(x16, w_in, w_in, w_in, w_in, w_in, cw, cb, wg, ba, bx, lam)


POS_SPLIT = 256
ONES_ROWS = 16


def _attn_kernel(q_ref, k_ref, v_ref, lamv_ref, g_ref, o_ref, ka_ref, vt_ref, acc_ref, s0_ref, s1_ref,
                 *, tq, tk, n_heads, head_dim, scale, lam_init):
    h = pl.program_id(1)
    S, d2 = q_ref.shape
    V = v_ref.shape[1]
    n_kv = S // tk
    n_q = S // tq
    s_bufs = (s0_ref, s1_ref)

    def extra_cols(shape, c, lo_val, hi_val):
        lane = lax.broadcasted_iota(jnp.int32, shape, 1)
        base = head_dim * (1 - c)
        return jnp.where(lane == base, lo_val, jnp.where(lane == base + 1, hi_val, 0.0))

    def own_half(shape, c):
        lane = lax.broadcasted_iota(jnp.int32, shape, 1)
        return (lane < head_dim) if c == 0 else (lane >= head_dim)

    for jb in range(n_kv):
        rows = slice(jb * tk, (jb + 1) * tk)
        vt_ref[jb, 0:V, :] = v_ref[rows, :].astype(F32).T.astype(vt_ref.dtype)
        vt_ref[jb, V:V + ONES_ROWS, :] = jnp.ones((ONES_ROWS, tk), vt_ref.dtype)
        kb = k_ref[rows, :]
        pos = lax.broadcasted_iota(jnp.int32, (tk, d2), 0) + jb * tk
        j_lo = (pos % POS_SPLIT).astype(F32)
        j_hi = (pos // POS_SPLIT).astype(F32)
        for c in range(2):
            ka_ref[c, rows, :] = jnp.where(own_half((tk, d2), c), kb,
                                           extra_cols((tk, d2), c, j_lo, j_hi).astype(kb.dtype))

    slope = jnp.exp2(jnp.full((1, 1), -8.0 / n_heads, F32) * (h + 1).astype(F32))
    lv = lamv_ref[...]
    lam = (jnp.exp(jnp.sum(lv[0:1, :] * lv[1:2, :], axis=-1, keepdims=True))
           - jnp.exp(jnp.sum(lv[2:3, :] * lv[3:4, :], axis=-1, keepdims=True)) + lam_init)
    gain = g_ref[...] * (1.0 - lam_init)

    q_aug = {}

    def augmented_queries(qi):
        if qi not in q_aug:
            q = q_ref[qi * tq:(qi + 1) * tq, :] * jnp.asarray(scale, q_ref.dtype)
            q_aug[qi] = [jnp.where(own_half(q.shape, c), q,
                                   extra_cols(q.shape, c, slope, slope * POS_SPLIT).astype(q.dtype))
                         for c in range(2)]
        return q_aug[qi]

    def scores(qi, j, s_ref):
        qa = augmented_queries(qi)
        for c in range(2):
            s_ref[:, c * tq:(c + 1) * tq] = lax.dot_general(
                ka_ref[c, j * tk:(j + 1) * tk, :], qa[c], (((1,), (1,)), ((), ())),
                preferred_element_type=F32)

    def softmax_pv(qi, j, s_ref, m_old):
        s = s_ref[...]
        if (j + 1) * tk - 1 > qi * tq:
            key_pos = lax.broadcasted_iota(jnp.int32, (tk, 2 * tq), 0) + j * tk
            col = lax.broadcasted_iota(jnp.int32, (tk, 2 * tq), 1)
            query_pos = jnp.where(col >= tq, col - tq, col) + qi * tq
            s = jnp.where(key_pos <= query_pos, s, MASK_VALUE)
        p_dtype = vt_ref.dtype
        if m_old is None:
            m_new = jnp.max(s, axis=0, keepdims=True)
            acc_ref[...] = jnp.dot(vt_ref[j], jnp.exp(s - m_new).astype(p_dtype), preferred_element_type=F32)
        else:
            m_new = jnp.maximum(m_old, jnp.max(s, axis=0, keepdims=True))
            alpha = jnp.exp(m_old - m_new)
            acc_ref[...] = alpha * acc_ref[...] + jnp.dot(vt_ref[j], jnp.exp(s - m_new).astype(p_dtype),
                                                          preferred_element_type=F32)
        return m_new

    def finish(qi):
        ot = (acc_ref[0:V, 0:tq] / acc_ref[V:V + 1, 0:tq]
              - lam * (acc_ref[0:V, tq:2 * tq] / acc_ref[V:V + 1, tq:2 * tq]))
        ms = jnp.mean(ot * ot, axis=0, keepdims=True)
        o = (ot * lax.rsqrt(ms + RMS_EPS)).T * gain
        o_ref[qi * tq:(qi + 1) * tq, :] = o.astype(o_ref.dtype)

    pairs = [(qi, j) for qi in range(n_q) for j in range(((qi + 1) * tq + tk - 1) // tk)]
    scores(*pairs[0], s_bufs[0])
    m_run = None
    for t, (qi, j) in enumerate(pairs):
        if t + 1 < len(pairs):
            scores(*pairs[t + 1], s_bufs[(t + 1) % 2])
        m_run = softmax_pv(qi, j, s_bufs[t % 2], m_run)
        if t + 1 == len(pairs) or pairs[t + 1][0] != qi:
            finish(qi)
            m_run = None


def _diff_attn(qkv, lamv, g, *, batch, seq, n_heads, head_dim, v_dim, lam_init, out_dtype, tq, tk):
    qk_w = 2 * head_dim
    assert qk_w == v_dim and seq % tq == 0 and seq % tk == 0 and seq % POS_SPLIT == 0
    assert seq // POS_SPLIT <= POS_SPLIT
    est = 2 * (3 * seq * qk_w * 2 + seq * v_dim * 2) + 2 * seq * qk_w * 2 \
        + seq * (v_dim + ONES_ROWS) * 2 + (v_dim + ONES_ROWS) * 2 * tq * 4 + 6 * tk * 2 * tq * 4
    return pl.pallas_call(
        functools.partial(_attn_kernel, tq=tq, tk=tk, n_heads=n_heads, head_dim=head_dim,
                          scale=head_dim ** -0.5, lam_init=lam_init),
        grid=(batch, n_heads),
        in_specs=[
            pl.BlockSpec((seq, qk_w), lambda b, h: (b, h)),
            pl.BlockSpec((seq, qk_w), lambda b, h: (b, n_heads + h)),
            pl.BlockSpec((seq, v_dim), lambda b, h: (b, 2 * n_heads + h)),
            pl.BlockSpec(lamv.shape, lambda b, h: (0, 0)),
            pl.BlockSpec((1, v_dim), lambda b, h: (0, 0)),
        ],
        out_specs=pl.BlockSpec((seq, v_dim), lambda b, h: (b, h)),
        out_shape=jax.ShapeDtypeStruct((batch * seq, n_heads * v_dim), out_dtype),
        scratch_shapes=[
            pltpu.VMEM((2, seq, qk_w), BF16),
            pltpu.VMEM((seq // tk, v_dim + ONES_ROWS, tk), BF16),
            pltpu.VMEM((v_dim + ONES_ROWS, 2 * tq), F32),
            pltpu.VMEM((tk, 2 * tq), F32),
            pltpu.VMEM((tk, 2 * tq), F32),
        ],
        compiler_params=pltpu.CompilerParams(
            dimension_semantics=("parallel", "parallel"), vmem_limit_bytes=_vmem_limit(est)),
        name="diff_attn",
    )(qkv, qkv, qkv, lamv, g)


def _residual_layernorm(acc_ref, xres_ref, g, b, o32_ref, o16_ref, *, alpha, r0, n_rows, rows):
    for c in range(n_rows // rows):
        lo = r0 + c * rows
        y = alpha * xres_ref[lo:lo + rows, :] + acc_ref[lo:lo + rows, :]
        mu = jnp.mean(y, axis=-1, keepdims=True)
        d = y - mu
        var = jnp.mean(d * d, axis=-1, keepdims=True)
        out = d * lax.rsqrt(var + LN_EPS) * g + b
        o32_ref[lo:lo + rows, :] = out
        o16_ref[lo:lo + rows, :] = out.astype(o16_ref.dtype)


LN_ROWS = 64


def _outproj_kernel(rg_ref, at_ref, w_ref, xres_ref, g_ref, b_ref, o32_ref, o16_ref, acc_ref, *, alpha, sub):
    tm, half = rg_ref.shape
    g = g_ref[...]
    b = b_ref[...]
    for t in range(tm // sub):
        r0 = t * sub
        acc_ref[r0:r0 + sub, :] = (
            jnp.dot(rg_ref[r0:r0 + sub, :], w_ref[0:half, :], preferred_element_type=F32)
            + jnp.dot(at_ref[r0:r0 + sub, :], w_ref[half:, :], preferred_element_type=F32))
        _residual_layernorm(acc_ref, xres_ref, g, b, o32_ref, o16_ref, alpha=alpha, r0=r0, n_rows=sub,
                            rows=LN_ROWS)


def _outproj_ln(rg_out, at_out, w, layer, xres, g, b, *, alpha, tm, sub):
    T, half = rg_out.shape
    _, Kw, D = w.shape
    est = 2 * (2 * tm * half * 2 + tm * D * 4 + tm * D * 4 + tm * D * 2) + Kw * D * 2 + 2 * tm * D * 4
    row = lambda width: pl.BlockSpec((tm, width), lambda m: (m, 0))
    full = lambda a: pl.BlockSpec(a.shape, lambda m: (0, 0))
    return pl.pallas_call(
        functools.partial(_outproj_kernel, alpha=alpha, sub=sub),
        grid=(T // tm,),
        in_specs=[row(half), row(half),
                  pl.BlockSpec((None, Kw, D), lambda m: (layer, 0, 0), pipeline_mode=pl.Buffered(1)),
                  row(D), full(g), full(b)],
        out_specs=[row(D), row(D)],
        out_shape=[jax.ShapeDtypeStruct((T, D), F32), jax.ShapeDtypeStruct((T, D), BF16)],
        scratch_shapes=[pltpu.VMEM((tm, D), F32)],
        compiler_params=pltpu.CompilerParams(
            dimension_semantics=("parallel",), vmem_limit_bytes=_vmem_limit(est)),
        name="outproj_ln",
    )(rg_out, at_out, w, xres, g, b)


def _ffn_up_kernel(x_ref, wg32_ref, wu32_ref, cwg_ref, cwu_ref, cbg_ref, cbu_ref, o_ref, pg_ref, pu_ref,
                   wg_ref, wu_ref, *, tiles_per_seq, conv_k, rows):
    tm, tn = o_ref.shape
    pad = V7X_SUBLANES
    m = pl.program_id(1)

    @pl.when(m == 0)
    def _():
        wg_ref[...] = wg32_ref[...].astype(wg_ref.dtype)
        wu_ref[...] = wu32_ref[...].astype(wu_ref.dtype)

    @pl.when(m % tiles_per_seq == 0)
    def _():
        pg_ref[:, 0:pad, :] = jnp.zeros((pg_ref.shape[0], pad, V7X_LANES), F32)
        pu_ref[:, 0:pad, :] = jnp.zeros((pu_ref.shape[0], pad, V7X_LANES), F32)

    cwg = cwg_ref[...]
    cbg = cbg_ref[...]
    cwu = 0.5 * cwu_ref[...]
    cbu = 0.5 * cbu_ref[...]
    for c in range(tm // rows):
        r0 = c * rows
        xs = x_ref[r0:r0 + rows, :]
        _store_slabs(pg_ref, r0, jnp.dot(xs, wg_ref[...], preferred_element_type=F32))
        _store_slabs(pu_ref, r0, jnp.dot(xs, wu_ref[...], preferred_element_type=F32))
        gv = _causal_conv_rows(pg_ref, cwg, cbg, r0, rows)
        uv = _causal_conv_rows(pu_ref, cwu, cbu, r0, rows)
        o_ref[r0:r0 + rows, :] = _gelu_tanh_times(gv, uv).astype(o_ref.dtype)
    pg_ref[:, 0:pad, :] = pg_ref[:, tm:tm + pad, :]
    pu_ref[:, 0:pad, :] = pu_ref[:, tm:tm + pad, :]


def _ffn_up(x, w, layer, cw, cb, *, seq, d_ff, tm, tn, out_dtype):
    T, K = x.shape
    nt = d_ff // tn
    conv_k = cw.shape[0]
    assert seq % tm == 0 and d_ff % tn == 0
    est = 2 * (tm * K * 2 + 2 * K * tn * 4 + tm * tn * 2) + 2 * K * tn * 2 + 2 * (tm + 8) * tn * 4 \
        + 2 * tm * tn * 4
    vecg = lambda r: pl.BlockSpec((r, tn), lambda n, m: (0, n))
    vecu = lambda r: pl.BlockSpec((r, tn), lambda n, m: (0, n + nt))
    return pl.pallas_call(
        functools.partial(_ffn_up_kernel, tiles_per_seq=seq // tm, conv_k=conv_k, rows=min(256, tm)),
        grid=(nt, T // tm),
        in_specs=[
            pl.BlockSpec((tm, K), lambda n, m: (m, 0)),
            pl.BlockSpec((None, K, tn), lambda n, m: (layer, 0, n)),
            pl.BlockSpec((None, K, tn), lambda n, m: (layer, 0, n + nt)),
            vecg(conv_k), vecu(conv_k), vecg(1), vecu(1),
        ],
        out_specs=pl.BlockSpec((tm, tn), lambda n, m: (m, n)),
        out_shape=jax.ShapeDtypeStruct((T, d_ff), out_dtype),
        scratch_shapes=[pltpu.VMEM((tn // V7X_LANES, tm + V7X_SUBLANES, V7X_LANES), F32),
                        pltpu.VMEM((tn // V7X_LANES, tm + V7X_SUBLANES, V7X_LANES), F32),
                        pltpu.VMEM((K, tn), BF16), pltpu.VMEM((K, tn), BF16)],
        compiler_params=pltpu.CompilerParams(
            dimension_semantics=("parallel", "arbitrary"), vmem_limit_bytes=_vmem_limit(est)),
        name="ffn_up",
    )(x, w, w, cw, cw, cb, cb)


def _ffn_down_kernel(h_ref, w_ref, xres_ref, g_ref, b_ref, o32_ref, o16_ref, acc_ref, *, alpha):
    tm = h_ref.shape[0]
    acc_ref[...] = jnp.dot(h_ref[...], w_ref[...], preferred_element_type=F32)
    _residual_layernorm(acc_ref, xres_ref, g_ref[...], b_ref[...], o32_ref, o16_ref, alpha=alpha, r0=0,
                        n_rows=tm, rows=LN_ROWS)


def _ffn_down_ln(h, w, layer, xres, g, b, *, alpha, tm):
    T, K = h.shape
    D = w.shape[2]
    assert T % tm == 0
    est = 2 * (tm * K * 2 + tm * D * 4 + tm * D * 4 + tm * D * 2) + K * D * 2 + 2 * tm * D * 4
    row = lambda width: pl.BlockSpec((tm, width), lambda m: (m, 0))
    full = lambda a: pl.BlockSpec(a.shape, lambda m: (0, 0))
    return pl.pallas_call(
        functools.partial(_ffn_down_kernel, alpha=alpha),
        grid=(T // tm,),
        in_specs=[row(K), pl.BlockSpec((None, K, D), lambda m: (layer, 0, 0), pipeline_mode=pl.Buffered(1)),
                  row(D), full(g), full(b)],
        out_specs=[row(D), row(D)],
        out_shape=[jax.ShapeDtypeStruct((T, D), F32), jax.ShapeDtypeStruct((T, D), BF16)],
        scratch_shapes=[pltpu.VMEM((tm, D), F32)],
        compiler_params=pltpu.CompilerParams(
            dimension_semantics=("parallel",), vmem_limit_bytes=_vmem_limit(est)),
        name="ffn_down_ln",
    )(h, w, xres, g, b)


def _block_diag(w, tile):
    G, d, _ = w.shape
    per = tile // d
    w4 = w.reshape(G // per, per, d, d)
    eye = jnp.eye(per, dtype=w.dtype)
    return jnp.einsum('jipq,ik->jipkq', w4, eye).reshape(G // per, tile, tile)


def kernel(x, w_in, rg_conv_w, rg_conv_b, rg_gate_a_w, rg_gate_a_b, rg_gate_x_w, rg_gate_x_b, rg_lambda,
           lam_q1, lam_k1, lam_q2, lam_k2, subln_g, w_out, ln_mix_g, ln_mix_b, w_up, ffn_conv_w,
           ffn_conv_b, w_down, ln_ffn_g, ln_ffn_b):
    B, S, D = x.shape
    depth = w_in.shape[0]
    rg_w = rg_conv_w.shape[2]
    head_dim = lam_q1.shape[1]
    v_dim = subln_g.shape[1]
    mix_w = w_out.shape[1]
    attn_w = mix_w - rg_w
    n_heads = attn_w // v_dim
    qkv_w = w_in.shape[2] - 2 * rg_w
    d_ff = w_down.shape[1]
    T = B * S
    alpha = (2.0 * depth) ** 0.25

    x32 = x.reshape(T, D)
    x16 = x32.astype(BF16)
    w_out16 = w_out.astype(BF16)
    w_down16 = w_down.astype(BF16)
    for i in range(depth):
        wg = jnp.concatenate([_block_diag(rg_gate_a_w[i], V7X_MXU_DIM),
                              _block_diag(rg_gate_x_w[i], V7X_MXU_DIM)], axis=-1).astype(BF16)
        lam_init = 0.8 - 0.6 * math.exp(-0.3 * i)
        lamv = jnp.stack([lam_q1[i], lam_k1[i], lam_q2[i], lam_k2[i]])

        rg_out, qkv = _inproj_rglru(x16, w_in, i, rg_conv_w[i], rg_conv_b[i][None], wg,
                                    rg_gate_a_b[i][None], rg_gate_x_b[i][None], rg_lambda[i][None],
                                    batch=B, seq=S, width=rg_w, qkv_width=qkv_w, out_dtype=BF16)
        at_out = _diff_attn(qkv, lamv, subln_g[i][None], batch=B, seq=S, n_heads=n_heads,
                            head_dim=head_dim, v_dim=v_dim, lam_init=lam_init, out_dtype=BF16,
                            tq=min(512, S), tk=min(512, S))
        x32, x16 = _outproj_ln(rg_out, at_out, w_out16, i, x32, ln_mix_g[i][None], ln_mix_b[i][None],
                               alpha=alpha, tm=512, sub=256)
        hmid = _ffn_up(x16, w_up, i, ffn_conv_w[i], ffn_conv_b[i][None], seq=S, d_ff=d_ff, tm=S,
                       tn=512, out_dtype=BF16)
        x32, x16 = _ffn_down_ln(hmid, w_down16, i, x32, ln_ffn_g[i][None], ln_ffn_b[i][None], alpha=alpha,
                                tm=256)
    return x32.reshape(B, S, D)
```

```python
import functools
import math

import jax
import jax.numpy as jnp
from jax import lax
from jax.experimental import pallas as pl
from jax.experimental.pallas import tpu as pltpu

F32 = jnp.float32
BF16 = jnp.bfloat16

V7X_LANES = 128
V7X_SUBLANES = 8
V7X_MXU_DIM = 256
V7X_VMEM_BYTES = 64 * 1024 * 1024

RG_C = 8.0
LN_EPS = 1e-5
RMS_EPS = 1e-5
MASK_VALUE = -1e30


def _vmem_limit(nbytes):
    return int(min(V7X_VMEM_BYTES - (4 << 20), max(nbytes + (8 << 20), 16 << 20)))


def _gelu_tanh_times(x, half_y):
    c = math.sqrt(2.0 / math.pi)
    t = jnp.tanh(x * (c + (c * 0.044715) * (x * x)))
    return (x + x * t) * half_y


def _causal_conv_rows(pad_ref, cw, cb, r0, rows):
    conv_k = cw.shape[0]
    pad = V7X_SUBLANES
    strips = []
    for s in range(pad_ref.shape[0]):
        lanes = slice(s * V7X_LANES, (s + 1) * V7X_LANES)
        out = cb[:, lanes] + cw[conv_k - 1:conv_k, lanes] * pad_ref[s, pl.ds(r0 + pad, rows), :]
        for k in range(conv_k - 1):
            start = r0 + pad - (conv_k - 1) + k
            out = out + cw[k:k + 1, lanes] * pad_ref[s, pl.ds(start, rows, stride=1), :]
        strips.append(out)
    return strips[0] if len(strips) == 1 else jnp.concatenate(strips, axis=1)


def _store_slabs(pad_ref, r0, value):
    rows = value.shape[0]
    for s in range(pad_ref.shape[0]):
        pad_ref[s, V7X_SUBLANES + r0:V7X_SUBLANES + r0 + rows, :] = value[:, s * V7X_LANES:(s + 1) * V7X_LANES]


def _inproj_rglru_kernel(x16_ref, wx32_ref, wgt32_ref, wq32a_ref, wq32b_ref, wq32c_ref, cw_ref, cb_ref, wg_ref,
                         ba_ref, bx_ref, lam_ref, o_ref, qkv_ref, xpad_ref, gate_ref, hs_ref,
                         wx_ref, wgt_ref, wq_ref, *, rows, sub_rows):
    S, C = o_ref.shape
    pad = V7X_SUBLANES
    xpad_ref[:, 0:pad, :] = jnp.zeros((xpad_ref.shape[0], pad, V7X_LANES), F32)
    wx_ref[...] = wx32_ref[...].astype(wx_ref.dtype)
    wgt_ref[...] = wgt32_ref[...].astype(wgt_ref.dtype)
    for part, ref in enumerate((wq32a_ref, wq32b_ref, wq32c_ref)):
        wq_ref[:, part * C:(part + 1) * C] = ref[...].astype(wq_ref.dtype)

    lam = lam_ref[...]
    log_sig = jnp.minimum(lam, 0.0) - jnp.log1p(jnp.exp(-jnp.abs(lam)))
    cw = cw_ref[...]
    cb = cb_ref[...]
    ba = ba_ref[...]
    bx = bx_ref[...]
    wg = wg_ref[0]
    n_groups = rows // V7X_SUBLANES
    sub3 = lax.broadcasted_iota(jnp.int32, (n_groups, V7X_SUBLANES, C), 1)
    n_sub = S // sub_rows

    def chunk(t0, h_prev):
        u = _causal_conv_rows(xpad_ref, cw, cb, t0, rows)
        pre = jnp.dot(u.astype(BF16), wg, preferred_element_type=F32)
        r = jax.nn.sigmoid(pre[:, :C] + ba)
        ig = jax.nn.sigmoid(pre[:, C:] + bx)
        log_a = RG_C * r * log_sig
        a = jnp.exp(log_a)
        th = jnp.tanh(log_a)
        m2 = -2.0 * th / (1.0 - th)
        mult = jnp.where(m2 > 0.0, m2 * lax.rsqrt(m2), 0.0)
        b = mult * (ig * u)
        a = a.reshape(n_groups, V7X_SUBLANES, C)
        b = b.reshape(n_groups, V7X_SUBLANES, C)
        for s in (1, 2, 4):
            keep = sub3 >= s
            a_sh = jnp.where(keep, pltpu.roll(a, s, axis=1), 1.0)
            b_sh = jnp.where(keep, pltpu.roll(b, s, axis=1), 0.0)
            b = a * b_sh + b
            a = a * a_sh
        a = a.reshape(rows, C)
        b = b.reshape(rows, C)
        h = h_prev
        for g in range(n_groups):
            lo = g * V7X_SUBLANES
            hg = a[lo:lo + V7X_SUBLANES, :] * h + b[lo:lo + V7X_SUBLANES, :]
            h = hg[V7X_SUBLANES - 1:V7X_SUBLANES, :]
            hs_ref[t0 + lo:t0 + lo + V7X_SUBLANES, :] = hg
        gate = gate_ref[t0:t0 + rows, :]
        o_ref[t0:t0 + rows, :] = _gelu_tanh_times(gate, 0.5 * hs_ref[t0:t0 + rows, :]).astype(o_ref.dtype)
        return h

    h = jnp.zeros((1, C), F32)
    for si in range(n_sub):
        r0 = si * sub_rows
        xs = x16_ref[r0:r0 + sub_rows, :]
        _store_slabs(xpad_ref, r0, jnp.dot(xs, wx_ref[...], preferred_element_type=F32))
        gate_ref[r0:r0 + sub_rows, :] = jnp.dot(xs, wgt_ref[...], preferred_element_type=F32)
        qkv_ref[r0:r0 + sub_rows, :] = jnp.dot(xs, wq_ref[...],
                                               preferred_element_type=F32).astype(qkv_ref.dtype)
        for ci in range(sub_rows // rows):
            h = chunk(r0 + ci * rows, h)


def _inproj_rglru(x16, w_in, layer, cw, cb, wg, ba, bx, lam, *, batch, seq, width, qkv_width, out_dtype):
    C = wg.shape[1]
    nct = width // C
    T, D = x16.shape
    nqc = qkv_width // nct
    q_parts = nqc // C
    q_off = 2 * width // C
    conv_k = cw.shape[0]
    assert qkv_width % nct == 0 and nqc == 3 * C
    vec = lambda: pl.BlockSpec((1, C), lambda b, c: (0, c))
    wq_spec = lambda part: pl.BlockSpec((None, D, C), lambda b, c: (layer, 0, q_off + q_parts * c + part))
    est = 2 * (seq * D * 2 + 2 * D * C * 4 + D * nqc * 4 + seq * C * 2 + seq * nqc * 2 + C * 2 * C * 2) \
        + 2 * D * C * 2 + D * nqc * 2 + (3 * seq + 8) * C * 4 + 256 * nqc * 4
    return pl.pallas_call(
        functools.partial(_inproj_rglru_kernel, rows=128, sub_rows=256),
        grid=(batch, nct),
        in_specs=[
            pl.BlockSpec((seq, D), lambda b, c: (b, 0)),
            pl.BlockSpec((None, D, C), lambda b, c: (layer, 0, c)),
            pl.BlockSpec((None, D, C), lambda b, c: (layer, 0, nct + c)),
            wq_spec(0), wq_spec(1), wq_spec(2),
            pl.BlockSpec((conv_k, C), lambda b, c: (0, c)),
            vec(),
            pl.BlockSpec((1, C, 2 * C), lambda b, c: (c, 0, 0)),
            vec(), vec(), vec(),
        ],
        out_specs=[pl.BlockSpec((seq, C), lambda b, c: (b, c)),
                   pl.BlockSpec((seq, nqc), lambda b, c: (b, c))],
        out_shape=[jax.ShapeDtypeStruct((batch * seq, width), out_dtype),
                   jax.ShapeDtypeStruct((T, qkv_width), BF16)],
        scratch_shapes=[pltpu.VMEM((C // V7X_LANES, seq + V7X_SUBLANES, V7X_LANES), F32),
                        pltpu.VMEM((seq, C), F32), pltpu.VMEM((seq, C), F32),
                        pltpu.VMEM((D, C), BF16), pltpu.VMEM((D, C), BF16), pltpu.VMEM((D, nqc), BF16)],
        compiler_params=pltpu.CompilerParams(
            dimension_semantics=("parallel", "parallel"), vmem_limit_bytes=_vmem_limit(est)),
        name="inproj_rglru",
    )(x16, w_in, w_in, w_in, w_in, w_in, cw, cb, wg, ba, bx, lam)


POS_SPLIT = 256
ONES_ROWS = 16


def _attn_kernel(q_ref, k_ref, v_ref, lamv_ref, g_ref, o_ref, ka_ref, vt_ref, acc_ref, s0_ref, s1_ref,
                 *, tq, tk, n_heads, head_dim, scale, lam_init):
    h = pl.program_id(1)
    S, d2 = q_ref.shape
    V = v_ref.shape[1]
    n_kv = S // tk
    n_q = S // tq
    s_bufs = (s0_ref, s1_ref)

    def extra_cols(shape, c, lo_val, hi_val):
        lane = lax.broadcasted_iota(jnp.int32, shape, 1)
        base = head_dim * (1 - c)
        return jnp.where(lane == base, lo_val, jnp.where(lane == base + 1, hi_val, 0.0))

    def own_half(shape, c):
        lane = lax.broadcasted_iota(jnp.int32, shape, 1)
        return (lane < head_dim) if c == 0 else (lane >= head_dim)

    for jb in range(n_kv):
        rows = slice(jb * tk, (jb + 1) * tk)
        vt_ref[jb, 0:V, :] = v_ref[rows, :].astype(F32).T.astype(vt_ref.dtype)
        vt_ref[jb, V:V + ONES_ROWS, :] = jnp.ones((ONES_ROWS, tk), vt_ref.dtype)
        kb = k_ref[rows, :]
        pos = lax.broadcasted_iota(jnp.int32, (tk, d2), 0) + jb * tk
        j_lo = (pos % POS_SPLIT).astype(F32)
        j_hi = (pos // POS_SPLIT).astype(F32)
        for c in range(2):
            ka_ref[c, rows, :] = jnp.where(own_half((tk, d2), c), kb,
                                           extra_cols((tk, d2), c, j_lo, j_hi).astype(kb.dtype))

    slope = jnp.exp2(jnp.full((1, 1), -8.0 / n_heads, F32) * (h + 1).astype(F32))
    lv = lamv_ref[...]
    lam = (jnp.exp(jnp.sum(lv[0:1, :] * lv[1:2, :], axis=-1, keepdims=True))
           - jnp.exp(jnp.sum(lv[2:3, :] * lv[3:4, :], axis=-1, keepdims=True)) + lam_init)
    gain = g_ref[...] * (1.0 - lam_init)

    q_aug = {}

    def augmented_queries(qi):
        if qi not in q_aug:
            q = q_ref[qi * tq:(qi + 1) * tq, :] * jnp.asarray(scale, q_ref.dtype)
            q_aug[qi] = [jnp.where(own_half(q.shape, c), q,
                                   extra_cols(q.shape, c, slope, slope * POS_SPLIT).astype(q.dtype))
                         for c in range(2)]
        return q_aug[qi]

    def scores(qi, j, s_ref):
        qa = augmented_queries(qi)
        for c in range(2):
            s_ref[:, c * tq:(c + 1) * tq] = lax.dot_general(
                ka_ref[c, j * tk:(j + 1) * tk, :], qa[c], (((1,), (1,)), ((), ())),
                preferred_element_type=F32)

    def softmax_pv(qi, j, s_ref, m_old):
        s = s_ref[...]
        if (j + 1) * tk - 1 > qi * tq:
            key_pos = lax.broadcasted_iota(jnp.int32, (tk, 2 * tq), 0) + j * tk
            col = lax.broadcasted_iota(jnp.int32, (tk, 2 * tq), 1)
            query_pos = jnp.where(col >= tq, col - tq, col) + qi * tq
            s = jnp.where(key_pos <= query_pos, s, MASK_VALUE)
        p_dtype = vt_ref.dtype
        if m_old is None:
            m_new = jnp.max(s, axis=0, keepdims=True)
            acc_ref[...] = jnp.dot(vt_ref[j], jnp.exp(s - m_new).astype(p_dtype), preferred_element_type=F32)
        else:
            m_new = jnp.maximum(m_old, jnp.max(s, axis=0, keepdims=True))
            alpha = jnp.exp(m_old - m_new)
            acc_ref[...] = alpha * acc_ref[...] + jnp.dot(vt_ref[j], jnp.exp(s - m_new).astype(p_dtype),
                                                          preferred_element_type=F32)
        return m_new

    def finish(qi):
        ot = (acc_ref[0:V, 0:tq] / acc_ref[V:V + 1, 0:tq]
              - lam * (acc_ref[0:V, tq:2 * tq] / acc_ref[V:V + 1, tq:2 * tq]))
        ms = jnp.mean(ot * ot, axis=0, keepdims=True)
        o = (ot * lax.rsqrt(ms + RMS_EPS)).T * gain
        o_ref[qi * tq:(qi + 1) * tq, :] = o.astype(o_ref.dtype)

    pairs = [(qi, j) for qi in range(n_q) for j in range(((qi + 1) * tq + tk - 1) // tk)]
    scores(*pairs[0], s_bufs[0])
    m_run = None
    for t, (qi, j) in enumerate(pairs):
        if t + 1 < len(pairs):
            scores(*pairs[t + 1], s_bufs[(t + 1) % 2])
        m_run = softmax_pv(qi, j, s_bufs[t % 2], m_run)
        if t + 1 == len(pairs) or pairs[t + 1][0] != qi:
            finish(qi)
            m_run = None


def _diff_attn(qkv, lamv, g, *, batch, seq, n_heads, head_dim, v_dim, lam_init, out_dtype, tq, tk):
    qk_w = 2 * head_dim
    assert qk_w == v_dim and seq % tq == 0 and seq % tk == 0 and seq % POS_SPLIT == 0
    assert seq // POS_SPLIT <= POS_SPLIT
    est = 2 * (3 * seq * qk_w * 2 + seq * v_dim * 2) + 2 * seq * qk_w * 2 \
        + seq * (v_dim + ONES_ROWS) * 2 + (v_dim + ONES_ROWS) * 2 * tq * 4 + 6 * tk * 2 * tq * 4
    return pl.pallas_call(
        functools.partial(_attn_kernel, tq=tq, tk=tk, n_heads=n_heads, head_dim=head_dim,
                          scale=head_dim ** -0.5, lam_init=lam_init),
        grid=(batch, n_heads),
        in_specs=[
            pl.BlockSpec((seq, qk_w), lambda b, h: (b, h)),
            pl.BlockSpec((seq, qk_w), lambda b, h: (b, n_heads + h)),
            pl.BlockSpec((seq, v_dim), lambda b, h: (b, 2 * n_heads + h)),
            pl.BlockSpec(lamv.shape, lambda b, h: (0, 0)),
            pl.BlockSpec((1, v_dim), lambda b, h: (0, 0)),
        ],
        out_specs=pl.BlockSpec((seq, v_dim), lambda b, h: (b, h)),
        out_shape=jax.ShapeDtypeStruct((batch * seq, n_heads * v_dim), out_dtype),
        scratch_shapes=[
            pltpu.VMEM((2, seq, qk_w), BF16),
            pltpu.VMEM((seq // tk, v_dim + ONES_ROWS, tk), BF16),
            pltpu.VMEM((v_dim + ONES_ROWS, 2 * tq), F32),
            pltpu.VMEM((tk, 2 * tq), F32),
            pltpu.VMEM((tk, 2 * tq), F32),
        ],
        compiler_params=pltpu.CompilerParams(
            dimension_semantics=("parallel", "parallel"), vmem_limit_bytes=_vmem_limit(est)),
        name="diff_attn",
    )(qkv, qkv, qkv, lamv, g)


def _residual_layernorm(acc_ref, xres_ref, g, b, o32_ref, o16_ref, *, alpha, r0, n_rows, rows):
    for c in range(n_rows // rows):
        lo = r0 + c * rows
        y = alpha * xres_ref[lo:lo + rows, :] + acc_ref[lo:lo + rows, :]
        mu = jnp.mean(y, axis=-1, keepdims=True)
        d = y - mu
        var = jnp.mean(d * d, axis=-1, keepdims=True)
        out = d * lax.rsqrt(var + LN_EPS) * g + b
        o32_ref[lo:lo + rows, :] = out
        o16_ref[lo:lo + rows, :] = out.astype(o16_ref.dtype)


LN_ROWS = 64


def _outproj_kernel(rg_ref, at_ref, w32_ref, xres_ref, g_ref, b_ref, o32_ref, o16_ref, acc_ref, w_ref,
                    *, alpha, sub):
    tm, half = rg_ref.shape

    @pl.when(pl.program_id(0) == 0)
    def _():
        w_ref[...] = w32_ref[...].astype(w_ref.dtype)

    g = g_ref[...]
    b = b_ref[...]
    for t in range(tm // sub):
        r0 = t * sub
        acc_ref[r0:r0 + sub, :] = (
            jnp.dot(rg_ref[r0:r0 + sub, :], w_ref[0:half, :], preferred_element_type=F32)
            + jnp.dot(at_ref[r0:r0 + sub, :], w_ref[half:, :], preferred_element_type=F32))
        _residual_layernorm(acc_ref, xres_ref, g, b, o32_ref, o16_ref, alpha=alpha, r0=r0, n_rows=sub,
                            rows=LN_ROWS)


def _outproj_ln(rg_out, at_out, w, layer, xres, g, b, *, alpha, tm, sub):
    T, half = rg_out.shape
    _, Kw, D = w.shape
    est = 2 * (2 * tm * half * 2 + tm * D * 4 + tm * D * 4 + tm * D * 2) + Kw * D * (4 + 2) \
        + 2 * tm * D * 4
    row = lambda width: pl.BlockSpec((tm, width), lambda m: (m, 0))
    full = lambda a: pl.BlockSpec(a.shape, lambda m: (0, 0))
    return pl.pallas_call(
        functools.partial(_outproj_kernel, alpha=alpha, sub=sub),
        grid=(T // tm,),
        in_specs=[row(half), row(half),
                  pl.BlockSpec((None, Kw, D), lambda m: (layer, 0, 0), pipeline_mode=pl.Buffered(1)),
                  row(D), full(g), full(b)],
        out_specs=[row(D), row(D)],
        out_shape=[jax.ShapeDtypeStruct((T, D), F32), jax.ShapeDtypeStruct((T, D), BF16)],
        scratch_shapes=[pltpu.VMEM((tm, D), F32), pltpu.VMEM((Kw, D), BF16)],
        compiler_params=pltpu.CompilerParams(
            dimension_semantics=("arbitrary",), vmem_limit_bytes=_vmem_limit(est)),
        name="outproj_ln",
    )(rg_out, at_out, w, xres, g, b)


def _ffn_up_kernel(x_ref, wg32_ref, wu32_ref, cwg_ref, cwu_ref, cbg_ref, cbu_ref, o_ref, pg_ref, pu_ref,
                   wg_ref, wu_ref, *, tiles_per_seq, conv_k, rows):
    tm, tn = o_ref.shape
    pad = V7X_SUBLANES
    m = pl.program_id(1)

    @pl.when(m == 0)
    def _():
        wg_ref[...] = wg32_ref[...].astype(wg_ref.dtype)
        wu_ref[...] = wu32_ref[...].astype(wu_ref.dtype)

    @pl.when(m % tiles_per_seq == 0)
    def _():
        pg_ref[:, 0:pad, :] = jnp.zeros((pg_ref.shape[0], pad, V7X_LANES), F32)
        pu_ref[:, 0:pad, :] = jnp.zeros((pu_ref.shape[0], pad, V7X_LANES), F32)

    cwg = cwg_ref[...]
    cbg = cbg_ref[...]
    cwu = 0.5 * cwu_ref[...]
    cbu = 0.5 * cbu_ref[...]
    for c in range(tm // rows):
        r0 = c * rows
        xs = x_ref[r0:r0 + rows, :]
        _store_slabs(pg_ref, r0, jnp.dot(xs, wg_ref[...], preferred_element_type=F32))
        _store_slabs(pu_ref, r0, jnp.dot(xs, wu_ref[...], preferred_element_type=F32))
        gv = _causal_conv_rows(pg_ref, cwg, cbg, r0, rows)
        uv = _causal_conv_rows(pu_ref, cwu, cbu, r0, rows)
        o_ref[r0:r0 + rows, :] = _gelu_tanh_times(gv, uv).astype(o_ref.dtype)
    pg_ref[:, 0:pad, :] = pg_ref[:, tm:tm + pad, :]
    pu_ref[:, 0:pad, :] = pu_ref[:, tm:tm + pad, :]


def _ffn_up(x, w, layer, cw, cb, *, seq, d_ff, tm, tn, out_dtype):
    T, K = x.shape
    nt = d_ff // tn
    conv_k = cw.shape[0]
    assert seq % tm == 0 and d_ff % tn == 0
    est = 2 * (tm * K * 2 + 2 * K * tn * 4 + tm * tn * 2) + 2 * K * tn * 2 + 2 * (tm + 8) * tn * 4 \
        + 2 * tm * tn * 4
    vecg = lambda r: pl.BlockSpec((r, tn), lambda n, m: (0, n))
    vecu = lambda r: pl.BlockSpec((r, tn), lambda n, m: (0, n + nt))
    return pl.pallas_call(
        functools.partial(_ffn_up_kernel, tiles_per_seq=seq // tm, conv_k=conv_k, rows=min(256, tm)),
        grid=(nt, T // tm),
        in_specs=[
            pl.BlockSpec((tm, K), lambda n, m: (m, 0)),
            pl.BlockSpec((None, K, tn), lambda n, m: (layer, 0, n)),
            pl.BlockSpec((None, K, tn), lambda n, m: (layer, 0, n + nt)),
            vecg(conv_k), vecu(conv_k), vecg(1), vecu(1),
        ],
        out_specs=pl.BlockSpec((tm, tn), lambda n, m: (m, n)),
        out_shape=jax.ShapeDtypeStruct((T, d_ff), out_dtype),
        scratch_shapes=[pltpu.VMEM((tn // V7X_LANES, tm + V7X_SUBLANES, V7X_LANES), F32),
                        pltpu.VMEM((tn // V7X_LANES, tm + V7X_SUBLANES, V7X_LANES), F32),
                        pltpu.VMEM((K, tn), BF16), pltpu.VMEM((K, tn), BF16)],
        compiler_params=pltpu.CompilerParams(
            dimension_semantics=("parallel", "arbitrary"), vmem_limit_bytes=_vmem_limit(est)),
        name="ffn_up",
    )(x, w, w, cw, cw, cb, cb)


def _ffn_down_kernel(h_ref, w_ref, xres_ref, g_ref, b_ref, o32_ref, o16_ref, acc_ref, *, alpha):
    tm = h_ref.shape[0]
    acc_ref[...] = jnp.dot(h_ref[...], w_ref[...], preferred_element_type=F32)
    _residual_layernorm(acc_ref, xres_ref, g_ref[...], b_ref[...], o32_ref, o16_ref, alpha=alpha, r0=0,
                        n_rows=tm, rows=LN_ROWS)


def _ffn_down_ln(h, w, layer, xres, g, b, *, alpha, tm):
    T, K = h.shape
    D = w.shape[2]
    assert T % tm == 0
    est = 2 * (tm * K * 2 + tm * D * 4 + tm * D * 4 + tm * D * 2) + K * D * 2 + 2 * tm * D * 4
    row = lambda width: pl.BlockSpec((tm, width), lambda m: (m, 0))
    full = lambda a: pl.BlockSpec(a.shape, lambda m: (0, 0))
    return pl.pallas_call(
        functools.partial(_ffn_down_kernel, alpha=alpha),
        grid=(T // tm,),
        in_specs=[row(K), pl.BlockSpec((None, K, D), lambda m: (layer, 0, 0), pipeline_mode=pl.Buffered(1)),
                  row(D), full(g), full(b)],
        out_specs=[row(D), row(D)],
        out_shape=[jax.ShapeDtypeStruct((T, D), F32), jax.ShapeDtypeStruct((T, D), BF16)],
        scratch_shapes=[pltpu.VMEM((tm, D), F32)],
        compiler_params=pltpu.CompilerParams(
            dimension_semantics=("parallel",), vmem_limit_bytes=_vmem_limit(est)),
        name="ffn_down_ln",
    )(h, w, xres, g, b)


def _block_diag(w, tile):
    G, d, _ = w.shape
    per = tile // d
    w4 = w.reshape(G // per, per, d, d)
    eye = jnp.eye(per, dtype=w.dtype)
    return jnp.einsum('jipq,ik->jipkq', w4, eye).reshape(G // per, tile, tile)


def kernel(x, w_in, rg_conv_w, rg_conv_b, rg_gate_a_w, rg_gate_a_b, rg_gate_x_w, rg_gate_x_b, rg_lambda,
           lam_q1, lam_k1, lam_q2, lam_k2, subln_g, w_out, ln_mix_g, ln_mix_b, w_up, ffn_conv_w,
           ffn_conv_b, w_down, ln_ffn_g, ln_ffn_b):
    B, S, D = x.shape
    depth = w_in.shape[0]
    rg_w = rg_conv_w.shape[2]
    head_dim = lam_q1.shape[1]
    v_dim = subln_g.shape[1]
    mix_w = w_out.shape[1]
    attn_w = mix_w - rg_w
    n_heads = attn_w // v_dim
    qkv_w = w_in.shape[2] - 2 * rg_w
    d_ff = w_down.shape[1]
    T = B * S
    alpha = (2.0 * depth) ** 0.25

    x32 = x.reshape(T, D)
    x16 = x32.astype(BF16)
    w_down16 = w_down.astype(BF16)
    for i in range(depth):
        wg = jnp.concatenate([_block_diag(rg_gate_a_w[i], V7X_MXU_DIM),
                              _block_diag(rg_gate_x_w[i], V7X_MXU_DIM)], axis=-1).astype(BF16)
        lam_init = 0.8 - 0.6 * math.exp(-0.3 * i)
        lamv = jnp.stack([lam_q1[i], lam_k1[i], lam_q2[i], lam_k2[i]])

        rg_out, qkv = _inproj_rglru(x16, w_in, i, rg_conv_w[i], rg_conv_b[i][None], wg,
                                    rg_gate_a_b[i][None], rg_gate_x_b[i][None], rg_lambda[i][None],
                                    batch=B, seq=S, width=rg_w, qkv_width=qkv_w, out_dtype=BF16)
        at_out = _diff_attn(qkv, lamv, subln_g[i][None], batch=B, seq=S, n_heads=n_heads,
                            head_dim=head_dim, v_dim=v_dim, lam_init=lam_init, out_dtype=BF16,
                            tq=min(512, S), tk=min(512, S))
        x32, x16 = _outproj_ln(rg_out, at_out, w_out, i, x32, ln_mix_g[i][None], ln_mix_b[i][None],
                               alpha=alpha, tm=512, sub=256)
        hmid = _ffn_up(x16, w_up, i, ffn_conv_w[i], ffn_conv_b[i][None], seq=S, d_ff=d_ff, tm=S,
                       tn=512, out_dtype=BF16)
        x32, x16 = _ffn_down_ln(hmid, w_down16, i, x32, ln_ffn_g[i][None], ln_ffn_b[i][None], alpha=alpha,
                                tm=256)
    return x32.reshape(B, S, D)
```

```python
import functools
import math

import jax
import jax.numpy as jnp
from jax import lax
from jax.experimental import pallas as pl
from jax.experimental.pallas import tpu as pltpu

F32 = jnp.float32
BF16 = jnp.bfloat16

V7X_LANES = 128
V7X_SUBLANES = 8
V7X_MXU_DIM = 256
V7X_VMEM_BYTES = 64 * 1024 * 1024

RG_C = 8.0
LN_EPS = 1e-5
RMS_EPS = 1e-5
MASK_VALUE = -1e30


def _vmem_limit(nbytes):
    return int(min(V7X_VMEM_BYTES - (4 << 20), max(nbytes + (8 << 20), 16 << 20)))


def _gelu_tanh_times(x, half_y):
    c = math.sqrt(2.0 / math.pi)
    t = jnp.tanh(x * (c + (c * 0.044715) * (x * x)))
    return (x + x * t) * half_y


def _causal_conv_rows(pad_ref, cw, cb, r0, rows):
    conv_k = cw.shape[0]
    pad = V7X_SUBLANES
    strips = []
    for s in range(pad_ref.shape[0]):
        lanes = slice(s * V7X_LANES, (s + 1) * V7X_LANES)
        out = cb[:, lanes] + cw[conv_k - 1:conv_k, lanes] * pad_ref[s, pl.ds(r0 + pad, rows), :]
        for k in range(conv_k - 1):
            start = r0 + pad - (conv_k - 1) + k
            out = out + cw[k:k + 1, lanes] * pad_ref[s, pl.ds(start, rows, stride=1), :]
        strips.append(out)
    return strips[0] if len(strips) == 1 else jnp.concatenate(strips, axis=1)


def _store_slabs(pad_ref, r0, value):
    rows = value.shape[0]
    for s in range(pad_ref.shape[0]):
        pad_ref[s, V7X_SUBLANES + r0:V7X_SUBLANES + r0 + rows, :] = value[:, s * V7X_LANES:(s + 1) * V7X_LANES]


def _inproj_rglru_kernel(x16_ref, wx32_ref, wgt32_ref, wq32a_ref, wq32b_ref, wq32c_ref, cw_ref, cb_ref, wg_ref,
                         ba_ref, bx_ref, lam_ref, o_ref, qkv_ref, xpad_ref, gate_ref, hs_ref,
                         wx_ref, wgt_ref, wq_ref, *, rows, sub_rows):
    S, C = o_ref.shape
    pad = V7X_SUBLANES
    xpad_ref[:, 0:pad, :] = jnp.zeros((xpad_ref.shape[0], pad, V7X_LANES), F32)
    wx_ref[...] = wx32_ref[...].astype(wx_ref.dtype)
    wgt_ref[...] = wgt32_ref[...].astype(wgt_ref.dtype)
    for part, ref in enumerate((wq32a_ref, wq32b_ref, wq32c_ref)):
        wq_ref[:, part * C:(part + 1) * C] = ref[...].astype(wq_ref.dtype)

    lam = lam_ref[...]
    log_sig = jnp.minimum(lam, 0.0) - jnp.log1p(jnp.exp(-jnp.abs(lam)))
    cw = cw_ref[...]
    cb = cb_ref[...]
    ba = ba_ref[...]
    bx = bx_ref[...]
    wg = wg_ref[0]
    n_groups = rows // V7X_SUBLANES
    sub3 = lax.broadcasted_iota(jnp.int32, (n_groups, V7X_SUBLANES, C), 1)
    n_sub = S // sub_rows

    def chunk(t0, h_prev):
        u = _causal_conv_rows(xpad_ref, cw, cb, t0, rows)
        pre = jnp.dot(u.astype(BF16), wg, preferred_element_type=F32)
        r = jax.nn.sigmoid(pre[:, :C] + ba)
        ig = jax.nn.sigmoid(pre[:, C:] + bx)
        log_a = RG_C * r * log_sig
        a = jnp.exp(log_a)
        th = jnp.tanh(log_a)
        m2 = -2.0 * th / (1.0 - th)
        mult = jnp.where(m2 > 0.0, m2 * lax.rsqrt(m2), 0.0)
        b = mult * (ig * u)
        a = a.reshape(n_groups, V7X_SUBLANES, C)
        b = b.reshape(n_groups, V7X_SUBLANES, C)
        for s in (1, 2, 4):
            keep = sub3 >= s
            a_sh = jnp.where(keep, pltpu.roll(a, s, axis=1), 1.0)
            b_sh = jnp.where(keep, pltpu.roll(b, s, axis=1), 0.0)
            b = a * b_sh + b
            a = a * a_sh
        a = a.reshape(rows, C)
        b = b.reshape(rows, C)
        h = h_prev
        for g in range(n_groups):
            lo = g * V7X_SUBLANES
            hg = a[lo:lo + V7X_SUBLANES, :] * h + b[lo:lo + V7X_SUBLANES, :]
            h = hg[V7X_SUBLANES - 1:V7X_SUBLANES, :]
            hs_ref[t0 + lo:t0 + lo + V7X_SUBLANES, :] = hg
        gate = gate_ref[t0:t0 + rows, :]
        o_ref[t0:t0 + rows, :] = _gelu_tanh_times(gate, 0.5 * hs_ref[t0:t0 + rows, :]).astype(o_ref.dtype)
        return h

    h = jnp.zeros((1, C), F32)
    for si in range(n_sub):
        r0 = si * sub_rows
        xs = x16_ref[r0:r0 + sub_rows, :]
        _store_slabs(xpad_ref, r0, jnp.dot(xs, wx_ref[...], preferred_element_type=F32))
        gate_ref[r0:r0 + sub_rows, :] = jnp.dot(xs, wgt_ref[...], preferred_element_type=F32)
        qkv_ref[r0:r0 + sub_rows, :] = jnp.dot(xs, wq_ref[...],
                                               preferred_element_type=F32).astype(qkv_ref.dtype)
        for ci in range(sub_rows // rows):
            h = chunk(r0 + ci * rows, h)


def _inproj_rglru(x16, w_in, layer, cw, cb, wg, ba, bx, lam, *, batch, seq, width, qkv_width, out_dtype):
    C = wg.shape[1]
    nct = width // C
    T, D = x16.shape
    nqc = qkv_width // nct
    q_parts = nqc // C
    q_off = 2 * width // C
    conv_k = cw.shape[0]
    assert qkv_width % nct == 0 and nqc == 3 * C
    vec = lambda: pl.BlockSpec((1, C), lambda b, c: (0, c))
    wq_spec = lambda part: pl.BlockSpec((None, D, C), lambda b, c: (layer, 0, q_off + q_parts * c + part))
    est = 2 * (seq * D * 2 + 2 * D * C * 4 + D * nqc * 4 + seq * C * 2 + seq * nqc * 2 + C * 2 * C * 2) \
        + 2 * D * C * 2 + D * nqc * 2 + (3 * seq + 8) * C * 4 + 256 * nqc * 4
    return pl.pallas_call(
        functools.partial(_inproj_rglru_kernel, rows=128, sub_rows=256),
        grid=(batch, nct),
        in_specs=[
            pl.BlockSpec((seq, D), lambda b, c: (b, 0)),
            pl.BlockSpec((None, D, C), lambda b, c: (layer, 0, c)),
            pl.BlockSpec((None, D, C), lambda b, c: (layer, 0, nct + c)),
            wq_spec(0), wq_spec(1), wq_spec(2),
            pl.BlockSpec((conv_k, C), lambda b, c: (0, c)),
            vec(),
            pl.BlockSpec((1, C, 2 * C), lambda b, c: (c, 0, 0)),
            vec(), vec(), vec(),
        ],
        out_specs=[pl.BlockSpec((seq, C), lambda b, c: (b, c)),
                   pl.BlockSpec((seq, nqc), lambda b, c: (b, c))],
        out_shape=[jax.ShapeDtypeStruct((batch * seq, width), out_dtype),
                   jax.ShapeDtypeStruct((T, qkv_width), BF16)],
        scratch_shapes=[pltpu.VMEM((C // V7X_LANES, seq + V7X_SUBLANES, V7X_LANES), F32),
                        pltpu.VMEM((seq, C), F32), pltpu.VMEM((seq, C), F32),
                        pltpu.VMEM((D, C), BF16), pltpu.VMEM((D, C), BF16), pltpu.VMEM((D, nqc), BF16)],
        compiler_params=pltpu.CompilerParams(
            dimension_semantics=("parallel", "parallel"), vmem_limit_bytes=_vmem_limit(est)),
        name="inproj_rglru",
    )(x16, w_in, w_in, w_in, w_in, w_in, cw, cb, wg, ba, bx, lam)


POS_SPLIT = 256
ONES_ROWS = 16


def _attn_kernel(q_ref, k_ref, v_ref, lamv_ref, g_ref, o_ref, ka_ref, vt_ref, acc_ref, s0_ref, s1_ref,
                 *, tq, tk, n_heads, head_dim, scale, lam_init):
    h = pl.program_id(1)
    S, d2 = q_ref.shape
    V = v_ref.shape[1]
    n_kv = S // tk
    n_q = S // tq
    s_bufs = (s0_ref, s1_ref)

    def extra_cols(shape, c, lo_val, hi_val):
        lane = lax.broadcasted_iota(jnp.int32, shape, 1)
        base = head_dim * (1 - c)
        return jnp.where(lane == base, lo_val, jnp.where(lane == base + 1, hi_val, 0.0))

    def own_half(shape, c):
        lane = lax.broadcasted_iota(jnp.int32, shape, 1)
        return (lane < head_dim) if c == 0 else (lane >= head_dim)

    for jb in range(n_kv):
        rows = slice(jb * tk, (jb + 1) * tk)
        vt_ref[jb, 0:V, :] = v_ref[rows, :].astype(F32).T.astype(vt_ref.dtype)
        vt_ref[jb, V:V + ONES_ROWS, :] = jnp.ones((ONES_ROWS, tk), vt_ref.dtype)
        kb = k_ref[rows, :]
        pos = lax.broadcasted_iota(jnp.int32, (tk, d2), 0) + jb * tk
        j_lo = (pos % POS_SPLIT).astype(F32)
        j_hi = (pos // POS_SPLIT).astype(F32)
        for c in range(2):
            ka_ref[c, rows, :] = jnp.where(own_half((tk, d2), c), kb,
                                           extra_cols((tk, d2), c, j_lo, j_hi).astype(kb.dtype))

    slope = jnp.exp2(jnp.full((1, 1), -8.0 / n_heads, F32) * (h + 1).astype(F32))
    lv = lamv_ref[...]
    lam = (jnp.exp(jnp.sum(lv[0:1, :] * lv[1:2, :], axis=-1, keepdims=True))
           - jnp.exp(jnp.sum(lv[2:3, :] * lv[3:4, :], axis=-1, keepdims=True)) + lam_init)
    gain = g_ref[...] * (1.0 - lam_init)

    q_aug = {}

    def augmented_queries(qi):
        if qi not in q_aug:
            q = q_ref[qi * tq:(qi + 1) * tq, :] * jnp.asarray(scale, q_ref.dtype)
            q_aug[qi] = [jnp.where(own_half(q.shape, c), q,
                                   extra_cols(q.shape, c, slope, slope * POS_SPLIT).astype(q.dtype))
                         for c in range(2)]
        return q_aug[qi]

    def scores(qi, j, s_ref):
        qa = augmented_queries(qi)
        for c in range(2):
            s_ref[:, c * tq:(c + 1) * tq] = lax.dot_general(
                ka_ref[c, j * tk:(j + 1) * tk, :], qa[c], (((1,), (1,)), ((), ())),
                preferred_element_type=F32)

    def softmax_pv(qi, j, s_ref, m_old):
        s = s_ref[...]
        if (j + 1) * tk - 1 > qi * tq:
            key_pos = lax.broadcasted_iota(jnp.int32, (tk, 2 * tq), 0) + j * tk
            col = lax.broadcasted_iota(jnp.int32, (tk, 2 * tq), 1)
            query_pos = jnp.where(col >= tq, col - tq, col) + qi * tq
            s = jnp.where(key_pos <= query_pos, s, MASK_VALUE)
        p_dtype = vt_ref.dtype
        if m_old is None:
            m_new = jnp.max(s, axis=0, keepdims=True)
            acc_ref[...] = jnp.dot(vt_ref[j], jnp.exp(s - m_new).astype(p_dtype), preferred_element_type=F32)
        else:
            m_new = jnp.maximum(m_old, jnp.max(s, axis=0, keepdims=True))
            alpha = jnp.exp(m_old - m_new)
            acc_ref[...] = alpha * acc_ref[...] + jnp.dot(vt_ref[j], jnp.exp(s - m_new).astype(p_dtype),
                                                          preferred_element_type=F32)
        return m_new

    def finish(qi):
        ot = (acc_ref[0:V, 0:tq] / acc_ref[V:V + 1, 0:tq]
              - lam * (acc_ref[0:V, tq:2 * tq] / acc_ref[V:V + 1, tq:2 * tq]))
        ms = jnp.mean(ot * ot, axis=0, keepdims=True)
        o = (ot * lax.rsqrt(ms + RMS_EPS)).T * gain
        o_ref[qi * tq:(qi + 1) * tq, :] = o.astype(o_ref.dtype)

    pairs = [(qi, j) for qi in range(n_q) for j in range(((qi + 1) * tq + tk - 1) // tk)]
    scores(*pairs[0], s_bufs[0])
    m_run = None
    for t, (qi, j) in enumerate(pairs):
        if t + 1 < len(pairs):
            scores(*pairs[t + 1], s_bufs[(t + 1) % 2])
        m_run = softmax_pv(qi, j, s_bufs[t % 2], m_run)
        if t + 1 == len(pairs) or pairs[t + 1][0] != qi:
            finish(qi)
            m_run = None


def _diff_attn(qkv, lamv, g, *, batch, seq, n_heads, head_dim, v_dim, lam_init, out_dtype, tq, tk):
    qk_w = 2 * head_dim
    assert qk_w == v_dim and seq % tq == 0 and seq % tk == 0 and seq % POS_SPLIT == 0
    assert seq // POS_SPLIT <= POS_SPLIT
    est = 2 * (3 * seq * qk_w * 2 + seq * v_dim * 2) + 2 * seq * qk_w * 2 \
        + seq * (v_dim + ONES_ROWS) * 2 + (v_dim + ONES_ROWS) * 2 * tq * 4 + 6 * tk * 2 * tq * 4
    return pl.pallas_call(
        functools.partial(_attn_kernel, tq=tq, tk=tk, n_heads=n_heads, head_dim=head_dim,
                          scale=head_dim ** -0.5, lam_init=lam_init),
        grid=(batch, n_heads),
        in_specs=[
            pl.BlockSpec((seq, qk_w), lambda b, h: (b, h)),
            pl.BlockSpec((seq, qk_w), lambda b, h: (b, n_heads + h)),
            pl.BlockSpec((seq, v_dim), lambda b, h: (b, 2 * n_heads + h)),
            pl.BlockSpec(lamv.shape, lambda b, h: (0, 0)),
            pl.BlockSpec((1, v_dim), lambda b, h: (0, 0)),
        ],
        out_specs=pl.BlockSpec((seq, v_dim), lambda b, h: (b, h)),
        out_shape=jax.ShapeDtypeStruct((batch * seq, n_heads * v_dim), out_dtype),
        scratch_shapes=[
            pltpu.VMEM((2, seq, qk_w), BF16),
            pltpu.VMEM((seq // tk, v_dim + ONES_ROWS, tk), BF16),
            pltpu.VMEM((v_dim + ONES_ROWS, 2 * tq), F32),
            pltpu.VMEM((tk, 2 * tq), F32),
            pltpu.VMEM((tk, 2 * tq), F32),
        ],
        compiler_params=pltpu.CompilerParams(
            dimension_semantics=("parallel", "parallel"), vmem_limit_bytes=_vmem_limit(est)),
        name="diff_attn",
    )(qkv, qkv, qkv, lamv, g)


def _residual_layernorm(acc_ref, xres_ref, g, b, o32_ref, o16_ref, *, alpha, r0, n_rows, rows):
    for c in range(n_rows // rows):
        lo = r0 + c * rows
        y = alpha * xres_ref[lo:lo + rows, :] + acc_ref[lo:lo + rows, :]
        mu = jnp.mean(y, axis=-1, keepdims=True)
        d = y - mu
        var = jnp.mean(d * d, axis=-1, keepdims=True)
        out = d * lax.rsqrt(var + LN_EPS) * g + b
        o32_ref[lo:lo + rows, :] = out
        o16_ref[lo:lo + rows, :] = out.astype(o16_ref.dtype)


LN_ROWS = 64


def _outproj_kernel(rg_ref, at_ref, w32_ref, xres_ref, g_ref, b_ref, o32_ref, o16_ref, acc_ref, w_ref,
                    *, alpha, sub):
    tm, half = rg_ref.shape

    @pl.when(pl.program_id(0) == 0)
    def _():
        w_ref[...] = w32_ref[...].astype(w_ref.dtype)

    g = g_ref[...]
    b = b_ref[...]
    for t in range(tm // sub):
        r0 = t * sub
        acc_ref[r0:r0 + sub, :] = (
            jnp.dot(rg_ref[r0:r0 + sub, :], w_ref[0:half, :], preferred_element_type=F32)
            + jnp.dot(at_ref[r0:r0 + sub, :], w_ref[half:, :], preferred_element_type=F32))
        _residual_layernorm(acc_ref, xres_ref, g, b, o32_ref, o16_ref, alpha=alpha, r0=r0, n_rows=sub,
                            rows=LN_ROWS)


def _outproj_ln(rg_out, at_out, w, layer, xres, g, b, *, alpha, tm, sub):
    T, half = rg_out.shape
    _, Kw, D = w.shape
    est = 2 * (2 * tm * half * 2 + tm * D * 4 + tm * D * 4 + tm * D * 2) + Kw * D * (4 + 2) \
        + 2 * tm * D * 4
    row = lambda width: pl.BlockSpec((tm, width), lambda m: (m, 0))
    full = lambda a: pl.BlockSpec(a.shape, lambda m: (0, 0))
    return pl.pallas_call(
        functools.partial(_outproj_kernel, alpha=alpha, sub=sub),
        grid=(T // tm,),
        in_specs=[row(half), row(half),
                  pl.BlockSpec((None, Kw, D), lambda m: (layer, 0, 0), pipeline_mode=pl.Buffered(1)),
                  row(D), full(g), full(b)],
        out_specs=[row(D), row(D)],
        out_shape=[jax.ShapeDtypeStruct((T, D), F32), jax.ShapeDtypeStruct((T, D), BF16)],
        scratch_shapes=[pltpu.VMEM((tm, D), F32), pltpu.VMEM((Kw, D), BF16)],
        compiler_params=pltpu.CompilerParams(
            dimension_semantics=("arbitrary",), vmem_limit_bytes=_vmem_limit(est)),
        name="outproj_ln",
    )(rg_out, at_out, w, xres, g, b)


def _ffn_up_kernel(x_ref, wg32_ref, wu32_ref, cwg_ref, cwu_ref, cbg_ref, cbu_ref, o_ref, pg_ref, pu_ref,
                   wg_ref, wu_ref, *, tiles_per_seq, conv_k, rows):
    tm, tn = o_ref.shape
    pad = V7X_SUBLANES
    m = pl.program_id(1)

    @pl.when(m == 0)
    def _():
        wg_ref[...] = wg32_ref[...].astype(wg_ref.dtype)
        wu_ref[...] = wu32_ref[...].astype(wu_ref.dtype)

    @pl.when(m % tiles_per_seq == 0)
    def _():
        pg_ref[:, 0:pad, :] = jnp.zeros((pg_ref.shape[0], pad, V7X_LANES), F32)
        pu_ref[:, 0:pad, :] = jnp.zeros((pu_ref.shape[0], pad, V7X_LANES), F32)

    cwg = cwg_ref[...]
    cbg = cbg_ref[...]
    cwu = 0.5 * cwu_ref[...]
    cbu = 0.5 * cbu_ref[...]
    for c in range(tm // rows):
        r0 = c * rows
        xs = x_ref[r0:r0 + rows, :]
        _store_slabs(pg_ref, r0, jnp.dot(xs, wg_ref[...], preferred_element_type=F32))
        _store_slabs(pu_ref, r0, jnp.dot(xs, wu_ref[...], preferred_element_type=F32))
        gv = _causal_conv_rows(pg_ref, cwg, cbg, r0, rows)
        uv = _causal_conv_rows(pu_ref, cwu, cbu, r0, rows)
        o_ref[r0:r0 + rows, :] = _gelu_tanh_times(gv, uv).astype(o_ref.dtype)
    pg_ref[:, 0:pad, :] = pg_ref[:, tm:tm + pad, :]
    pu_ref[:, 0:pad, :] = pu_ref[:, tm:tm + pad, :]


def _ffn_up(x, w, layer, cw, cb, *, seq, d_ff, tm, tn, out_dtype):
    T, K = x.shape
    nt = d_ff // tn
    conv_k = cw.shape[0]
    assert seq % tm == 0 and d_ff % tn == 0
    est = 2 * (tm * K * 2 + 2 * K * tn * 4 + tm * tn * 2) + 2 * K * tn * 2 + 2 * (tm + 8) * tn * 4 \
        + 2 * tm * tn * 4
    vecg = lambda r: pl.BlockSpec((r, tn), lambda n, m: (0, n))
    vecu = lambda r: pl.BlockSpec((r, tn), lambda n, m: (0, n + nt))
    return pl.pallas_call(
        functools.partial(_ffn_up_kernel, tiles_per_seq=seq // tm, conv_k=conv_k, rows=min(256, tm)),
        grid=(nt, T // tm),
        in_specs=[
            pl.BlockSpec((tm, K), lambda n, m: (m, 0)),
            pl.BlockSpec((None, K, tn), lambda n, m: (layer, 0, n)),
            pl.BlockSpec((None, K, tn), lambda n, m: (layer, 0, n + nt)),
            vecg(conv_k), vecu(conv_k), vecg(1), vecu(1),
        ],
        out_specs=pl.BlockSpec((tm, tn), lambda n, m: (m, n)),
        out_shape=jax.ShapeDtypeStruct((T, d_ff), out_dtype),
        scratch_shapes=[pltpu.VMEM((tn // V7X_LANES, tm + V7X_SUBLANES, V7X_LANES), F32),
                        pltpu.VMEM((tn // V7X_LANES, tm + V7X_SUBLANES, V7X_LANES), F32),
                        pltpu.VMEM((K, tn), BF16), pltpu.VMEM((K, tn), BF16)],
        compiler_params=pltpu.CompilerParams(
            dimension_semantics=("parallel", "arbitrary"), vmem_limit_bytes=_vmem_limit(est)),
        name="ffn_up",
    )(x, w, w, cw, cw, cb, cb)


def _ffn_down_kernel(h_ref, w32_ref, xres_ref, g_ref, b_ref, o32_ref, o16_ref, acc_ref, w_ref, *, alpha, n_pre):
    s = pl.program_id(0)
    kc = w32_ref.shape[0]

    @pl.when(s < n_pre)
    def _():
        w_ref[pl.ds(pl.multiple_of(s * kc, kc), kc), :] = w32_ref[...].astype(w_ref.dtype)

    @pl.when(s >= n_pre)
    def _():
        tm = h_ref.shape[0]
        acc_ref[...] = jnp.dot(h_ref[...], w_ref[...], preferred_element_type=F32)
        _residual_layernorm(acc_ref, xres_ref, g_ref[...], b_ref[...], o32_ref, o16_ref, alpha=alpha, r0=0,
                            n_rows=tm, rows=LN_ROWS)


def _ffn_down_ln(h, w, layer, xres, g, b, *, alpha, tm, kc):
    T, K = h.shape
    D = w.shape[2]
    assert T % tm == 0 and K % kc == 0
    n_pre = K // kc
    est = 2 * (tm * K * 2 + kc * D * 4 + tm * D * 4 + tm * D * 4 + tm * D * 2) + K * D * 2 + 2 * tm * D * 4
    tile = lambda s: jnp.maximum(s - n_pre, 0)
    row = lambda width: pl.BlockSpec((tm, width), lambda s: (tile(s), 0))
    full = lambda a: pl.BlockSpec(a.shape, lambda s: (0, 0))
    return pl.pallas_call(
        functools.partial(_ffn_down_kernel, alpha=alpha, n_pre=n_pre),
        grid=(n_pre + T // tm,),
        in_specs=[row(K),
                  pl.BlockSpec((None, kc, D), lambda s: (layer, jnp.minimum(s, n_pre - 1), 0)),
                  row(D), full(g), full(b)],
        out_specs=[row(D), row(D)],
        out_shape=[jax.ShapeDtypeStruct((T, D), F32), jax.ShapeDtypeStruct((T, D), BF16)],
        scratch_shapes=[pltpu.VMEM((tm, D), F32), pltpu.VMEM((K, D), BF16)],
        compiler_params=pltpu.CompilerParams(
            dimension_semantics=("arbitrary",), vmem_limit_bytes=_vmem_limit(est)),
        name="ffn_down_ln",
    )(h, w, xres, g, b)


def _block_diag(w, tile):
    G, d, _ = w.shape
    per = tile // d
    w4 = w.reshape(G // per, per, d, d)
    eye = jnp.eye(per, dtype=w.dtype)
    return jnp.einsum('jipq,ik->jipkq', w4, eye).reshape(G // per, tile, tile)


def kernel(x, w_in, rg_conv_w, rg_conv_b, rg_gate_a_w, rg_gate_a_b, rg_gate_x_w, rg_gate_x_b, rg_lambda,
           lam_q1, lam_k1, lam_q2, lam_k2, subln_g, w_out, ln_mix_g, ln_mix_b, w_up, ffn_conv_w,
           ffn_conv_b, w_down, ln_ffn_g, ln_ffn_b):
    B, S, D = x.shape
    depth = w_in.shape[0]
    rg_w = rg_conv_w.shape[2]
    head_dim = lam_q1.shape[1]
    v_dim = subln_g.shape[1]
    mix_w = w_out.shape[1]
    attn_w = mix_w - rg_w
    n_heads = attn_w // v_dim
    qkv_w = w_in.shape[2] - 2 * rg_w
    d_ff = w_down.shape[1]
    T = B * S
    alpha = (2.0 * depth) ** 0.25

    x32 = x.reshape(T, D)
    x16 = x32.astype(BF16)
    for i in range(depth):
        wg = jnp.concatenate([_block_diag(rg_gate_a_w[i], V7X_MXU_DIM),
                              _block_diag(rg_gate_x_w[i], V7X_MXU_DIM)], axis=-1).astype(BF16)
        lam_init = 0.8 - 0.6 * math.exp(-0.3 * i)
        lamv = jnp.stack([lam_q1[i], lam_k1[i], lam_q2[i], lam_k2[i]])

        rg_out, qkv = _inproj_rglru(x16, w_in, i, rg_conv_w[i], rg_conv_b[i][None], wg,
                                    rg_gate_a_b[i][None], rg_gate_x_b[i][None], rg_lambda[i][None],
                                    batch=B, seq=S, width=rg_w, qkv_width=qkv_w, out_dtype=BF16)
        at_out = _diff_attn(qkv, lamv, subln_g[i][None], batch=B, seq=S, n_heads=n_heads,
                            head_dim=head_dim, v_dim=v_dim, lam_init=lam_init, out_dtype=BF16,
                            tq=min(512, S), tk=min(512, S))
        x32, x16 = _outproj_ln(rg_out, at_out, w_out, i, x32, ln_mix_g[i][None], ln_mix_b[i][None],
                               alpha=alpha, tm=512, sub=256)
        hmid = _ffn_up(x16, w_up, i, ffn_conv_w[i], ffn_conv_b[i][None], seq=S, d_ff=d_ff, tm=S,
                       tn=512, out_dtype=BF16)
        x32, x16 = _ffn_down_ln(hmid, w_down, i, x32, ln_ffn_g[i][None], ln_ffn_b[i][None], alpha=alpha,
                                tm=256, kc=512)
    return x32.reshape(B, S, D)
```

```python
import functools
import math

import jax
import jax.numpy as jnp
from jax import lax
from jax.experimental import pallas as pl
from jax.experimental.pallas import tpu as pltpu

F32 = jnp.float32
BF16 = jnp.bfloat16

V7X_LANES = 128
V7X_SUBLANES = 8
V7X_MXU_DIM = 256
V7X_VMEM_BYTES = 64 * 1024 * 1024
VMEM_TEMP_BYTES = 8 * 1024 * 1024
VMEM_UNSCOPED_BYTES = 4 * 1024 * 1024
VMEM_MIN_REQUEST_BYTES = 16 * 1024 * 1024

RG_CHUNK_ROWS = 128
PROJ_SUB_ROWS = 256
LN_ROWS = 64

RG_C = 8.0
LN_EPS = 1e-5
RMS_EPS = 1e-5
MASK_VALUE = -1e30


def _vmem_limit(nbytes):
    return int(min(V7X_VMEM_BYTES - VMEM_UNSCOPED_BYTES, max(nbytes + VMEM_TEMP_BYTES, VMEM_MIN_REQUEST_BYTES)))


def _gelu_tanh_times(x, half_y):
    c = math.sqrt(2.0 / math.pi)
    t = jnp.tanh(x * (c + (c * 0.044715) * (x * x)))
    return (x + x * t) * half_y


def _causal_conv_rows(pad_ref, cw, cb, r0, rows):
    conv_k = cw.shape[0]
    pad = V7X_SUBLANES
    strips = []
    for s in range(pad_ref.shape[0]):
        lanes = slice(s * V7X_LANES, (s + 1) * V7X_LANES)
        out = cb[:, lanes] + cw[conv_k - 1:conv_k, lanes] * pad_ref[s, pl.ds(r0 + pad, rows), :]
        for k in range(conv_k - 1):
            start = r0 + pad - (conv_k - 1) + k
            out = out + cw[k:k + 1, lanes] * pad_ref[s, pl.ds(start, rows, stride=1), :]
        strips.append(out)
    return strips[0] if len(strips) == 1 else jnp.concatenate(strips, axis=1)


def _store_slabs(pad_ref, r0, value):
    rows = value.shape[0]
    for s in range(pad_ref.shape[0]):
        pad_ref[s, V7X_SUBLANES + r0:V7X_SUBLANES + r0 + rows, :] = value[:, s * V7X_LANES:(s + 1) * V7X_LANES]


def _inproj_rglru_kernel(x16_ref, wx32_ref, wgt32_ref, wq32a_ref, wq32b_ref, wq32c_ref, cw_ref, cb_ref, wg_ref,
                         ba_ref, bx_ref, lam_ref, o_ref, qkv_ref, xpad_ref, gate_ref, hs_ref,
                         wx_ref, wgt_ref, wq_ref, *, rows, sub_rows):
    S, C = o_ref.shape
    pad = V7X_SUBLANES
    xpad_ref[:, 0:pad, :] = jnp.zeros((xpad_ref.shape[0], pad, V7X_LANES), F32)
    wx_ref[...] = wx32_ref[...].astype(wx_ref.dtype)
    wgt_ref[...] = wgt32_ref[...].astype(wgt_ref.dtype)
    for part, ref in enumerate((wq32a_ref, wq32b_ref, wq32c_ref)):
        wq_ref[:, part * C:(part + 1) * C] = ref[...].astype(wq_ref.dtype)

    lam = lam_ref[...]
    log_sig = jnp.minimum(lam, 0.0) - jnp.log1p(jnp.exp(-jnp.abs(lam)))
    cw = cw_ref[...]
    cb = cb_ref[...]
    ba = ba_ref[...]
    bx = bx_ref[...]
    wg = wg_ref[0]
    n_groups = rows // V7X_SUBLANES
    sub3 = lax.broadcasted_iota(jnp.int32, (n_groups, V7X_SUBLANES, C), 1)
    n_sub = S // sub_rows

    def chunk(t0, h_prev):
        u = _causal_conv_rows(xpad_ref, cw, cb, t0, rows)
        pre = jnp.dot(u.astype(BF16), wg, preferred_element_type=F32)
        r = jax.nn.sigmoid(pre[:, :C] + ba)
        ig = jax.nn.sigmoid(pre[:, C:] + bx)
        log_a = RG_C * r * log_sig
        a = jnp.exp(log_a)
        th = jnp.tanh(log_a)
        m2 = -2.0 * th / (1.0 - th)
        mult = jnp.where(m2 > 0.0, m2 * lax.rsqrt(m2), 0.0)
        b = mult * (ig * u)
        a = a.reshape(n_groups, V7X_SUBLANES, C)
        b = b.reshape(n_groups, V7X_SUBLANES, C)
        for s in (1, 2, 4):
            keep = sub3 >= s
            a_sh = jnp.where(keep, pltpu.roll(a, s, axis=1), 1.0)
            b_sh = jnp.where(keep, pltpu.roll(b, s, axis=1), 0.0)
            b = a * b_sh + b
            a = a * a_sh
        a = a.reshape(rows, C)
        b = b.reshape(rows, C)
        h = h_prev
        for g in range(n_groups):
            lo = g * V7X_SUBLANES
            hg = a[lo:lo + V7X_SUBLANES, :] * h + b[lo:lo + V7X_SUBLANES, :]
            h = hg[V7X_SUBLANES - 1:V7X_SUBLANES, :]
            hs_ref[t0 + lo:t0 + lo + V7X_SUBLANES, :] = hg
        gate = gate_ref[t0:t0 + rows, :]
        o_ref[t0:t0 + rows, :] = _gelu_tanh_times(gate, 0.5 * hs_ref[t0:t0 + rows, :]).astype(o_ref.dtype)
        return h

    h = jnp.zeros((1, C), F32)
    for si in range(n_sub):
        r0 = si * sub_rows
        xs = x16_ref[r0:r0 + sub_rows, :]
        _store_slabs(xpad_ref, r0, jnp.dot(xs, wx_ref[...], preferred_element_type=F32))
        gate_ref[r0:r0 + sub_rows, :] = jnp.dot(xs, wgt_ref[...], preferred_element_type=F32)
        qkv_ref[r0:r0 + sub_rows, :] = jnp.dot(xs, wq_ref[...],
                                               preferred_element_type=F32).astype(qkv_ref.dtype)
        for ci in range(sub_rows // rows):
            h = chunk(r0 + ci * rows, h)


def _inproj_rglru(x16, w_in, layer, cw, cb, wg, ba, bx, lam, *, batch, seq, width, qkv_width, out_dtype):
    C = wg.shape[1]
    nct = width // C
    T, D = x16.shape
    nqc = qkv_width // nct
    q_parts = nqc // C
    q_off = 2 * width // C
    conv_k = cw.shape[0]
    assert qkv_width % nct == 0 and nqc == 3 * C
    vec = lambda: pl.BlockSpec((1, C), lambda b, c: (0, c))
    wq_spec = lambda part: pl.BlockSpec((None, D, C), lambda b, c: (layer, 0, q_off + q_parts * c + part))
    est = 2 * (seq * D * 2 + 2 * D * C * 4 + D * nqc * 4 + seq * C * 2 + seq * nqc * 2 + C * 2 * C * 2) \
        + 2 * D * C * 2 + D * nqc * 2 + (3 * seq + 8) * C * 4 + PROJ_SUB_ROWS * nqc * 4
    return pl.pallas_call(
        functools.partial(_inproj_rglru_kernel, rows=RG_CHUNK_ROWS, sub_rows=PROJ_SUB_ROWS),
        grid=(batch, nct),
        in_specs=[
            pl.BlockSpec((seq, D), lambda b, c: (b, 0)),
            pl.BlockSpec((None, D, C), lambda b, c: (layer, 0, c)),
            pl.BlockSpec((None, D, C), lambda b, c: (layer, 0, nct + c)),
            wq_spec(0), wq_spec(1), wq_spec(2),
            pl.BlockSpec((conv_k, C), lambda b, c: (0, c)),
            vec(),
            pl.BlockSpec((1, C, 2 * C), lambda b, c: (c, 0, 0)),
            vec(), vec(), vec(),
        ],
        out_specs=[pl.BlockSpec((seq, C), lambda b, c: (b, c)),
                   pl.BlockSpec((seq, nqc), lambda b, c: (b, c))],
        out_shape=[jax.ShapeDtypeStruct((batch * seq, width), out_dtype),
                   jax.ShapeDtypeStruct((T, qkv_width), BF16)],
        scratch_shapes=[pltpu.VMEM((C // V7X_LANES, seq + V7X_SUBLANES, V7X_LANES), F32),
                        pltpu.VMEM((seq, C), F32), pltpu.VMEM((seq, C), F32),
                        pltpu.VMEM((D, C), BF16), pltpu.VMEM((D, C), BF16), pltpu.VMEM((D, nqc), BF16)],
        compiler_params=pltpu.CompilerParams(
            dimension_semantics=("parallel", "parallel"), vmem_limit_bytes=_vmem_limit(est)),
        name="inproj_rglru",
    )(x16, w_in, w_in, w_in, w_in, w_in, cw, cb, wg, ba, bx, lam)


POS_SPLIT = 256
ONES_ROWS = 16


def _attn_kernel(q_ref, k_ref, v_ref, lamv_ref, g_ref, o_ref, ka_ref, vt_ref, acc_ref, s0_ref, s1_ref,
                 *, tq, tk, n_heads, head_dim, scale, lam_init):
    h = pl.program_id(1)
    S, d2 = q_ref.shape
    V = v_ref.shape[1]
    n_kv = S // tk
    n_q = S // tq
    s_bufs = (s0_ref, s1_ref)

    def extra_cols(shape, c, lo_val, hi_val):
        lane = lax.broadcasted_iota(jnp.int32, shape, 1)
        base = head_dim * (1 - c)
        return jnp.where(lane == base, lo_val, jnp.where(lane == base + 1, hi_val, 0.0))

    def own_half(shape, c):
        lane = lax.broadcasted_iota(jnp.int32, shape, 1)
        return (lane < head_dim) if c == 0 else (lane >= head_dim)

    for jb in range(n_kv):
        rows = slice(jb * tk, (jb + 1) * tk)
        vt_ref[jb, 0:V, :] = v_ref[rows, :].astype(F32).T.astype(vt_ref.dtype)
        vt_ref[jb, V:V + ONES_ROWS, :] = jnp.ones((ONES_ROWS, tk), vt_ref.dtype)
        kb = k_ref[rows, :]
        pos = lax.broadcasted_iota(jnp.int32, (tk, d2), 0) + jb * tk
        j_lo = (pos % POS_SPLIT).astype(F32)
        j_hi = (pos // POS_SPLIT).astype(F32)
        for c in range(2):
            ka_ref[c, rows, :] = jnp.where(own_half((tk, d2), c), kb,
                                           extra_cols((tk, d2), c, j_lo, j_hi).astype(kb.dtype))

    slope = jnp.exp2(jnp.full((1, 1), -8.0 / n_heads, F32) * (h + 1).astype(F32))
    lv = lamv_ref[...]
    lam = (jnp.exp(jnp.sum(lv[0:1, :] * lv[1:2, :], axis=-1, keepdims=True))
           - jnp.exp(jnp.sum(lv[2:3, :] * lv[3:4, :], axis=-1, keepdims=True)) + lam_init)
    gain = g_ref[...] * (1.0 - lam_init)

    q_aug = {}

    def augmented_queries(qi):
        if qi not in q_aug:
            q = q_ref[qi * tq:(qi + 1) * tq, :] * jnp.asarray(scale, q_ref.dtype)
            q_aug[qi] = [jnp.where(own_half(q.shape, c), q,
                                   extra_cols(q.shape, c, slope, slope * POS_SPLIT).astype(q.dtype))
                         for c in range(2)]
        return q_aug[qi]

    def on_diagonal(qi, j):
        return tq == tk and j == qi and tk % 2 == 0

    def scores(qi, j, s_ref):
        qa = augmented_queries(qi)
        half_k = tk // 2
        half_q = tq // 2
        for c in range(2):
            keys = ka_ref[c, j * tk:(j + 1) * tk, :]
            if on_diagonal(qi, j):
                s_ref[0:half_k, c * tq:(c + 1) * tq] = lax.dot_general(
                    keys[0:half_k], qa[c], (((1,), (1,)), ((), ())), preferred_element_type=F32)
                s_ref[half_k:tk, c * tq:c * tq + half_q] = jnp.full((tk - half_k, half_q), MASK_VALUE, F32)
                s_ref[half_k:tk, c * tq + half_q:(c + 1) * tq] = lax.dot_general(
                    keys[half_k:tk], qa[c][half_q:tq], (((1,), (1,)), ((), ())), preferred_element_type=F32)
            else:
                s_ref[:, c * tq:(c + 1) * tq] = lax.dot_general(
                    keys, qa[c], (((1,), (1,)), ((), ())), preferred_element_type=F32)

    def weighted_values(qi, j, p):
        vt = vt_ref[j]
        if not on_diagonal(qi, j):
            return jnp.dot(vt, p, preferred_element_type=F32)
        half_k = tk // 2
        half_q = tq // 2
        parts = []
        for c in range(2):
            lo = slice(c * tq, c * tq + half_q)
            hi = slice(c * tq + half_q, (c + 1) * tq)
            parts.append(jnp.dot(vt[:, 0:half_k], p[0:half_k, lo], preferred_element_type=F32))
            parts.append(jnp.dot(vt, p[:, hi], preferred_element_type=F32))
        return jnp.concatenate(parts, axis=1)

    def softmax_pv(qi, j, s_ref, m_old):
        s = s_ref[...]
        if (j + 1) * tk - 1 > qi * tq:
            key_pos = lax.broadcasted_iota(jnp.int32, (tk, 2 * tq), 0) + j * tk
            col = lax.broadcasted_iota(jnp.int32, (tk, 2 * tq), 1)
            query_pos = jnp.where(col >= tq, col - tq, col) + qi * tq
            s = jnp.where(key_pos <= query_pos, s, MASK_VALUE)
        p_dtype = vt_ref.dtype
        if m_old is None:
            m_new = jnp.max(s, axis=0, keepdims=True)
            acc_ref[...] = weighted_values(qi, j, jnp.exp(s - m_new).astype(p_dtype))
        else:
            m_new = jnp.maximum(m_old, jnp.max(s, axis=0, keepdims=True))
            alpha = jnp.exp(m_old - m_new)
            acc_ref[...] = alpha * acc_ref[...] + weighted_values(qi, j, jnp.exp(s - m_new).astype(p_dtype))
        return m_new

    def finish(qi):
        ot = (acc_ref[0:V, 0:tq] / acc_ref[V:V + 1, 0:tq]
              - lam * (acc_ref[0:V, tq:2 * tq] / acc_ref[V:V + 1, tq:2 * tq]))
        ms = jnp.mean(ot * ot, axis=0, keepdims=True)
        o = (ot * lax.rsqrt(ms + RMS_EPS)).T * gain
        o_ref[qi * tq:(qi + 1) * tq, :] = o.astype(o_ref.dtype)

    pairs = [(qi, j) for qi in range(n_q) for j in range(((qi + 1) * tq + tk - 1) // tk)]
    scores(*pairs[0], s_bufs[0])
    m_run = None
    for t, (qi, j) in enumerate(pairs):
        if t + 1 < len(pairs):
            scores(*pairs[t + 1], s_bufs[(t + 1) % 2])
        m_run = softmax_pv(qi, j, s_bufs[t % 2], m_run)
        if t + 1 == len(pairs) or pairs[t + 1][0] != qi:
            finish(qi)
            m_run = None


def _diff_attn(qkv, lamv, g, *, batch, seq, n_heads, head_dim, v_dim, lam_init, out_dtype, tq, tk):
    qk_w = 2 * head_dim
    assert qk_w == v_dim and seq % tq == 0 and seq % tk == 0 and seq % POS_SPLIT == 0
    assert seq // POS_SPLIT <= POS_SPLIT
    est = 2 * (3 * seq * qk_w * 2 + seq * v_dim * 2) + 2 * seq * qk_w * 2 \
        + seq * (v_dim + ONES_ROWS) * 2 + (v_dim + ONES_ROWS) * 2 * tq * 4 + 6 * tk * 2 * tq * 4
    return pl.pallas_call(
        functools.partial(_attn_kernel, tq=tq, tk=tk, n_heads=n_heads, head_dim=head_dim,
                          scale=head_dim ** -0.5, lam_init=lam_init),
        grid=(batch, n_heads),
        in_specs=[
            pl.BlockSpec((seq, qk_w), lambda b, h: (b, h)),
            pl.BlockSpec((seq, qk_w), lambda b, h: (b, n_heads + h)),
            pl.BlockSpec((seq, v_dim), lambda b, h: (b, 2 * n_heads + h)),
            pl.BlockSpec(lamv.shape, lambda b, h: (0, 0)),
            pl.BlockSpec((1, v_dim), lambda b, h: (0, 0)),
        ],
        out_specs=pl.BlockSpec((seq, v_dim), lambda b, h: (b, h)),
        out_shape=jax.ShapeDtypeStruct((batch * seq, n_heads * v_dim), out_dtype),
        scratch_shapes=[
            pltpu.VMEM((2, seq, qk_w), BF16),
            pltpu.VMEM((seq // tk, v_dim + ONES_ROWS, tk), BF16),
            pltpu.VMEM((v_dim + ONES_ROWS, 2 * tq), F32),
            pltpu.VMEM((tk, 2 * tq), F32),
            pltpu.VMEM((tk, 2 * tq), F32),
        ],
        compiler_params=pltpu.CompilerParams(
            dimension_semantics=("parallel", "parallel"), vmem_limit_bytes=_vmem_limit(est)),
        name="diff_attn",
    )(qkv, qkv, qkv, lamv, g)


def _residual_layernorm(acc_ref, xres_ref, g, b, o32_ref, o16_ref, *, alpha, r0, n_rows, rows):
    for c in range(n_rows // rows):
        lo = r0 + c * rows
        y = alpha * xres_ref[lo:lo + rows, :] + acc_ref[lo:lo + rows, :]
        mu = jnp.mean(y, axis=-1, keepdims=True)
        d = y - mu
        var = jnp.mean(d * d, axis=-1, keepdims=True)
        out = d * lax.rsqrt(var + LN_EPS) * g + b
        o32_ref[lo:lo + rows, :] = out
        o16_ref[lo:lo + rows, :] = out.astype(o16_ref.dtype)


def _outproj_kernel(rg_ref, at_ref, w32_ref, xres_ref, g_ref, b_ref, o32_ref, o16_ref, acc_ref, w_ref,
                    *, alpha, sub):
    tm, half = rg_ref.shape

    @pl.when(pl.program_id(0) == 0)
    def _():
        w_ref[...] = w32_ref[...].astype(w_ref.dtype)

    g = g_ref[...]
    b = b_ref[...]
    for t in range(tm // sub):
        r0 = t * sub
        acc_ref[r0:r0 + sub, :] = (
            jnp.dot(rg_ref[r0:r0 + sub, :], w_ref[0:half, :], preferred_element_type=F32)
            + jnp.dot(at_ref[r0:r0 + sub, :], w_ref[half:, :], preferred_element_type=F32))
        _residual_layernorm(acc_ref, xres_ref, g, b, o32_ref, o16_ref, alpha=alpha, r0=r0, n_rows=sub,
                            rows=LN_ROWS)


def _outproj_ln(rg_out, at_out, w, layer, xres, g, b, *, alpha, tm, sub):
    T, half = rg_out.shape
    _, Kw, D = w.shape
    est = 2 * (2 * tm * half * 2 + tm * D * 4 + tm * D * 4 + tm * D * 2) + Kw * D * (4 + 2) \
        + 2 * tm * D * 4
    row = lambda width: pl.BlockSpec((tm, width), lambda m: (m, 0))
    full = lambda a: pl.BlockSpec(a.shape, lambda m: (0, 0))
    return pl.pallas_call(
        functools.partial(_outproj_kernel, alpha=alpha, sub=sub),
        grid=(T // tm,),
        in_specs=[row(half), row(half),
                  pl.BlockSpec((None, Kw, D), lambda m: (layer, 0, 0), pipeline_mode=pl.Buffered(1)),
                  row(D), full(g), full(b)],
        out_specs=[row(D), row(D)],
        out_shape=[jax.ShapeDtypeStruct((T, D), F32), jax.ShapeDtypeStruct((T, D), BF16)],
        scratch_shapes=[pltpu.VMEM((tm, D), F32), pltpu.VMEM((Kw, D), BF16)],
        compiler_params=pltpu.CompilerParams(
            dimension_semantics=("arbitrary",), vmem_limit_bytes=_vmem_limit(est)),
        name="outproj_ln",
    )(rg_out, at_out, w, xres, g, b)


def _ffn_up_kernel(x_ref, wg32_ref, wu32_ref, cwg_ref, cwu_ref, cbg_ref, cbu_ref, o_ref, pg_ref, pu_ref,
                   wg_ref, wu_ref, *, tiles_per_seq, rows):
    tm, tn = o_ref.shape
    pad = V7X_SUBLANES
    m = pl.program_id(1)

    @pl.when(m == 0)
    def _():
        wg_ref[...] = wg32_ref[...].astype(wg_ref.dtype)
        wu_ref[...] = wu32_ref[...].astype(wu_ref.dtype)

    @pl.when(m % tiles_per_seq == 0)
    def _():
        pg_ref[:, 0:pad, :] = jnp.zeros((pg_ref.shape[0], pad, V7X_LANES), F32)
        pu_ref[:, 0:pad, :] = jnp.zeros((pu_ref.shape[0], pad, V7X_LANES), F32)

    cwg = cwg_ref[...]
    cbg = cbg_ref[...]
    cwu = 0.5 * cwu_ref[...]
    cbu = 0.5 * cbu_ref[...]
    for c in range(tm // rows):
        r0 = c * rows
        xs = x_ref[r0:r0 + rows, :]
        _store_slabs(pg_ref, r0, jnp.dot(xs, wg_ref[...], preferred_element_type=F32))
        _store_slabs(pu_ref, r0, jnp.dot(xs, wu_ref[...], preferred_element_type=F32))
        gv = _causal_conv_rows(pg_ref, cwg, cbg, r0, rows)
        uv = _causal_conv_rows(pu_ref, cwu, cbu, r0, rows)
        o_ref[r0:r0 + rows, :] = _gelu_tanh_times(gv, uv).astype(o_ref.dtype)
    pg_ref[:, 0:pad, :] = pg_ref[:, tm:tm + pad, :]
    pu_ref[:, 0:pad, :] = pu_ref[:, tm:tm + pad, :]


def _ffn_up(x, w, layer, cw, cb, *, seq, d_ff, tm, tn, out_dtype):
    T, K = x.shape
    nt = d_ff // tn
    conv_k = cw.shape[0]
    assert seq % tm == 0 and d_ff % tn == 0
    est = 2 * (tm * K * 2 + 2 * K * tn * 4 + tm * tn * 2) + 2 * K * tn * 2 + 2 * (tm + 8) * tn * 4 \
        + 2 * tm * tn * 4
    vecg = lambda r: pl.BlockSpec((r, tn), lambda n, m: (0, n))
    vecu = lambda r: pl.BlockSpec((r, tn), lambda n, m: (0, n + nt))
    return pl.pallas_call(
        functools.partial(_ffn_up_kernel, tiles_per_seq=seq // tm, rows=min(PROJ_SUB_ROWS, tm)),
        grid=(nt, T // tm),
        in_specs=[
            pl.BlockSpec((tm, K), lambda n, m: (m, 0)),
            pl.BlockSpec((None, K, tn), lambda n, m: (layer, 0, n)),
            pl.BlockSpec((None, K, tn), lambda n, m: (layer, 0, n + nt)),
            vecg(conv_k), vecu(conv_k), vecg(1), vecu(1),
        ],
        out_specs=pl.BlockSpec((tm, tn), lambda n, m: (m, n)),
        out_shape=jax.ShapeDtypeStruct((T, d_ff), out_dtype),
        scratch_shapes=[pltpu.VMEM((tn // V7X_LANES, tm + V7X_SUBLANES, V7X_LANES), F32),
                        pltpu.VMEM((tn // V7X_LANES, tm + V7X_SUBLANES, V7X_LANES), F32),
                        pltpu.VMEM((K, tn), BF16), pltpu.VMEM((K, tn), BF16)],
        compiler_params=pltpu.CompilerParams(
            dimension_semantics=("parallel", "arbitrary"), vmem_limit_bytes=_vmem_limit(est)),
        name="ffn_up",
    )(x, w, w, cw, cw, cb, cb)


def _ffn_down_kernel(h_ref, w32_ref, xres_ref, g_ref, b_ref, o32_ref, o16_ref, acc_ref, w_ref, *, alpha, n_pre):
    s = pl.program_id(0)
    kc = w32_ref.shape[0]

    @pl.when(s < n_pre)
    def _():
        w_ref[pl.ds(pl.multiple_of(s * kc, kc), kc), :] = w32_ref[...].astype(w_ref.dtype)

    @pl.when(s >= n_pre)
    def _():
        tm = h_ref.shape[0]
        acc_ref[...] = jnp.dot(h_ref[...], w_ref[...], preferred_element_type=F32)
        _residual_layernorm(acc_ref, xres_ref, g_ref[...], b_ref[...], o32_ref, o16_ref, alpha=alpha, r0=0,
                            n_rows=tm, rows=LN_ROWS)


def _ffn_down_ln(h, w, layer, xres, g, b, *, alpha, tm, kc):
    T, K = h.shape
    D = w.shape[2]
    assert T % tm == 0 and K % kc == 0
    n_pre = K // kc
    est = 2 * (tm * K * 2 + kc * D * 4 + tm * D * 4 + tm * D * 4 + tm * D * 2) + K * D * 2 + 2 * tm * D * 4
    tile = lambda s: jnp.maximum(s - n_pre, 0)
    row = lambda width: pl.BlockSpec((tm, width), lambda s: (tile(s), 0))
    full = lambda a: pl.BlockSpec(a.shape, lambda s: (0, 0))
    return pl.pallas_call(
        functools.partial(_ffn_down_kernel, alpha=alpha, n_pre=n_pre),
        grid=(n_pre + T // tm,),
        in_specs=[row(K),
                  pl.BlockSpec((None, kc, D), lambda s: (layer, jnp.minimum(s, n_pre - 1), 0)),
                  row(D), full(g), full(b)],
        out_specs=[row(D), row(D)],
        out_shape=[jax.ShapeDtypeStruct((T, D), F32), jax.ShapeDtypeStruct((T, D), BF16)],
        scratch_shapes=[pltpu.VMEM((tm, D), F32), pltpu.VMEM((K, D), BF16)],
        compiler_params=pltpu.CompilerParams(
            dimension_semantics=("arbitrary",), vmem_limit_bytes=_vmem_limit(est)),
        name="ffn_down_ln",
    )(h, w, xres, g, b)


def _block_diag(w, tile):
    G, d, _ = w.shape
    per = tile // d
    w4 = w.reshape(G // per, per, d, d)
    eye = jnp.eye(per, dtype=w.dtype)
    return jnp.einsum('jipq,ik->jipkq', w4, eye).reshape(G // per, tile, tile)


def kernel(x, w_in, rg_conv_w, rg_conv_b, rg_gate_a_w, rg_gate_a_b, rg_gate_x_w, rg_gate_x_b, rg_lambda,
           lam_q1, lam_k1, lam_q2, lam_k2, subln_g, w_out, ln_mix_g, ln_mix_b, w_up, ffn_conv_w,
           ffn_conv_b, w_down, ln_ffn_g, ln_ffn_b):
    B, S, D = x.shape
    depth = w_in.shape[0]
    rg_w = rg_conv_w.shape[2]
    head_dim = lam_q1.shape[1]
    v_dim = subln_g.shape[1]
    mix_w = w_out.shape[1]
    attn_w = mix_w - rg_w
    n_heads = attn_w // v_dim
    qkv_w = w_in.shape[2] - 2 * rg_w
    d_ff = w_down.shape[1]
    T = B * S
    alpha = (2.0 * depth) ** 0.25

    x32 = x.reshape(T, D)
    x16 = x32.astype(BF16)
    for i in range(depth):
        wg = jnp.concatenate([_block_diag(rg_gate_a_w[i], V7X_MXU_DIM),
                              _block_diag(rg_gate_x_w[i], V7X_MXU_DIM)], axis=-1).astype(BF16)
        lam_init = 0.8 - 0.6 * math.exp(-0.3 * i)
        lamv = jnp.stack([lam_q1[i], lam_k1[i], lam_q2[i], lam_k2[i]])

        rg_out, qkv = _inproj_rglru(x16, w_in, i, rg_conv_w[i], rg_conv_b[i][None], wg,
                                    rg_gate_a_b[i][None], rg_gate_x_b[i][None], rg_lambda[i][None],
                                    batch=B, seq=S, width=rg_w, qkv_width=qkv_w, out_dtype=BF16)
        at_out = _diff_attn(qkv, lamv, subln_g[i][None], batch=B, seq=S, n_heads=n_heads,
                            head_dim=head_dim, v_dim=v_dim, lam_init=lam_init, out_dtype=BF16,
                            tq=min(512, S), tk=min(512, S))
        x32, x16 = _outproj_ln(rg_out, at_out, w_out, i, x32, ln_mix_g[i][None], ln_mix_b[i][None],
                               alpha=alpha, tm=512, sub=256)
        hmid = _ffn_up(x16, w_up, i, ffn_conv_w[i], ffn_conv_b[i][None], seq=S, d_ff=d_ff, tm=S,
                       tn=512, out_dtype=BF16)
        x32, x16 = _ffn_down_ln(hmid, w_down, i, x32, ln_ffn_g[i][None], ln_ffn_b[i][None], alpha=alpha,
                                tm=256, kc=512)
    return x32.reshape(B, S, D)
```

```python
import functools
import math

import jax
import jax.numpy as jnp
from jax import lax
from jax.experimental import pallas as pl
from jax.experimental.pallas import tpu as pltpu

F32 = jnp.float32
BF16 = jnp.bfloat16

V7X_LANES = 128
V7X_SUBLANES = 8
V7X_MXU_DIM = 256
V7X_VMEM_BYTES = 64 * 1024 * 1024
VMEM_TEMP_BYTES = 8 * 1024 * 1024
VMEM_UNSCOPED_BYTES = 4 * 1024 * 1024
VMEM_MIN_REQUEST_BYTES = 16 * 1024 * 1024

RG_CHUNK_ROWS = 128
PROJ_SUB_ROWS = 256
LN_ROWS = 64

RG_C = 8.0
LN_EPS = 1e-5
RMS_EPS = 1e-5
MASK_VALUE = -1e30


def _vmem_limit(nbytes):
    return int(min(V7X_VMEM_BYTES - VMEM_UNSCOPED_BYTES, max(nbytes + VMEM_TEMP_BYTES, VMEM_MIN_REQUEST_BYTES)))


def _gelu_tanh_times(x, half_y):
    c = math.sqrt(2.0 / math.pi)
    t = jnp.tanh(x * (c + (c * 0.044715) * (x * x)))
    return (x + x * t) * half_y


def _causal_conv_rows(pad_ref, cw, cb, r0, rows):
    conv_k = cw.shape[0]
    pad = V7X_SUBLANES
    strips = []
    for s in range(pad_ref.shape[0]):
        lanes = slice(s * V7X_LANES, (s + 1) * V7X_LANES)
        out = cb[:, lanes] + cw[conv_k - 1:conv_k, lanes] * pad_ref[s, pl.ds(r0 + pad, rows), :]
        for k in range(conv_k - 1):
            start = r0 + pad - (conv_k - 1) + k
            out = out + cw[k:k + 1, lanes] * pad_ref[s, pl.ds(start, rows, stride=1), :]
        strips.append(out)
    return strips[0] if len(strips) == 1 else jnp.concatenate(strips, axis=1)


def _store_slabs(pad_ref, r0, value):
    rows = value.shape[0]
    for s in range(pad_ref.shape[0]):
        pad_ref[s, V7X_SUBLANES + r0:V7X_SUBLANES + r0 + rows, :] = value[:, s * V7X_LANES:(s + 1) * V7X_LANES]


def _inproj_rglru_kernel(x16_ref, wx32_ref, wgt32_ref, wq32a_ref, wq32b_ref, wq32c_ref, cw_ref, cb_ref, wg_ref,
                         ba_ref, bx_ref, lam_ref, o_ref, qkv_ref, xpad_ref, gate_ref, hs_ref,
                         wx_ref, wgt_ref, wq_ref, *, rows, sub_rows):
    S, C = o_ref.shape
    pad = V7X_SUBLANES
    xpad_ref[:, 0:pad, :] = jnp.zeros((xpad_ref.shape[0], pad, V7X_LANES), F32)
    wx_ref[...] = wx32_ref[...].astype(wx_ref.dtype)
    wgt_ref[...] = wgt32_ref[...].astype(wgt_ref.dtype)
    for part, ref in enumerate((wq32a_ref, wq32b_ref, wq32c_ref)):
        wq_ref[:, part * C:(part + 1) * C] = ref[...].astype(wq_ref.dtype)

    lam = lam_ref[...]
    log_sig = jnp.minimum(lam, 0.0) - jnp.log1p(jnp.exp(-jnp.abs(lam)))
    cw = cw_ref[...]
    cb = cb_ref[...]
    ba = ba_ref[...]
    bx = bx_ref[...]
    wg = wg_ref[0]
    n_groups = rows // V7X_SUBLANES
    sub3 = lax.broadcasted_iota(jnp.int32, (n_groups, V7X_SUBLANES, C), 1)
    n_sub = S // sub_rows

    def chunk(t0, h_prev):
        u = _causal_conv_rows(xpad_ref, cw, cb, t0, rows)
        pre = jnp.dot(u.astype(BF16), wg, preferred_element_type=F32)
        r = jax.nn.sigmoid(pre[:, :C] + ba)
        ig = jax.nn.sigmoid(pre[:, C:] + bx)
        log_a = RG_C * r * log_sig
        a = jnp.exp(log_a)
        th = jnp.tanh(log_a)
        m2 = -2.0 * th / (1.0 - th)
        mult = jnp.where(m2 > 0.0, m2 * lax.rsqrt(m2), 0.0)
        b = mult * (ig * u)
        a = a.reshape(n_groups, V7X_SUBLANES, C)
        b = b.reshape(n_groups, V7X_SUBLANES, C)
        for s in (1, 2, 4):
            keep = sub3 >= s
            a_sh = jnp.where(keep, pltpu.roll(a, s, axis=1), 1.0)
            b_sh = jnp.where(keep, pltpu.roll(b, s, axis=1), 0.0)
            b = a * b_sh + b
            a = a * a_sh
        a = a.reshape(rows, C)
        b = b.reshape(rows, C)
        h = h_prev
        for g in range(n_groups):
            lo = g * V7X_SUBLANES
            hg = a[lo:lo + V7X_SUBLANES, :] * h + b[lo:lo + V7X_SUBLANES, :]
            h = hg[V7X_SUBLANES - 1:V7X_SUBLANES, :]
            hs_ref[t0 + lo:t0 + lo + V7X_SUBLANES, :] = hg
        gate = gate_ref[t0:t0 + rows, :]
        o_ref[t0:t0 + rows, :] = _gelu_tanh_times(gate, 0.5 * hs_ref[t0:t0 + rows, :]).astype(o_ref.dtype)
        return h

    h = jnp.zeros((1, C), F32)
    for si in range(n_sub):
        r0 = si * sub_rows
        xs = x16_ref[r0:r0 + sub_rows, :]
        _store_slabs(xpad_ref, r0, jnp.dot(xs, wx_ref[...], preferred_element_type=F32))
        gate_ref[r0:r0 + sub_rows, :] = jnp.dot(xs, wgt_ref[...], preferred_element_type=F32)
        qkv_ref[r0:r0 + sub_rows, :] = jnp.dot(xs, wq_ref[...],
                                               preferred_element_type=F32).astype(qkv_ref.dtype)
        for ci in range(sub_rows // rows):
            h = chunk(r0 + ci * rows, h)


def _inproj_rglru(x16, w_in, layer, cw, cb, wg, ba, bx, lam, *, batch, seq, width, qkv_width, out_dtype):
    C = wg.shape[2]
    nct = width // C
    T, D = x16.shape
    nqc = qkv_width // nct
    q_parts = nqc // C
    q_off = 2 * width // C
    conv_k = cw.shape[1]
    assert qkv_width % nct == 0 and nqc == 3 * C
    vec = lambda: pl.BlockSpec((None, 1, C), lambda b, c: (layer, 0, c))
    wq_spec = lambda part: pl.BlockSpec((None, D, C), lambda b, c: (layer, 0, q_off + q_parts * c + part))
    est = 2 * (seq * D * 2 + 2 * D * C * 4 + D * nqc * 4 + seq * C * 2 + seq * nqc * 2 + C * 2 * C * 2) \
        + 2 * D * C * 2 + D * nqc * 2 + (3 * seq + 8) * C * 4 + PROJ_SUB_ROWS * nqc * 4
    return pl.pallas_call(
        functools.partial(_inproj_rglru_kernel, rows=RG_CHUNK_ROWS, sub_rows=PROJ_SUB_ROWS),
        grid=(batch, nct),
        in_specs=[
            pl.BlockSpec((seq, D), lambda b, c: (b, 0)),
            pl.BlockSpec((None, D, C), lambda b, c: (layer, 0, c)),
            pl.BlockSpec((None, D, C), lambda b, c: (layer, 0, nct + c)),
            wq_spec(0), wq_spec(1), wq_spec(2),
            pl.BlockSpec((None, conv_k, C), lambda b, c: (layer, 0, c)),
            vec(),
            pl.BlockSpec((None, 1, C, 2 * C), lambda b, c: (layer, c, 0, 0)),
            vec(), vec(), vec(),
        ],
        out_specs=[pl.BlockSpec((seq, C), lambda b, c: (b, c)),
                   pl.BlockSpec((seq, nqc), lambda b, c: (b, c))],
        out_shape=[jax.ShapeDtypeStruct((batch * seq, width), out_dtype),
                   jax.ShapeDtypeStruct((T, qkv_width), BF16)],
        scratch_shapes=[pltpu.VMEM((C // V7X_LANES, seq + V7X_SUBLANES, V7X_LANES), F32),
                        pltpu.VMEM((seq, C), F32), pltpu.VMEM((seq, C), F32),
                        pltpu.VMEM((D, C), BF16), pltpu.VMEM((D, C), BF16), pltpu.VMEM((D, nqc), BF16)],
        compiler_params=pltpu.CompilerParams(
            dimension_semantics=("parallel", "parallel"), vmem_limit_bytes=_vmem_limit(est)),
        name="inproj_rglru",
    )(x16, w_in, w_in, w_in, w_in, w_in, cw, cb, wg, ba, bx, lam)


POS_SPLIT = 256
ONES_ROWS = 16


def _attn_kernel(q_ref, k_ref, v_ref, lamv_ref, g_ref, o_ref, ka_ref, vt_ref, acc_ref, s0_ref, s1_ref,
                 *, tq, tk, n_heads, head_dim, scale, lam_init):
    h = pl.program_id(1)
    S, d2 = q_ref.shape
    V = v_ref.shape[1]
    n_kv = S // tk
    n_q = S // tq
    s_bufs = (s0_ref, s1_ref)

    def extra_cols(shape, c, lo_val, hi_val):
        lane = lax.broadcasted_iota(jnp.int32, shape, 1)
        base = head_dim * (1 - c)
        return jnp.where(lane == base, lo_val, jnp.where(lane == base + 1, hi_val, 0.0))

    def own_half(shape, c):
        lane = lax.broadcasted_iota(jnp.int32, shape, 1)
        return (lane < head_dim) if c == 0 else (lane >= head_dim)

    for jb in range(n_kv):
        rows = slice(jb * tk, (jb + 1) * tk)
        vt_ref[jb, 0:V, :] = v_ref[rows, :].astype(F32).T.astype(vt_ref.dtype)
        vt_ref[jb, V:V + ONES_ROWS, :] = jnp.ones((ONES_ROWS, tk), vt_ref.dtype)
        kb = k_ref[rows, :]
        pos = lax.broadcasted_iota(jnp.int32, (tk, d2), 0) + jb * tk
        j_lo = (pos % POS_SPLIT).astype(F32)
        j_hi = (pos // POS_SPLIT).astype(F32)
        for c in range(2):
            ka_ref[c, rows, :] = jnp.where(own_half((tk, d2), c), kb,
                                           extra_cols((tk, d2), c, j_lo, j_hi).astype(kb.dtype))

    slope = jnp.exp2(jnp.full((1, 1), -8.0 / n_heads, F32) * (h + 1).astype(F32))
    lv = lamv_ref[...]
    lam = (jnp.exp(jnp.sum(lv[0:1, :] * lv[1:2, :], axis=-1, keepdims=True))
           - jnp.exp(jnp.sum(lv[2:3, :] * lv[3:4, :], axis=-1, keepdims=True)) + lam_init)
    gain = g_ref[...] * (1.0 - lam_init)

    q_aug = {}

    def augmented_queries(qi):
        if qi not in q_aug:
            q = q_ref[qi * tq:(qi + 1) * tq, :] * jnp.asarray(scale, q_ref.dtype)
            q_aug[qi] = [jnp.where(own_half(q.shape, c), q,
                                   extra_cols(q.shape, c, slope, slope * POS_SPLIT).astype(q.dtype))
                         for c in range(2)]
        return q_aug[qi]

    def on_diagonal(qi, j):
        return tq == tk and j == qi and tk % 2 == 0

    def scores(qi, j, s_ref):
        qa = augmented_queries(qi)
        half_k = tk // 2
        half_q = tq // 2
        for c in range(2):
            keys = ka_ref[c, j * tk:(j + 1) * tk, :]
            if on_diagonal(qi, j):
                s_ref[0:half_k, c * tq:(c + 1) * tq] = lax.dot_general(
                    keys[0:half_k], qa[c], (((1,), (1,)), ((), ())), preferred_element_type=F32)
                s_ref[half_k:tk, c * tq:c * tq + half_q] = jnp.full((tk - half_k, half_q), MASK_VALUE, F32)
                s_ref[half_k:tk, c * tq + half_q:(c + 1) * tq] = lax.dot_general(
                    keys[half_k:tk], qa[c][half_q:tq], (((1,), (1,)), ((), ())), preferred_element_type=F32)
            else:
                s_ref[:, c * tq:(c + 1) * tq] = lax.dot_general(
                    keys, qa[c], (((1,), (1,)), ((), ())), preferred_element_type=F32)

    def weighted_values(qi, j, p):
        vt = vt_ref[j]
        if not on_diagonal(qi, j):
            return jnp.dot(vt, p, preferred_element_type=F32)
        half_k = tk // 2
        half_q = tq // 2
        parts = []
        for c in range(2):
            lo = slice(c * tq, c * tq + half_q)
            hi = slice(c * tq + half_q, (c + 1) * tq)
            parts.append(jnp.dot(vt[:, 0:half_k], p[0:half_k, lo], preferred_element_type=F32))
            parts.append(jnp.dot(vt, p[:, hi], preferred_element_type=F32))
        return jnp.concatenate(parts, axis=1)

    def softmax_pv(qi, j, s_ref, m_old):
        s = s_ref[...]
        if (j + 1) * tk - 1 > qi * tq:
            key_pos = lax.broadcasted_iota(jnp.int32, (tk, 2 * tq), 0) + j * tk
            col = lax.broadcasted_iota(jnp.int32, (tk, 2 * tq), 1)
            query_pos = jnp.where(col >= tq, col - tq, col) + qi * tq
            s = jnp.where(key_pos <= query_pos, s, MASK_VALUE)
        p_dtype = vt_ref.dtype
        if m_old is None:
            m_new = jnp.max(s, axis=0, keepdims=True)
            acc_ref[...] = weighted_values(qi, j, jnp.exp(s - m_new).astype(p_dtype))
        else:
            m_new = jnp.maximum(m_old, jnp.max(s, axis=0, keepdims=True))
            alpha = jnp.exp(m_old - m_new)
            acc_ref[...] = alpha * acc_ref[...] + weighted_values(qi, j, jnp.exp(s - m_new).astype(p_dtype))
        return m_new

    def finish(qi):
        ot = (acc_ref[0:V, 0:tq] / acc_ref[V:V + 1, 0:tq]
              - lam * (acc_ref[0:V, tq:2 * tq] / acc_ref[V:V + 1, tq:2 * tq]))
        ms = jnp.mean(ot * ot, axis=0, keepdims=True)
        o = (ot * lax.rsqrt(ms + RMS_EPS)).T * gain
        o_ref[qi * tq:(qi + 1) * tq, :] = o.astype(o_ref.dtype)

    pairs = [(qi, j) for qi in range(n_q) for j in range(((qi + 1) * tq + tk - 1) // tk)]
    scores(*pairs[0], s_bufs[0])
    m_run = None
    for t, (qi, j) in enumerate(pairs):
        if t + 1 < len(pairs):
            scores(*pairs[t + 1], s_bufs[(t + 1) % 2])
        m_run = softmax_pv(qi, j, s_bufs[t % 2], m_run)
        if t + 1 == len(pairs) or pairs[t + 1][0] != qi:
            finish(qi)
            m_run = None


def _diff_attn(qkv, lamv, g, layer, *, batch, seq, n_heads, head_dim, v_dim, lam_init, out_dtype, tq, tk):
    qk_w = 2 * head_dim
    assert qk_w == v_dim and seq % tq == 0 and seq % tk == 0 and seq % POS_SPLIT == 0
    assert seq // POS_SPLIT <= POS_SPLIT
    return pl.pallas_call(
        functools.partial(_attn_kernel, tq=tq, tk=tk, n_heads=n_heads, head_dim=head_dim,
                          scale=head_dim ** -0.5, lam_init=lam_init),
        grid=(batch, n_heads),
        in_specs=[
            pl.BlockSpec((seq, qk_w), lambda b, h: (b, h)),
            pl.BlockSpec((seq, qk_w), lambda b, h: (b, n_heads + h)),
            pl.BlockSpec((seq, v_dim), lambda b, h: (b, 2 * n_heads + h)),
            pl.BlockSpec((None,) + lamv.shape[1:], lambda b, h: (layer, 0, 0)),
            pl.BlockSpec((None, 1, v_dim), lambda b, h: (layer, 0, 0)),
        ],
        out_specs=pl.BlockSpec((seq, v_dim), lambda b, h: (b, h)),
        out_shape=jax.ShapeDtypeStruct((batch * seq, n_heads * v_dim), out_dtype),
        scratch_shapes=[
            pltpu.VMEM((2, seq, qk_w), BF16),
            pltpu.VMEM((seq // tk, v_dim + ONES_ROWS, tk), BF16),
            pltpu.VMEM((v_dim + ONES_ROWS, 2 * tq), F32),
            pltpu.VMEM((tk, 2 * tq), F32),
            pltpu.VMEM((tk, 2 * tq), F32),
        ],
        compiler_params=pltpu.CompilerParams(
            dimension_semantics=("parallel", "parallel"),
            vmem_limit_bytes=V7X_VMEM_BYTES - VMEM_UNSCOPED_BYTES),
        name="diff_attn",
    )(qkv, qkv, qkv, lamv, g)


def _residual_layernorm(acc_ref, xres_ref, g, b, o32_ref, o16_ref, *, alpha, r0, n_rows, rows):
    for c in range(n_rows // rows):
        lo = r0 + c * rows
        y = alpha * xres_ref[lo:lo + rows, :] + acc_ref[lo:lo + rows, :]
        mu = jnp.mean(y, axis=-1, keepdims=True)
        d = y - mu
        var = jnp.mean(d * d, axis=-1, keepdims=True)
        out = d * lax.rsqrt(var + LN_EPS) * g + b
        o32_ref[lo:lo + rows, :] = out
        o16_ref[lo:lo + rows, :] = out.astype(o16_ref.dtype)


def _outproj_kernel(rg_ref, at_ref, w32_ref, xres_ref, g_ref, b_ref, o32_ref, o16_ref, acc_ref, w_ref,
                    *, alpha, sub):
    tm, half = rg_ref.shape

    @pl.when(pl.program_id(0) == 0)
    def _():
        w_ref[...] = w32_ref[...].astype(w_ref.dtype)

    g = g_ref[...]
    b = b_ref[...]
    for t in range(tm // sub):
        r0 = t * sub
        acc_ref[r0:r0 + sub, :] = (
            jnp.dot(rg_ref[r0:r0 + sub, :], w_ref[0:half, :], preferred_element_type=F32)
            + jnp.dot(at_ref[r0:r0 + sub, :], w_ref[half:, :], preferred_element_type=F32))
        _residual_layernorm(acc_ref, xres_ref, g, b, o32_ref, o16_ref, alpha=alpha, r0=r0, n_rows=sub,
                            rows=LN_ROWS)


def _outproj_ln(rg_out, at_out, w, layer, xres, g, b, *, alpha, tm, sub):
    T, half = rg_out.shape
    _, Kw, D = w.shape
    est = 2 * (2 * tm * half * 2 + tm * D * 4 + tm * D * 4 + tm * D * 2) + Kw * D * (4 + 2) \
        + 2 * tm * D * 4
    row = lambda width: pl.BlockSpec((tm, width), lambda m: (m, 0))
    full = lambda a: pl.BlockSpec((None,) + a.shape[1:], lambda m: (layer, 0, 0))
    return pl.pallas_call(
        functools.partial(_outproj_kernel, alpha=alpha, sub=sub),
        grid=(T // tm,),
        in_specs=[row(half), row(half),
                  pl.BlockSpec((None, Kw, D), lambda m: (layer, 0, 0), pipeline_mode=pl.Buffered(1)),
                  row(D), full(g), full(b)],
        out_specs=[row(D), row(D)],
        out_shape=[jax.ShapeDtypeStruct((T, D), F32), jax.ShapeDtypeStruct((T, D), BF16)],
        scratch_shapes=[pltpu.VMEM((tm, D), F32), pltpu.VMEM((Kw, D), BF16)],
        compiler_params=pltpu.CompilerParams(
            dimension_semantics=("arbitrary",), vmem_limit_bytes=_vmem_limit(est)),
        name="outproj_ln",
    )(rg_out, at_out, w, xres, g, b)


def _ffn_up_kernel(x_ref, wg32_ref, wu32_ref, cwg_ref, cwu_ref, cbg_ref, cbu_ref, o_ref, pg_ref, pu_ref,
                   wg_ref, wu_ref, *, tiles_per_seq, rows):
    tm, tn = o_ref.shape
    pad = V7X_SUBLANES
    m = pl.program_id(1)

    @pl.when(m == 0)
    def _():
        wg_ref[...] = wg32_ref[...].astype(wg_ref.dtype)
        wu_ref[...] = wu32_ref[...].astype(wu_ref.dtype)

    @pl.when(m % tiles_per_seq == 0)
    def _():
        pg_ref[:, 0:pad, :] = jnp.zeros((pg_ref.shape[0], pad, V7X_LANES), F32)
        pu_ref[:, 0:pad, :] = jnp.zeros((pu_ref.shape[0], pad, V7X_LANES), F32)

    cwg = cwg_ref[...]
    cbg = cbg_ref[...]
    cwu = 0.5 * cwu_ref[...]
    cbu = 0.5 * cbu_ref[...]
    for c in range(tm // rows):
        r0 = c * rows
        xs = x_ref[r0:r0 + rows, :]
        _store_slabs(pg_ref, r0, jnp.dot(xs, wg_ref[...], preferred_element_type=F32))
        _store_slabs(pu_ref, r0, jnp.dot(xs, wu_ref[...], preferred_element_type=F32))
        gv = _causal_conv_rows(pg_ref, cwg, cbg, r0, rows)
        uv = _causal_conv_rows(pu_ref, cwu, cbu, r0, rows)
        o_ref[r0:r0 + rows, :] = _gelu_tanh_times(gv, uv).astype(o_ref.dtype)
    pg_ref[:, 0:pad, :] = pg_ref[:, tm:tm + pad, :]
    pu_ref[:, 0:pad, :] = pu_ref[:, tm:tm + pad, :]


def _ffn_up(x, w, layer, cw, cb, *, seq, d_ff, tm, tn, out_dtype):
    T, K = x.shape
    nt = d_ff // tn
    conv_k = cw.shape[1]
    assert seq % tm == 0 and d_ff % tn == 0
    est = 2 * (tm * K * 2 + 2 * K * tn * 4 + tm * tn * 2) + 2 * K * tn * 2 + 2 * (tm + 8) * tn * 4 \
        + 2 * tm * tn * 4
    vecg = lambda r: pl.BlockSpec((None, r, tn), lambda n, m: (layer, 0, n))
    vecu = lambda r: pl.BlockSpec((None, r, tn), lambda n, m: (layer, 0, n + nt))
    return pl.pallas_call(
        functools.partial(_ffn_up_kernel, tiles_per_seq=seq // tm, rows=min(PROJ_SUB_ROWS, tm)),
        grid=(nt, T // tm),
        in_specs=[
            pl.BlockSpec((tm, K), lambda n, m: (m, 0)),
            pl.BlockSpec((None, K, tn), lambda n, m: (layer, 0, n)),
            pl.BlockSpec((None, K, tn), lambda n, m: (layer, 0, n + nt)),
            vecg(conv_k), vecu(conv_k), vecg(1), vecu(1),
        ],
        out_specs=pl.BlockSpec((tm, tn), lambda n, m: (m, n)),
        out_shape=jax.ShapeDtypeStruct((T, d_ff), out_dtype),
        scratch_shapes=[pltpu.VMEM((tn // V7X_LANES, tm + V7X_SUBLANES, V7X_LANES), F32),
                        pltpu.VMEM((tn // V7X_LANES, tm + V7X_SUBLANES, V7X_LANES), F32),
                        pltpu.VMEM((K, tn), BF16), pltpu.VMEM((K, tn), BF16)],
        compiler_params=pltpu.CompilerParams(
            dimension_semantics=("parallel", "arbitrary"), vmem_limit_bytes=_vmem_limit(est)),
        name="ffn_up",
    )(x, w, w, cw, cw, cb, cb)


def _ffn_down_kernel(h_ref, w32_ref, xres_ref, g_ref, b_ref, o32_ref, o16_ref, acc_ref, w_ref, *, alpha, n_pre):
    s = pl.program_id(0)
    kc = w32_ref.shape[0]

    @pl.when(s < n_pre)
    def _():
        w_ref[pl.ds(pl.multiple_of(s * kc, kc), kc), :] = w32_ref[...].astype(w_ref.dtype)

    @pl.when(s >= n_pre)
    def _():
        tm = h_ref.shape[0]
        acc_ref[...] = jnp.dot(h_ref[...], w_ref[...], preferred_element_type=F32)
        _residual_layernorm(acc_ref, xres_ref, g_ref[...], b_ref[...], o32_ref, o16_ref, alpha=alpha, r0=0,
                            n_rows=tm, rows=LN_ROWS)


def _ffn_down_ln(h, w, layer, xres, g, b, *, alpha, tm, kc):
    T, K = h.shape
    D = w.shape[2]
    assert T % tm == 0 and K % kc == 0
    n_pre = K // kc
    est = 2 * (tm * K * 2 + kc * D * 4 + tm * D * 4 + tm * D * 4 + tm * D * 2) + K * D * 2 + 2 * tm * D * 4
    tile = lambda s: jnp.maximum(s - n_pre, 0)
    row = lambda width: pl.BlockSpec((tm, width), lambda s: (tile(s), 0))
    full = lambda a: pl.BlockSpec((None,) + a.shape[1:], lambda s: (layer, 0, 0))
    return pl.pallas_call(
        functools.partial(_ffn_down_kernel, alpha=alpha, n_pre=n_pre),
        grid=(n_pre + T // tm,),
        in_specs=[row(K),
                  pl.BlockSpec((None, kc, D), lambda s: (layer, jnp.minimum(s, n_pre - 1), 0)),
                  row(D), full(g), full(b)],
        out_specs=[row(D), row(D)],
        out_shape=[jax.ShapeDtypeStruct((T, D), F32), jax.ShapeDtypeStruct((T, D), BF16)],
        scratch_shapes=[pltpu.VMEM((tm, D), F32), pltpu.VMEM((K, D), BF16)],
        compiler_params=pltpu.CompilerParams(
            dimension_semantics=("arbitrary",), vmem_limit_bytes=_vmem_limit(est)),
        name="ffn_down_ln",
    )(h, w, xres, g, b)


def _block_diag(w, tile):
    G, d, _ = w.shape
    per = tile // d
    w4 = w.reshape(G // per, per, d, d)
    eye = jnp.eye(per, dtype=w.dtype)
    return jnp.einsum('jipq,ik->jipkq', w4, eye).reshape(G // per, tile, tile)


def kernel(x, w_in, rg_conv_w, rg_conv_b, rg_gate_a_w, rg_gate_a_b, rg_gate_x_w, rg_gate_x_b, rg_lambda,
           lam_q1, lam_k1, lam_q2, lam_k2, subln_g, w_out, ln_mix_g, ln_mix_b, w_up, ffn_conv_w,
           ffn_conv_b, w_down, ln_ffn_g, ln_ffn_b):
    B, S, D = x.shape
    depth = w_in.shape[0]
    rg_w = rg_conv_w.shape[2]
    head_dim = lam_q1.shape[1]
    v_dim = subln_g.shape[1]
    mix_w = w_out.shape[1]
    attn_w = mix_w - rg_w
    n_heads = attn_w // v_dim
    qkv_w = w_in.shape[2] - 2 * rg_w
    d_ff = w_down.shape[1]
    T = B * S
    alpha = (2.0 * depth) ** 0.25

    vec3 = lambda a: a.reshape(depth, 1, a.shape[-1])
    n_blocks, blk = rg_gate_a_w.shape[1], rg_gate_a_w.shape[2]
    gate_tiles = lambda w: _block_diag(w.reshape(depth * n_blocks, blk, blk), V7X_MXU_DIM)
    wg = jnp.concatenate([gate_tiles(rg_gate_a_w), gate_tiles(rg_gate_x_w)], axis=-1).astype(BF16)
    wg = wg.reshape(depth, rg_w // V7X_MXU_DIM, V7X_MXU_DIM, 2 * V7X_MXU_DIM)
    lamv = jnp.stack([lam_q1, lam_k1, lam_q2, lam_k2], axis=1)
    rg_cb, rg_ba, rg_bx, rg_lam = vec3(rg_conv_b), vec3(rg_gate_a_b), vec3(rg_gate_x_b), vec3(rg_lambda)
    sub_g, mix_g, mix_b = vec3(subln_g), vec3(ln_mix_g), vec3(ln_mix_b)
    ffn_cb, ffn_g, ffn_b = vec3(ffn_conv_b), vec3(ln_ffn_g), vec3(ln_ffn_b)

    x32 = x.reshape(T, D)
    x16 = x32.astype(BF16)
    for i in range(depth):
        lam_init = 0.8 - 0.6 * math.exp(-0.3 * i)
        rg_out, qkv = _inproj_rglru(x16, w_in, i, rg_conv_w, rg_cb, wg, rg_ba, rg_bx, rg_lam,
                                    batch=B, seq=S, width=rg_w, qkv_width=qkv_w, out_dtype=BF16)
        at_out = _diff_attn(qkv, lamv, sub_g, i, batch=B, seq=S, n_heads=n_heads,
                            head_dim=head_dim, v_dim=v_dim, lam_init=lam_init, out_dtype=BF16,
                            tq=min(512, S), tk=min(512, S))
        x32, x16 = _outproj_ln(rg_out, at_out, w_out, i, x32, mix_g, mix_b, alpha=alpha, tm=512, sub=256)
        hmid = _ffn_up(x16, w_up, i, ffn_conv_w, ffn_cb, seq=S, d_ff=d_ff, tm=S, tn=512, out_dtype=BF16)
        x32, x16 = _ffn_down_ln(hmid, w_down, i, x32, ffn_g, ffn_b, alpha=alpha, tm=256, kc=512)
    return x32.reshape(B, S, D)
```

```python
import functools
import math

import jax
import jax.numpy as jnp
from jax import lax
from jax.experimental import pallas as pl
from jax.experimental.pallas import tpu as pltpu

F32 = jnp.float32
BF16 = jnp.bfloat16

V7X_LANES = 128
V7X_SUBLANES = 8
V7X_MXU_DIM = 256
V7X_VMEM_BYTES = 64 * 1024 * 1024
VMEM_TEMP_BYTES = 8 * 1024 * 1024
VMEM_UNSCOPED_BYTES = 4 * 1024 * 1024
VMEM_MIN_REQUEST_BYTES = 16 * 1024 * 1024

RG_CHUNK_ROWS = 128
PROJ_SUB_ROWS = 256
LN_ROWS = 64

RG_C = 8.0
LN_EPS = 1e-5
RMS_EPS = 1e-5
MASK_VALUE = -1e30


def _vmem_limit(nbytes):
    return int(min(V7X_VMEM_BYTES - VMEM_UNSCOPED_BYTES, max(nbytes + VMEM_TEMP_BYTES, VMEM_MIN_REQUEST_BYTES)))


def _gelu_tanh_times(x, half_y):
    c = math.sqrt(2.0 / math.pi)
    t = jnp.tanh(x * (c + (c * 0.044715) * (x * x)))
    return (x + x * t) * half_y


def _causal_conv_rows(pad_ref, cw, cb, r0, rows):
    conv_k = cw.shape[0]
    pad = V7X_SUBLANES
    strips = []
    for s in range(pad_ref.shape[0]):
        lanes = slice(s * V7X_LANES, (s + 1) * V7X_LANES)
        out = cb[:, lanes] + cw[conv_k - 1:conv_k, lanes] * pad_ref[s, pl.ds(r0 + pad, rows), :]
        for k in range(conv_k - 1):
            start = r0 + pad - (conv_k - 1) + k
            out = out + cw[k:k + 1, lanes] * pad_ref[s, pl.ds(start, rows, stride=1), :]
        strips.append(out)
    return strips[0] if len(strips) == 1 else jnp.concatenate(strips, axis=1)


def _store_slabs(pad_ref, r0, value):
    rows = value.shape[0]
    for s in range(pad_ref.shape[0]):
        pad_ref[s, V7X_SUBLANES + r0:V7X_SUBLANES + r0 + rows, :] = value[:, s * V7X_LANES:(s + 1) * V7X_LANES]


def _inproj_rglru_kernel(x16_ref, wx32_ref, wgt32_ref, wq32a_ref, wq32b_ref, wq32c_ref, cw_ref, cb_ref, wg_ref,
                         ba_ref, bx_ref, lam_ref, o_ref, qkv_ref, xpad_ref, gate_ref, hs_ref,
                         wx_ref, wgt_ref, wq_ref, *, rows, sub_rows):
    S, C = o_ref.shape
    pad = V7X_SUBLANES
    xpad_ref[:, 0:pad, :] = jnp.zeros((xpad_ref.shape[0], pad, V7X_LANES), F32)

    @pl.when(pl.program_id(1) == 0)
    def _():
        wx_ref[...] = wx32_ref[...].astype(wx_ref.dtype)
        wgt_ref[...] = wgt32_ref[...].astype(wgt_ref.dtype)
        for part, ref in enumerate((wq32a_ref, wq32b_ref, wq32c_ref)):
            wq_ref[:, part * C:(part + 1) * C] = ref[...].astype(wq_ref.dtype)


    lam = lam_ref[...]
    log_sig = jnp.minimum(lam, 0.0) - jnp.log1p(jnp.exp(-jnp.abs(lam)))
    cw = cw_ref[...]
    cb = cb_ref[...]
    ba = ba_ref[...]
    bx = bx_ref[...]
    wg = wg_ref[0]
    n_groups = rows // V7X_SUBLANES
    sub3 = lax.broadcasted_iota(jnp.int32, (n_groups, V7X_SUBLANES, C), 1)
    n_sub = S // sub_rows

    def chunk(t0, h_prev):
        u = _causal_conv_rows(xpad_ref, cw, cb, t0, rows)
        pre = jnp.dot(u.astype(BF16), wg, preferred_element_type=F32)
        r = jax.nn.sigmoid(pre[:, :C] + ba)
        ig = jax.nn.sigmoid(pre[:, C:] + bx)
        log_a = RG_C * r * log_sig
        a = jnp.exp(log_a)
        th = jnp.tanh(log_a)
        m2 = -2.0 * th / (1.0 - th)
        mult = jnp.where(m2 > 0.0, m2 * lax.rsqrt(m2), 0.0)
        b = mult * (ig * u)
        a = a.reshape(n_groups, V7X_SUBLANES, C)
        b = b.reshape(n_groups, V7X_SUBLANES, C)
        for s in (1, 2, 4):
            keep = sub3 >= s
            a_sh = jnp.where(keep, pltpu.roll(a, s, axis=1), 1.0)
            b_sh = jnp.where(keep, pltpu.roll(b, s, axis=1), 0.0)
            b = a * b_sh + b
            a = a * a_sh
        a = a.reshape(rows, C)
        b = b.reshape(rows, C)
        h = h_prev
        for g in range(n_groups):
            lo = g * V7X_SUBLANES
            hg = a[lo:lo + V7X_SUBLANES, :] * h + b[lo:lo + V7X_SUBLANES, :]
            h = hg[V7X_SUBLANES - 1:V7X_SUBLANES, :]
            hs_ref[t0 + lo:t0 + lo + V7X_SUBLANES, :] = hg
        gate = gate_ref[t0:t0 + rows, :]
        o_ref[t0:t0 + rows, :] = _gelu_tanh_times(gate, 0.5 * hs_ref[t0:t0 + rows, :]).astype(o_ref.dtype)
        return h

    h = jnp.zeros((1, C), F32)
    for si in range(n_sub):
        r0 = si * sub_rows
        xs = x16_ref[r0:r0 + sub_rows, :]
        _store_slabs(xpad_ref, r0, jnp.dot(xs, wx_ref[...], preferred_element_type=F32))
        gate_ref[r0:r0 + sub_rows, :] = jnp.dot(xs, wgt_ref[...], preferred_element_type=F32)
        qkv_ref[r0:r0 + sub_rows, :] = jnp.dot(xs, wq_ref[...],
                                               preferred_element_type=F32).astype(qkv_ref.dtype)
        for ci in range(sub_rows // rows):
            h = chunk(r0 + ci * rows, h)


def _inproj_rglru(x16, w_in, layer, cw, cb, wg, ba, bx, lam, *, batch, seq, width, qkv_width, out_dtype):
    C = wg.shape[2]
    nct = width // C
    T, D = x16.shape
    nqc = qkv_width // nct
    q_parts = nqc // C
    q_off = 2 * width // C
    conv_k = cw.shape[1]
    assert qkv_width % nct == 0 and nqc == 3 * C
    vec = lambda: pl.BlockSpec((None, 1, C), lambda c, b: (layer, 0, c))
    wq_spec = lambda part: pl.BlockSpec((None, D, C), lambda c, b: (layer, 0, q_off + q_parts * c + part))
    est = 2 * (seq * D * 2 + 2 * D * C * 4 + D * nqc * 4 + seq * C * 2 + seq * nqc * 2 + C * 2 * C * 2) \
        + 2 * D * C * 2 + D * nqc * 2 + (3 * seq + 8) * C * 4 + PROJ_SUB_ROWS * nqc * 4
    return pl.pallas_call(
        functools.partial(_inproj_rglru_kernel, rows=RG_CHUNK_ROWS, sub_rows=PROJ_SUB_ROWS),
        grid=(nct, batch),
        in_specs=[
            pl.BlockSpec((seq, D), lambda c, b: (b, 0)),
            pl.BlockSpec((None, D, C), lambda c, b: (layer, 0, c)),
            pl.BlockSpec((None, D, C), lambda c, b: (layer, 0, nct + c)),
            wq_spec(0), wq_spec(1), wq_spec(2),
            pl.BlockSpec((None, conv_k, C), lambda c, b: (layer, 0, c)),
            vec(),
            pl.BlockSpec((None, 1, C, 2 * C), lambda c, b: (layer, c, 0, 0)),
            vec(), vec(), vec(),
        ],
        out_specs=[pl.BlockSpec((seq, C), lambda c, b: (b, c)),
                   pl.BlockSpec((seq, nqc), lambda c, b: (b, c))],
        out_shape=[jax.ShapeDtypeStruct((batch * seq, width), out_dtype),
                   jax.ShapeDtypeStruct((T, qkv_width), BF16)],
        scratch_shapes=[pltpu.VMEM((C // V7X_LANES, seq + V7X_SUBLANES, V7X_LANES), F32),
                        pltpu.VMEM((seq, C), F32), pltpu.VMEM((seq, C), F32),
                        pltpu.VMEM((D, C), BF16), pltpu.VMEM((D, C), BF16), pltpu.VMEM((D, nqc), BF16)],
        compiler_params=pltpu.CompilerParams(
            dimension_semantics=("parallel", "arbitrary"), vmem_limit_bytes=_vmem_limit(est)),
        name="inproj_rglru",
    )(x16, w_in, w_in, w_in, w_in, w_in, cw, cb, wg, ba, bx, lam)


POS_SPLIT = 256
ONES_ROWS = 16


def _attn_kernel(q_ref, k_ref, v_ref, lamv_ref, g_ref, o_ref, ka_ref, vt_ref, acc_ref, s0_ref, s1_ref,
                 *, tq, tk, n_heads, head_dim, scale, lam_init):
    h = pl.program_id(1)
    S, d2 = q_ref.shape
    V = v_ref.shape[1]
    n_kv = S // tk
    n_q = S // tq
    s_bufs = (s0_ref, s1_ref)

    def extra_cols(shape, c, lo_val, hi_val):
        lane = lax.broadcasted_iota(jnp.int32, shape, 1)
        base = head_dim * (1 - c)
        return jnp.where(lane == base, lo_val, jnp.where(lane == base + 1, hi_val, 0.0))

    def own_half(shape, c):
        lane = lax.broadcasted_iota(jnp.int32, shape, 1)
        return (lane < head_dim) if c == 0 else (lane >= head_dim)

    for jb in range(n_kv):
        rows = slice(jb * tk, (jb + 1) * tk)
        vt_ref[jb, 0:V, :] = v_ref[rows, :].astype(F32).T.astype(vt_ref.dtype)
        vt_ref[jb, V:V + ONES_ROWS, :] = jnp.ones((ONES_ROWS, tk), vt_ref.dtype)
        kb = k_ref[rows, :]
        pos = lax.broadcasted_iota(jnp.int32, (tk, d2), 0) + jb * tk
        j_lo = (pos % POS_SPLIT).astype(F32)
        j_hi = (pos // POS_SPLIT).astype(F32)
        for c in range(2):
            ka_ref[c, rows, :] = jnp.where(own_half((tk, d2), c), kb,
                                           extra_cols((tk, d2), c, j_lo, j_hi).astype(kb.dtype))

    slope = jnp.exp2(jnp.full((1, 1), -8.0 / n_heads, F32) * (h + 1).astype(F32))
    lv = lamv_ref[...]
    lam = (jnp.exp(jnp.sum(lv[0:1, :] * lv[1:2, :], axis=-1, keepdims=True))
           - jnp.exp(jnp.sum(lv[2:3, :] * lv[3:4, :], axis=-1, keepdims=True)) + lam_init)
    gain = g_ref[...] * (1.0 - lam_init)

    q_aug = {}

    def augmented_queries(qi):
        if qi not in q_aug:
            q = q_ref[qi * tq:(qi + 1) * tq, :] * jnp.asarray(scale, q_ref.dtype)
            q_aug[qi] = [jnp.where(own_half(q.shape, c), q,
                                   extra_cols(q.shape, c, slope, slope * POS_SPLIT).astype(q.dtype))
                         for c in range(2)]
        return q_aug[qi]

    def on_diagonal(qi, j):
        return tq == tk and j == qi and tk % 2 == 0

    def scores(qi, j, s_ref):
        qa = augmented_queries(qi)
        half_k = tk // 2
        half_q = tq // 2
        for c in range(2):
            keys = ka_ref[c, j * tk:(j + 1) * tk, :]
            if on_diagonal(qi, j):
                s_ref[0:half_k, c * tq:(c + 1) * tq] = lax.dot_general(
                    keys[0:half_k], qa[c], (((1,), (1,)), ((), ())), preferred_element_type=F32)
                s_ref[half_k:tk, c * tq:c * tq + half_q] = jnp.full((tk - half_k, half_q), MASK_VALUE, F32)
                s_ref[half_k:tk, c * tq + half_q:(c + 1) * tq] = lax.dot_general(
                    keys[half_k:tk], qa[c][half_q:tq], (((1,), (1,)), ((), ())), preferred_element_type=F32)
            else:
                s_ref[:, c * tq:(c + 1) * tq] = lax.dot_general(
                    keys, qa[c], (((1,), (1,)), ((), ())), preferred_element_type=F32)

    def weighted_values(qi, j, p):
        vt = vt_ref[j]
        if not on_diagonal(qi, j):
            return jnp.dot(vt, p, preferred_element_type=F32)
        half_k = tk // 2
        half_q = tq // 2
        parts = []
        for c in range(2):
            lo = slice(c * tq, c * tq + half_q)
            hi = slice(c * tq + half_q, (c + 1) * tq)
            parts.append(jnp.dot(vt[:, 0:half_k], p[0:half_k, lo], preferred_element_type=F32))
            parts.append(jnp.dot(vt, p[:, hi], preferred_element_type=F32))
        return jnp.concatenate(parts, axis=1)

    def softmax_pv(qi, j, s_ref, m_old):
        s = s_ref[...]
        if (j + 1) * tk - 1 > qi * tq:
            key_pos = lax.broadcasted_iota(jnp.int32, (tk, 2 * tq), 0) + j * tk
            col = lax.broadcasted_iota(jnp.int32, (tk, 2 * tq), 1)
            query_pos = jnp.where(col >= tq, col - tq, col) + qi * tq
            s = jnp.where(key_pos <= query_pos, s, MASK_VALUE)
        p_dtype = vt_ref.dtype
        if m_old is None:
            m_new = jnp.max(s, axis=0, keepdims=True)
            acc_ref[...] = weighted_values(qi, j, jnp.exp(s - m_new).astype(p_dtype))
        else:
            m_new = jnp.maximum(m_old, jnp.max(s, axis=0, keepdims=True))
            alpha = jnp.exp(m_old - m_new)
            acc_ref[...] = alpha * acc_ref[...] + weighted_values(qi, j, jnp.exp(s - m_new).astype(p_dtype))
        return m_new

    def finish(qi):
        ot = (acc_ref[0:V, 0:tq] / acc_ref[V:V + 1, 0:tq]
              - lam * (acc_ref[0:V, tq:2 * tq] / acc_ref[V:V + 1, tq:2 * tq]))
        ms = jnp.mean(ot * ot, axis=0, keepdims=True)
        o = (ot * lax.rsqrt(ms + RMS_EPS)).T * gain
        o_ref[qi * tq:(qi + 1) * tq, :] = o.astype(o_ref.dtype)

    pairs = [(qi, j) for qi in range(n_q) for j in range(((qi + 1) * tq + tk - 1) // tk)]
    scores(*pairs[0], s_bufs[0])
    m_run = None
    for t, (qi, j) in enumerate(pairs):
        if t + 1 < len(pairs):
            scores(*pairs[t + 1], s_bufs[(t + 1) % 2])
        m_run = softmax_pv(qi, j, s_bufs[t % 2], m_run)
        if t + 1 == len(pairs) or pairs[t + 1][0] != qi:
            finish(qi)
            m_run = None


def _diff_attn(qkv, lamv, g, layer, *, batch, seq, n_heads, head_dim, v_dim, lam_init, out_dtype, tq, tk):
    qk_w = 2 * head_dim
    assert qk_w == v_dim and seq % tq == 0 and seq % tk == 0 and seq % POS_SPLIT == 0
    assert seq // POS_SPLIT <= POS_SPLIT
    return pl.pallas_call(
        functools.partial(_attn_kernel, tq=tq, tk=tk, n_heads=n_heads, head_dim=head_dim,
                          scale=head_dim ** -0.5, lam_init=lam_init),
        grid=(batch, n_heads),
        in_specs=[
            pl.BlockSpec((seq, qk_w), lambda b, h: (b, h)),
            pl.BlockSpec((seq, qk_w), lambda b, h: (b, n_heads + h)),
            pl.BlockSpec((seq, v_dim), lambda b, h: (b, 2 * n_heads + h)),
            pl.BlockSpec((None,) + lamv.shape[1:], lambda b, h: (layer, 0, 0)),
            pl.BlockSpec((None, 1, v_dim), lambda b, h: (layer, 0, 0)),
        ],
        out_specs=pl.BlockSpec((seq, v_dim), lambda b, h: (b, h)),
        out_shape=jax.ShapeDtypeStruct((batch * seq, n_heads * v_dim), out_dtype),
        scratch_shapes=[
            pltpu.VMEM((2, seq, qk_w), BF16),
            pltpu.VMEM((seq // tk, v_dim + ONES_ROWS, tk), BF16),
            pltpu.VMEM((v_dim + ONES_ROWS, 2 * tq), F32),
            pltpu.VMEM((tk, 2 * tq), F32),
            pltpu.VMEM((tk, 2 * tq), F32),
        ],
        compiler_params=pltpu.CompilerParams(
            dimension_semantics=("parallel", "parallel"),
            vmem_limit_bytes=V7X_VMEM_BYTES - VMEM_UNSCOPED_BYTES),
        name="diff_attn",
    )(qkv, qkv, qkv, lamv, g)


def _residual_layernorm(acc_ref, xres_ref, g, b, o32_ref, o16_ref, *, alpha, r0, n_rows, rows):
    for c in range(n_rows // rows):
        lo = r0 + c * rows
        y = alpha * xres_ref[lo:lo + rows, :] + acc_ref[lo:lo + rows, :]
        mu = jnp.mean(y, axis=-1, keepdims=True)
        d = y - mu
        var = jnp.mean(d * d, axis=-1, keepdims=True)
        out = d * lax.rsqrt(var + LN_EPS) * g + b
        o32_ref[lo:lo + rows, :] = out
        o16_ref[lo:lo + rows, :] = out.astype(o16_ref.dtype)


def _outproj_kernel(rg_ref, at_ref, w32_ref, xres_ref, g_ref, b_ref, o32_ref, o16_ref, acc_ref, w_ref,
                    *, alpha, sub):
    tm, half = rg_ref.shape

    @pl.when(pl.program_id(0) == 0)
    def _():
        w_ref[...] = w32_ref[...].astype(w_ref.dtype)

    g = g_ref[...]
    b = b_ref[...]
    for t in range(tm // sub):
        r0 = t * sub
        acc_ref[r0:r0 + sub, :] = (
            jnp.dot(rg_ref[r0:r0 + sub, :], w_ref[0:half, :], preferred_element_type=F32)
            + jnp.dot(at_ref[r0:r0 + sub, :], w_ref[half:, :], preferred_element_type=F32))
        _residual_layernorm(acc_ref, xres_ref, g, b, o32_ref, o16_ref, alpha=alpha, r0=r0, n_rows=sub,
                            rows=LN_ROWS)


def _outproj_ln(rg_out, at_out, w, layer, xres, g, b, *, alpha, tm, sub):
    T, half = rg_out.shape
    _, Kw, D = w.shape
    est = 2 * (2 * tm * half * 2 + tm * D * 4 + tm * D * 4 + tm * D * 2) + Kw * D * (4 + 2) \
        + 2 * tm * D * 4
    row = lambda width: pl.BlockSpec((tm, width), lambda m: (m, 0))
    full = lambda a: pl.BlockSpec((None,) + a.shape[1:], lambda m: (layer, 0, 0))
    return pl.pallas_call(
        functools.partial(_outproj_kernel, alpha=alpha, sub=sub),
        grid=(T // tm,),
        in_specs=[row(half), row(half),
                  pl.BlockSpec((None, Kw, D), lambda m: (layer, 0, 0), pipeline_mode=pl.Buffered(1)),
                  row(D), full(g), full(b)],
        out_specs=[row(D), row(D)],
        out_shape=[jax.ShapeDtypeStruct((T, D), F32), jax.ShapeDtypeStruct((T, D), BF16)],
        scratch_shapes=[pltpu.VMEM((tm, D), F32), pltpu.VMEM((Kw, D), BF16)],
        compiler_params=pltpu.CompilerParams(
            dimension_semantics=("arbitrary",), vmem_limit_bytes=_vmem_limit(est)),
        name="outproj_ln",
    )(rg_out, at_out, w, xres, g, b)


def _ffn_up_kernel(x_ref, wg32_ref, wu32_ref, cwg_ref, cwu_ref, cbg_ref, cbu_ref, o_ref, pg_ref, pu_ref,
                   wg_ref, wu_ref, *, tiles_per_seq, rows):
    tm, tn = o_ref.shape
    pad = V7X_SUBLANES
    m = pl.program_id(1)

    @pl.when(m == 0)
    def _():
        wg_ref[...] = wg32_ref[...].astype(wg_ref.dtype)
        wu_ref[...] = wu32_ref[...].astype(wu_ref.dtype)

    @pl.when(m % tiles_per_seq == 0)
    def _():
        pg_ref[:, 0:pad, :] = jnp.zeros((pg_ref.shape[0], pad, V7X_LANES), F32)
        pu_ref[:, 0:pad, :] = jnp.zeros((pu_ref.shape[0], pad, V7X_LANES), F32)

    cwg = cwg_ref[...]
    cbg = cbg_ref[...]
    cwu = 0.5 * cwu_ref[...]
    cbu = 0.5 * cbu_ref[...]
    for c in range(tm // rows):
        r0 = c * rows
        xs = x_ref[r0:r0 + rows, :]
        _store_slabs(pg_ref, r0, jnp.dot(xs, wg_ref[...], preferred_element_type=F32))
        _store_slabs(pu_ref, r0, jnp.dot(xs, wu_ref[...], preferred_element_type=F32))
        gv = _causal_conv_rows(pg_ref, cwg, cbg, r0, rows)
        uv = _causal_conv_rows(pu_ref, cwu, cbu, r0, rows)
        o_ref[r0:r0 + rows, :] = _gelu_tanh_times(gv, uv).astype(o_ref.dtype)
    pg_ref[:, 0:pad, :] = pg_ref[:, tm:tm + pad, :]
    pu_ref[:, 0:pad, :] = pu_ref[:, tm:tm + pad, :]


def _ffn_up(x, w, layer, cw, cb, *, seq, d_ff, tm, tn, out_dtype):
    T, K = x.shape
    nt = d_ff // tn
    conv_k = cw.shape[1]
    assert seq % tm == 0 and d_ff % tn == 0
    est = 2 * (tm * K * 2 + 2 * K * tn * 4 + tm * tn * 2) + 2 * K * tn * 2 + 2 * (tm + 8) * tn * 4 \
        + 2 * tm * tn * 4
    vecg = lambda r: pl.BlockSpec((None, r, tn), lambda n, m: (layer, 0, n))
    vecu = lambda r: pl.BlockSpec((None, r, tn), lambda n, m: (layer, 0, n + nt))
    return pl.pallas_call(
        functools.partial(_ffn_up_kernel, tiles_per_seq=seq // tm, rows=min(PROJ_SUB_ROWS, tm)),
        grid=(nt, T // tm),
        in_specs=[
            pl.BlockSpec((tm, K), lambda n, m: (m, 0)),
            pl.BlockSpec((None, K, tn), lambda n, m: (layer, 0, n)),
            pl.BlockSpec((None, K, tn), lambda n, m: (layer, 0, n + nt)),
            vecg(conv_k), vecu(conv_k), vecg(1), vecu(1),
        ],
        out_specs=pl.BlockSpec((tm, tn), lambda n, m: (m, n)),
        out_shape=jax.ShapeDtypeStruct((T, d_ff), out_dtype),
        scratch_shapes=[pltpu.VMEM((tn // V7X_LANES, tm + V7X_SUBLANES, V7X_LANES), F32),
                        pltpu.VMEM((tn // V7X_LANES, tm + V7X_SUBLANES, V7X_LANES), F32),
                        pltpu.VMEM((K, tn), BF16), pltpu.VMEM((K, tn), BF16)],
        compiler_params=pltpu.CompilerParams(
            dimension_semantics=("parallel", "arbitrary"), vmem_limit_bytes=_vmem_limit(est)),
        name="ffn_up",
    )(x, w, w, cw, cw, cb, cb)


def _ffn_down_kernel(h_ref, w32_ref, xres_ref, g_ref, b_ref, o32_ref, o16_ref, acc_ref, w_ref, *, alpha, n_pre):
    s = pl.program_id(0)
    kc = w32_ref.shape[0]

    @pl.when(s < n_pre)
    def _():
        w_ref[pl.ds(pl.multiple_of(s * kc, kc), kc), :] = w32_ref[...].astype(w_ref.dtype)

    @pl.when(s >= n_pre)
    def _():
        tm = h_ref.shape[0]
        acc_ref[...] = jnp.dot(h_ref[...], w_ref[...], preferred_element_type=F32)
        _residual_layernorm(acc_ref, xres_ref, g_ref[...], b_ref[...], o32_ref, o16_ref, alpha=alpha, r0=0,
                            n_rows=tm, rows=LN_ROWS)


def _ffn_down_ln(h, w, layer, xres, g, b, *, alpha, tm, kc):
    T, K = h.shape
    D = w.shape[2]
    assert T % tm == 0 and K % kc == 0
    n_pre = K // kc
    est = 2 * (tm * K * 2 + kc * D * 4 + tm * D * 4 + tm * D * 4 + tm * D * 2) + K * D * 2 + 2 * tm * D * 4
    tile = lambda s: jnp.maximum(s - n_pre, 0)
    row = lambda width: pl.BlockSpec((tm, width), lambda s: (tile(s), 0))
    full = lambda a: pl.BlockSpec((None,) + a.shape[1:], lambda s: (layer, 0, 0))
    return pl.pallas_call(
        functools.partial(_ffn_down_kernel, alpha=alpha, n_pre=n_pre),
        grid=(n_pre + T // tm,),
        in_specs=[row(K),
                  pl.BlockSpec((None, kc, D), lambda s: (layer, jnp.minimum(s, n_pre - 1), 0)),
                  row(D), full(g), full(b)],
        out_specs=[row(D), row(D)],
        out_shape=[jax.ShapeDtypeStruct((T, D), F32), jax.ShapeDtypeStruct((T, D), BF16)],
        scratch_shapes=[pltpu.VMEM((tm, D), F32), pltpu.VMEM((K, D), BF16)],
        compiler_params=pltpu.CompilerParams(
            dimension_semantics=("arbitrary",), vmem_limit_bytes=_vmem_limit(est)),
        name="ffn_down_ln",
    )(h, w, xres, g, b)


def _block_diag(w, tile):
    G, d, _ = w.shape
    per = tile // d
    w4 = w.reshape(G // per, per, d, d)
    eye = jnp.eye(per, dtype=w.dtype)
    return jnp.einsum('jipq,ik->jipkq', w4, eye).reshape(G // per, tile, tile)


def kernel(x, w_in, rg_conv_w, rg_conv_b, rg_gate_a_w, rg_gate_a_b, rg_gate_x_w, rg_gate_x_b, rg_lambda,
           lam_q1, lam_k1, lam_q2, lam_k2, subln_g, w_out, ln_mix_g, ln_mix_b, w_up, ffn_conv_w,
           ffn_conv_b, w_down, ln_ffn_g, ln_ffn_b):
    B, S, D = x.shape
    depth = w_in.shape[0]
    rg_w = rg_conv_w.shape[2]
    head_dim = lam_q1.shape[1]
    v_dim = subln_g.shape[1]
    mix_w = w_out.shape[1]
    attn_w = mix_w - rg_w
    n_heads = attn_w // v_dim
    qkv_w = w_in.shape[2] - 2 * rg_w
    d_ff = w_down.shape[1]
    T = B * S
    alpha = (2.0 * depth) ** 0.25

    vec3 = lambda a: a.reshape(depth, 1, a.shape[-1])
    n_blocks, blk = rg_gate_a_w.shape[1], rg_gate_a_w.shape[2]
    gate_tiles = lambda w: _block_diag(w.reshape(depth * n_blocks, blk, blk), V7X_MXU_DIM)
    wg = jnp.concatenate([gate_tiles(rg_gate_a_w), gate_tiles(rg_gate_x_w)], axis=-1).astype(BF16)
    wg = wg.reshape(depth, rg_w // V7X_MXU_DIM, V7X_MXU_DIM, 2 * V7X_MXU_DIM)
    lamv = jnp.stack([lam_q1, lam_k1, lam_q2, lam_k2], axis=1)
    rg_cb, rg_ba, rg_bx, rg_lam = vec3(rg_conv_b), vec3(rg_gate_a_b), vec3(rg_gate_x_b), vec3(rg_lambda)
    sub_g, mix_g, mix_b = vec3(subln_g), vec3(ln_mix_g), vec3(ln_mix_b)
    ffn_cb, ffn_g, ffn_b = vec3(ffn_conv_b), vec3(ln_ffn_g), vec3(ln_ffn_b)

    x32 = x.reshape(T, D)
    x16 = x32.astype(BF16)
    for i in range(depth):
        lam_init = 0.8 - 0.6 * math.exp(-0.3 * i)
        rg_out, qkv = _inproj_rglru(x16, w_in, i, rg_conv_w, rg_cb, wg, rg_ba, rg_bx, rg_lam,
                                    batch=B, seq=S, width=rg_w, qkv_width=qkv_w, out_dtype=BF16)
        at_out = _diff_attn(qkv, lamv, sub_g, i, batch=B, seq=S, n_heads=n_heads,
                            head_dim=head_dim, v_dim=v_dim, lam_init=lam_init, out_dtype=BF16,
                            tq=min(512, S), tk=min(512, S))
        x32, x16 = _outproj_ln(rg_out, at_out, w_out, i, x32, mix_g, mix_b, alpha=alpha, tm=512, sub=256)
        hmid = _ffn_up(x16, w_up, i, ffn_conv_w, ffn_cb, seq=S, d_ff=d_ff, tm=S, tn=512, out_dtype=BF16)
        x32, x16 = _ffn_down_ln(hmid, w_down, i, x32, ffn_g, ffn_b, alpha=alpha, tm=256, kc=512)
    return x32.reshape(B, S, D)
```

```python
import functools
import math

import jax
import jax.numpy as jnp
from jax import lax
from jax.experimental import pallas as pl
from jax.experimental.pallas import tpu as pltpu

F32 = jnp.float32
BF16 = jnp.bfloat16

V7X_LANES = 128
V7X_SUBLANES = 8
V7X_MXU_DIM = 256
V7X_VMEM_BYTES = 64 * 1024 * 1024
VMEM_TEMP_BYTES = 8 * 1024 * 1024
VMEM_UNSCOPED_BYTES = 4 * 1024 * 1024
VMEM_MIN_REQUEST_BYTES = 16 * 1024 * 1024

RG_CHUNK_ROWS = 128
PROJ_SUB_ROWS = 256
LN_ROWS = 64

RG_C = 8.0
LN_EPS = 1e-5
RMS_EPS = 1e-5
MASK_VALUE = -1e30


def _vmem_limit(nbytes):
    return int(min(V7X_VMEM_BYTES - VMEM_UNSCOPED_BYTES, max(nbytes + VMEM_TEMP_BYTES, VMEM_MIN_REQUEST_BYTES)))


def _gelu_tanh_times(x, half_y):
    c = math.sqrt(2.0 / math.pi)
    t = jnp.tanh(x * (c + (c * 0.044715) * (x * x)))
    return (x + x * t) * half_y


def _causal_conv_rows(pad_ref, cw, cb, r0, rows):
    conv_k = cw.shape[0]
    pad = V7X_SUBLANES
    strips = []
    for s in range(pad_ref.shape[0]):
        lanes = slice(s * V7X_LANES, (s + 1) * V7X_LANES)
        out = cb[:, lanes] + cw[conv_k - 1:conv_k, lanes] * pad_ref[s, pl.ds(r0 + pad, rows), :]
        for k in range(conv_k - 1):
            start = r0 + pad - (conv_k - 1) + k
            out = out + cw[k:k + 1, lanes] * pad_ref[s, pl.ds(start, rows, stride=1), :]
        strips.append(out)
    return strips[0] if len(strips) == 1 else jnp.concatenate(strips, axis=1)


def _store_slabs(pad_ref, r0, value):
    rows = value.shape[0]
    for s in range(pad_ref.shape[0]):
        pad_ref[s, V7X_SUBLANES + r0:V7X_SUBLANES + r0 + rows, :] = value[:, s * V7X_LANES:(s + 1) * V7X_LANES]


def _inproj_rglru_kernel(x16_ref, wx32_ref, wgt32_ref, wq32a_ref, wq32b_ref, wq32c_ref, cw_ref, cb_ref, wg_ref,
                         ba_ref, bx_ref, lam_ref, o_ref, qkv_ref, xpad_ref, gate_ref, hs_ref,
                         wx_ref, wgt_ref, wq_ref, *, rows, sub_rows):
    S, C = o_ref.shape
    pad = V7X_SUBLANES
    xpad_ref[:, 0:pad, :] = jnp.zeros((xpad_ref.shape[0], pad, V7X_LANES), F32)
    wx_ref[...] = wx32_ref[...].astype(wx_ref.dtype)
    wgt_ref[...] = wgt32_ref[...].astype(wgt_ref.dtype)
    for part, ref in enumerate((wq32a_ref, wq32b_ref, wq32c_ref)):
        wq_ref[:, part * C:(part + 1) * C] = ref[...].astype(wq_ref.dtype)

    lam = lam_ref[...]
    log_sig = jnp.minimum(lam, 0.0) - jnp.log1p(jnp.exp(-jnp.abs(lam)))
    cw = cw_ref[...]
    cb = cb_ref[...]
    ba = ba_ref[...]
    bx = bx_ref[...]
    wg = wg_ref[0]
    n_groups = rows // V7X_SUBLANES
    sub3 = lax.broadcasted_iota(jnp.int32, (n_groups, V7X_SUBLANES, C), 1)
    n_sub = S // sub_rows

    def chunk(t0, h_prev):
        u = _causal_conv_rows(xpad_ref, cw, cb, t0, rows)
        pre = jnp.dot(u.astype(BF16), wg, preferred_element_type=F32)
        r = jax.nn.sigmoid(pre[:, :C] + ba)
        ig = jax.nn.sigmoid(pre[:, C:] + bx)
        log_a = RG_C * r * log_sig
        a = jnp.exp(log_a)
        th = jnp.tanh(log_a)
        m2 = -2.0 * th / (1.0 - th)
        mult = jnp.where(m2 > 0.0, m2 * lax.rsqrt(m2), 0.0)
        b = mult * (ig * u)
        a = a.reshape(n_groups, V7X_SUBLANES, C)
        b = b.reshape(n_groups, V7X_SUBLANES, C)
        for s in (1, 2, 4):
            keep = sub3 >= s
            a_sh = jnp.where(keep, pltpu.roll(a, s, axis=1), 1.0)
            b_sh = jnp.where(keep, pltpu.roll(b, s, axis=1), 0.0)
            b = a * b_sh + b
            a = a * a_sh
        a = a.reshape(rows, C)
        b = b.reshape(rows, C)
        h = h_prev
        for g in range(n_groups):
            lo = g * V7X_SUBLANES
            hg = a[lo:lo + V7X_SUBLANES, :] * h + b[lo:lo + V7X_SUBLANES, :]
            h = hg[V7X_SUBLANES - 1:V7X_SUBLANES, :]
            hs_ref[t0 + lo:t0 + lo + V7X_SUBLANES, :] = hg
        gate = gate_ref[t0:t0 + rows, :]
        o_ref[t0:t0 + rows, :] = _gelu_tanh_times(gate, 0.5 * hs_ref[t0:t0 + rows, :]).astype(o_ref.dtype)
        return h

    h = jnp.zeros((1, C), F32)
    for si in range(n_sub):
        r0 = si * sub_rows
        xs = x16_ref[r0:r0 + sub_rows, :]
        _store_slabs(xpad_ref, r0, jnp.dot(xs, wx_ref[...], preferred_element_type=F32))
        gate_ref[r0:r0 + sub_rows, :] = jnp.dot(xs, wgt_ref[...], preferred_element_type=F32)
        qkv_ref[r0:r0 + sub_rows, :] = jnp.dot(xs, wq_ref[...],
                                               preferred_element_type=F32).astype(qkv_ref.dtype)
        for ci in range(sub_rows // rows):
            h = chunk(r0 + ci * rows, h)


def _inproj_rglru(x16, w_in, layer, cw, cb, wg, ba, bx, lam, *, batch, seq, width, qkv_width, out_dtype):
    C = wg.shape[2]
    nct = width // C
    T, D = x16.shape
    nqc = qkv_width // nct
    q_parts = nqc // C
    q_off = 2 * width // C
    conv_k = cw.shape[1]
    assert qkv_width % nct == 0 and nqc == 3 * C
    vec = lambda: pl.BlockSpec((None, 1, C), lambda b, c: (layer, 0, c))
    wq_spec = lambda part: pl.BlockSpec((None, D, C), lambda b, c: (layer, 0, q_off + q_parts * c + part))
    est = 2 * (seq * D * 2 + 2 * D * C * 4 + D * nqc * 4 + seq * C * 2 + seq * nqc * 2 + C * 2 * C * 2) \
        + 2 * D * C * 2 + D * nqc * 2 + (3 * seq + 8) * C * 4 + PROJ_SUB_ROWS * nqc * 4
    return pl.pallas_call(
        functools.partial(_inproj_rglru_kernel, rows=RG_CHUNK_ROWS, sub_rows=PROJ_SUB_ROWS),
        grid=(batch, nct),
        in_specs=[
            pl.BlockSpec((seq, D), lambda b, c: (b, 0)),
            pl.BlockSpec((None, D, C), lambda b, c: (layer, 0, c)),
            pl.BlockSpec((None, D, C), lambda b, c: (layer, 0, nct + c)),
            wq_spec(0), wq_spec(1), wq_spec(2),
            pl.BlockSpec((None, conv_k, C), lambda b, c: (layer, 0, c)),
            vec(),
            pl.BlockSpec((None, 1, C, 2 * C), lambda b, c: (layer, c, 0, 0)),
            vec(), vec(), vec(),
        ],
        out_specs=[pl.BlockSpec((seq, C), lambda b, c: (b, c)),
                   pl.BlockSpec((seq, nqc), lambda b, c: (b, c))],
        out_shape=[jax.ShapeDtypeStruct((batch * seq, width), out_dtype),
                   jax.ShapeDtypeStruct((T, qkv_width), BF16)],
        scratch_shapes=[pltpu.VMEM((C // V7X_LANES, seq + V7X_SUBLANES, V7X_LANES), F32),
                        pltpu.VMEM((seq, C), F32), pltpu.VMEM((seq, C), F32),
                        pltpu.VMEM((D, C), BF16), pltpu.VMEM((D, C), BF16), pltpu.VMEM((D, nqc), BF16)],
        compiler_params=pltpu.CompilerParams(
            dimension_semantics=("parallel", "parallel"), vmem_limit_bytes=_vmem_limit(est)),
        name="inproj_rglru",
    )(x16, w_in, w_in, w_in, w_in, w_in, cw, cb, wg, ba, bx, lam)


POS_SPLIT = 256
ONES_ROWS = 16


def _attn_kernel(q_ref, k_ref, v_ref, lamv_ref, g_ref, o_ref, ka_ref, vt_ref, acc_ref, s0_ref, s1_ref,
                 *, tq, tk, n_heads, head_dim, scale, lam_init):
    h = pl.program_id(1)
    S, d2 = q_ref.shape
    V = v_ref.shape[1]
    n_kv = S // tk
    n_q = S // tq
    s_bufs = (s0_ref, s1_ref)

    def extra_cols(shape, c, lo_val, hi_val):
        lane = lax.broadcasted_iota(jnp.int32, shape, 1)
        base = head_dim * (1 - c)
        return jnp.where(lane == base, lo_val, jnp.where(lane == base + 1, hi_val, 0.0))

    def own_half(shape, c):
        lane = lax.broadcasted_iota(jnp.int32, shape, 1)
        return (lane < head_dim) if c == 0 else (lane >= head_dim)

    for jb in range(n_kv):
        rows = slice(jb * tk, (jb + 1) * tk)
        vt_ref[jb, 0:V, :] = v_ref[rows, :].astype(F32).T.astype(vt_ref.dtype)
        vt_ref[jb, V:V + ONES_ROWS, :] = jnp.ones((ONES_ROWS, tk), vt_ref.dtype)
        kb = k_ref[rows, :]
        pos = lax.broadcasted_iota(jnp.int32, (tk, d2), 0) + jb * tk
        j_lo = (pos % POS_SPLIT).astype(F32)
        j_hi = (pos // POS_SPLIT).astype(F32)
        for c in range(2):
            ka_ref[c, rows, :] = jnp.where(own_half((tk, d2), c), kb,
                                           extra_cols((tk, d2), c, j_lo, j_hi).astype(kb.dtype))

    slope = jnp.exp2(jnp.full((1, 1), -8.0 / n_heads, F32) * (h + 1).astype(F32))
    lv = lamv_ref[...]
    lam = (jnp.exp(jnp.sum(lv[0:1, :] * lv[1:2, :], axis=-1, keepdims=True))
           - jnp.exp(jnp.sum(lv[2:3, :] * lv[3:4, :], axis=-1, keepdims=True)) + lam_init)
    gain = g_ref[...] * (1.0 - lam_init)

    q_aug = {}

    def augmented_queries(qi):
        if qi not in q_aug:
            q = q_ref[qi * tq:(qi + 1) * tq, :] * jnp.asarray(scale, q_ref.dtype)
            q_aug[qi] = [jnp.where(own_half(q.shape, c), q,
                                   extra_cols(q.shape, c, slope, slope * POS_SPLIT).astype(q.dtype))
                         for c in range(2)]
        return q_aug[qi]

    def on_diagonal(qi, j):
        return tq == tk and j == qi and tk % 2 == 0

    def scores(qi, j, s_ref):
        qa = augmented_queries(qi)
        half_k = tk // 2
        half_q = tq // 2
        for c in range(2):
            keys = ka_ref[c, j * tk:(j + 1) * tk, :]
            if on_diagonal(qi, j):
                s_ref[0:half_k, c * tq:(c + 1) * tq] = lax.dot_general(
                    keys[0:half_k], qa[c], (((1,), (1,)), ((), ())), preferred_element_type=F32)
                s_ref[half_k:tk, c * tq:c * tq + half_q] = jnp.full((tk - half_k, half_q), MASK_VALUE, F32)
                s_ref[half_k:tk, c * tq + half_q:(c + 1) * tq] = lax.dot_general(
                    keys[half_k:tk], qa[c][half_q:tq], (((1,), (1,)), ((), ())), preferred_element_type=F32)
            else:
                s_ref[:, c * tq:(c + 1) * tq] = lax.dot_general(
                    keys, qa[c], (((1,), (1,)), ((), ())), preferred_element_type=F32)

    def weighted_values(qi, j, p):
        vt = vt_ref[j]
        if not on_diagonal(qi, j):
            return jnp.dot(vt, p, preferred_element_type=F32)
        half_k = tk // 2
        half_q = tq // 2
        parts = []
        for c in range(2):
            lo = slice(c * tq, c * tq + half_q)
            hi = slice(c * tq + half_q, (c + 1) * tq)
            parts.append(jnp.dot(vt[:, 0:half_k], p[0:half_k, lo], preferred_element_type=F32))
            parts.append(jnp.dot(vt, p[:, hi], preferred_element_type=F32))
        return jnp.concatenate(parts, axis=1)

    def softmax_pv(qi, j, s_ref, m_old):
        s = s_ref[...]
        if (j + 1) * tk - 1 > qi * tq:
            key_pos = lax.broadcasted_iota(jnp.int32, (tk, 2 * tq), 0) + j * tk
            col = lax.broadcasted_iota(jnp.int32, (tk, 2 * tq), 1)
            query_pos = jnp.where(col >= tq, col - tq, col) + qi * tq
            s = jnp.where(key_pos <= query_pos, s, MASK_VALUE)
        p_dtype = vt_ref.dtype
        if m_old is None:
            m_new = jnp.max(s, axis=0, keepdims=True)
            acc_ref[...] = weighted_values(qi, j, jnp.exp(s - m_new).astype(p_dtype))
        else:
            m_new = jnp.maximum(m_old, jnp.max(s, axis=0, keepdims=True))
            alpha = jnp.exp(m_old - m_new)
            acc_ref[...] = alpha * acc_ref[...] + weighted_values(qi, j, jnp.exp(s - m_new).astype(p_dtype))
        return m_new

    def finish(qi):
        ot = (acc_ref[0:V, 0:tq] / acc_ref[V:V + 1, 0:tq]
              - lam * (acc_ref[0:V, tq:2 * tq] / acc_ref[V:V + 1, tq:2 * tq]))
        ms = jnp.mean(ot * ot, axis=0, keepdims=True)
        o = (ot * lax.rsqrt(ms + RMS_EPS)).T * gain
        o_ref[qi * tq:(qi + 1) * tq, :] = o.astype(o_ref.dtype)

    pairs = [(qi, j) for qi in range(n_q) for j in range(((qi + 1) * tq + tk - 1) // tk)]
    scores(*pairs[0], s_bufs[0])
    m_run = None
    for t, (qi, j) in enumerate(pairs):
        if t + 1 < len(pairs):
            scores(*pairs[t + 1], s_bufs[(t + 1) % 2])
        m_run = softmax_pv(qi, j, s_bufs[t % 2], m_run)
        if t + 1 == len(pairs) or pairs[t + 1][0] != qi:
            finish(qi)
            m_run = None


def _diff_attn(qkv, lamv, g, layer, *, batch, seq, n_heads, head_dim, v_dim, lam_init, out_dtype, tq, tk):
    qk_w = 2 * head_dim
    assert qk_w == v_dim and seq % tq == 0 and seq % tk == 0 and seq % POS_SPLIT == 0
    assert seq // POS_SPLIT <= POS_SPLIT
    return pl.pallas_call(
        functools.partial(_attn_kernel, tq=tq, tk=tk, n_heads=n_heads, head_dim=head_dim,
                          scale=head_dim ** -0.5, lam_init=lam_init),
        grid=(batch, n_heads),
        in_specs=[
            pl.BlockSpec((seq, qk_w), lambda b, h: (b, h)),
            pl.BlockSpec((seq, qk_w), lambda b, h: (b, n_heads + h)),
            pl.BlockSpec((seq, v_dim), lambda b, h: (b, 2 * n_heads + h)),
            pl.BlockSpec((None,) + lamv.shape[1:], lambda b, h: (layer, 0, 0)),
            pl.BlockSpec((None, 1, v_dim), lambda b, h: (layer, 0, 0)),
        ],
        out_specs=pl.BlockSpec((seq, v_dim), lambda b, h: (b, h)),
        out_shape=jax.ShapeDtypeStruct((batch * seq, n_heads * v_dim), out_dtype),
        scratch_shapes=[
            pltpu.VMEM((2, seq, qk_w), BF16),
            pltpu.VMEM((seq // tk, v_dim + ONES_ROWS, tk), BF16),
            pltpu.VMEM((v_dim + ONES_ROWS, 2 * tq), F32),
            pltpu.VMEM((tk, 2 * tq), F32),
            pltpu.VMEM((tk, 2 * tq), F32),
        ],
        compiler_params=pltpu.CompilerParams(
            dimension_semantics=("parallel", "parallel"),
            vmem_limit_bytes=V7X_VMEM_BYTES - VMEM_UNSCOPED_BYTES),
        name="diff_attn",
    )(qkv, qkv, qkv, lamv, g)


def _residual_layernorm(acc_ref, xres_ref, g, b, o32_ref, o16_ref, *, alpha, r0, n_rows, rows):
    for c in range(n_rows // rows):
        lo = r0 + c * rows
        y = alpha * xres_ref[lo:lo + rows, :] + acc_ref[lo:lo + rows, :]
        mu = jnp.mean(y, axis=-1, keepdims=True)
        d = y - mu
        var = jnp.mean(d * d, axis=-1, keepdims=True)
        out = d * lax.rsqrt(var + LN_EPS) * g + b
        o32_ref[lo:lo + rows, :] = out
        o16_ref[lo:lo + rows, :] = out.astype(o16_ref.dtype)


def _outproj_kernel(rg_ref, at_ref, w32_ref, xres_ref, g_ref, b_ref, o32_ref, o16_ref, acc_ref, w_ref,
                    *, alpha, sub):
    tm, half = rg_ref.shape

    @pl.when(pl.program_id(0) == 0)
    def _():
        w_ref[...] = w32_ref[...].astype(w_ref.dtype)

    g = g_ref[...]
    b = b_ref[...]
    for t in range(tm // sub):
        r0 = t * sub
        acc_ref[r0:r0 + sub, :] = (
            jnp.dot(rg_ref[r0:r0 + sub, :], w_ref[0:half, :], preferred_element_type=F32)
            + jnp.dot(at_ref[r0:r0 + sub, :], w_ref[half:, :], preferred_element_type=F32))
        _residual_layernorm(acc_ref, xres_ref, g, b, o32_ref, o16_ref, alpha=alpha, r0=r0, n_rows=sub,
                            rows=LN_ROWS)


def _outproj_ln(rg_out, at_out, w, layer, xres, g, b, *, alpha, tm, sub):
    T, half = rg_out.shape
    _, Kw, D = w.shape
    est = 2 * (2 * tm * half * 2 + tm * D * 4 + tm * D * 4 + tm * D * 2) + Kw * D * (4 + 2) \
        + 2 * tm * D * 4
    row = lambda width: pl.BlockSpec((tm, width), lambda m: (m, 0))
    full = lambda a: pl.BlockSpec((None,) + a.shape[1:], lambda m: (layer, 0, 0))
    return pl.pallas_call(
        functools.partial(_outproj_kernel, alpha=alpha, sub=sub),
        grid=(T // tm,),
        in_specs=[row(half), row(half),
                  pl.BlockSpec((None, Kw, D), lambda m: (layer, 0, 0), pipeline_mode=pl.Buffered(1)),
                  row(D), full(g), full(b)],
        out_specs=[row(D), row(D)],
        out_shape=[jax.ShapeDtypeStruct((T, D), F32), jax.ShapeDtypeStruct((T, D), BF16)],
        scratch_shapes=[pltpu.VMEM((tm, D), F32), pltpu.VMEM((Kw, D), BF16)],
        compiler_params=pltpu.CompilerParams(
            dimension_semantics=("arbitrary",), vmem_limit_bytes=_vmem_limit(est)),
        name="outproj_ln",
    )(rg_out, at_out, w, xres, g, b)


def _ffn_up_kernel(x_ref, wg32_ref, wu32_ref, cwg_ref, cwu_ref, cbg_ref, cbu_ref, o_ref, pg_ref, pu_ref,
                   wg_ref, wu_ref, *, tiles_per_seq, rows):
    tm, tn = o_ref.shape
    pad = V7X_SUBLANES
    m = pl.program_id(1)

    @pl.when(m == 0)
    def _():
        wg_ref[...] = wg32_ref[...].astype(wg_ref.dtype)
        wu_ref[...] = wu32_ref[...].astype(wu_ref.dtype)

    @pl.when(m % tiles_per_seq == 0)
    def _():
        pg_ref[:, 0:pad, :] = jnp.zeros((pg_ref.shape[0], pad, V7X_LANES), F32)
        pu_ref[:, 0:pad, :] = jnp.zeros((pu_ref.shape[0], pad, V7X_LANES), F32)

    cwg = cwg_ref[...]
    cbg = cbg_ref[...]
    cwu = 0.5 * cwu_ref[...]
    cbu = 0.5 * cbu_ref[...]
    for c in range(tm // rows):
        r0 = c * rows
        xs = x_ref[r0:r0 + rows, :]
        _store_slabs(pg_ref, r0, jnp.dot(xs, wg_ref[...], preferred_element_type=F32))
        _store_slabs(pu_ref, r0, jnp.dot(xs, wu_ref[...], preferred_element_type=F32))
        gv = _causal_conv_rows(pg_ref, cwg, cbg, r0, rows)
        uv = _causal_conv_rows(pu_ref, cwu, cbu, r0, rows)
        o_ref[r0:r0 + rows, :] = _gelu_tanh_times(gv, uv).astype(o_ref.dtype)
    pg_ref[:, 0:pad, :] = pg_ref[:, tm:tm + pad, :]
    pu_ref[:, 0:pad, :] = pu_ref[:, tm:tm + pad, :]


def _ffn_up(x, w, layer, cw, cb, *, seq, d_ff, tm, tn, out_dtype):
    T, K = x.shape
    nt = d_ff // tn
    conv_k = cw.shape[1]
    assert seq % tm == 0 and d_ff % tn == 0
    est = 2 * (tm * K * 2 + 2 * K * tn * 4 + tm * tn * 2) + 2 * K * tn * 2 + 2 * (tm + 8) * tn * 4 \
        + 2 * tm * tn * 4
    vecg = lambda r: pl.BlockSpec((None, r, tn), lambda n, m: (layer, 0, n))
    vecu = lambda r: pl.BlockSpec((None, r, tn), lambda n, m: (layer, 0, n + nt))
    return pl.pallas_call(
        functools.partial(_ffn_up_kernel, tiles_per_seq=seq // tm, rows=min(PROJ_SUB_ROWS, tm)),
        grid=(nt, T // tm),
        in_specs=[
            pl.BlockSpec((tm, K), lambda n, m: (m, 0)),
            pl.BlockSpec((None, K, tn), lambda n, m: (layer, 0, n)),
            pl.BlockSpec((None, K, tn), lambda n, m: (layer, 0, n + nt)),
            vecg(conv_k), vecu(conv_k), vecg(1), vecu(1),
        ],
        out_specs=pl.BlockSpec((tm, tn), lambda n, m: (m, n)),
        out_shape=jax.ShapeDtypeStruct((T, d_ff), out_dtype),
        scratch_shapes=[pltpu.VMEM((tn // V7X_LANES, tm + V7X_SUBLANES, V7X_LANES), F32),
                        pltpu.VMEM((tn // V7X_LANES, tm + V7X_SUBLANES, V7X_LANES), F32),
                        pltpu.VMEM((K, tn), BF16), pltpu.VMEM((K, tn), BF16)],
        compiler_params=pltpu.CompilerParams(
            dimension_semantics=("parallel", "arbitrary"), vmem_limit_bytes=_vmem_limit(est)),
        name="ffn_up",
    )(x, w, w, cw, cw, cb, cb)


def _ffn_down_kernel(h_ref, w32_ref, xres_ref, g_ref, b_ref, o32_ref, o16_ref, acca_ref, accb_ref, w_ref,
                     *, alpha, n_pre, n_tiles):
    s = pl.program_id(0)
    t = s - n_pre
    kc = w32_ref.shape[0]
    tm = h_ref.shape[0]

    @pl.when(s < n_pre)
    def _():
        w_ref[pl.ds(pl.multiple_of(s * kc, kc), kc), :] = w32_ref[...].astype(w_ref.dtype)

    def project(acc_ref):
        acc_ref[...] = jnp.dot(h_ref[...], w_ref[...], preferred_element_type=F32)

    def normalise(acc_ref):
        _residual_layernorm(acc_ref, xres_ref, g_ref[...], b_ref[...], o32_ref, o16_ref, alpha=alpha, r0=0,
                            n_rows=tm, rows=LN_ROWS)

    @pl.when(t == 0)
    def _():
        project(acca_ref)

    @pl.when(jnp.logical_and(jnp.logical_and(t > 0, t < n_tiles), t % 2 == 1))
    def _():
        project(accb_ref)
        normalise(acca_ref)

    @pl.when(jnp.logical_and(jnp.logical_and(t > 0, t < n_tiles), t % 2 == 0))
    def _():
        project(acca_ref)
        normalise(accb_ref)

    @pl.when(t == n_tiles)
    def _():
        normalise(accb_ref if (n_tiles - 1) % 2 == 1 else acca_ref)


def _ffn_down_ln(h, w, layer, xres, g, b, *, alpha, tm, kc):
    T, K = h.shape
    D = w.shape[2]
    assert T % tm == 0 and K % kc == 0
    n_pre = K // kc
    n_tiles = T // tm
    est = 2 * (tm * K * 2 + kc * D * 4 + tm * D * 4 + tm * D * 4 + tm * D * 2) + K * D * 2 + 3 * tm * D * 4
    cur = lambda s: jnp.clip(s - n_pre, 0, n_tiles - 1)
    prev = lambda s: jnp.clip(s - n_pre - 1, 0, n_tiles - 1)
    row = lambda width: pl.BlockSpec((tm, width), lambda s: (prev(s), 0))
    full = lambda a: pl.BlockSpec((None,) + a.shape[1:], lambda s: (layer, 0, 0))
    return pl.pallas_call(
        functools.partial(_ffn_down_kernel, alpha=alpha, n_pre=n_pre, n_tiles=n_tiles),
        grid=(n_pre + n_tiles + 1,),
        in_specs=[pl.BlockSpec((tm, K), lambda s: (cur(s), 0)),
                  pl.BlockSpec((None, kc, D), lambda s: (layer, jnp.minimum(s, n_pre - 1), 0)),
                  row(D), full(g), full(b)],
        out_specs=[row(D), row(D)],
        out_shape=[jax.ShapeDtypeStruct((T, D), F32), jax.ShapeDtypeStruct((T, D), BF16)],
        scratch_shapes=[pltpu.VMEM((tm, D), F32), pltpu.VMEM((tm, D), F32), pltpu.VMEM((K, D), BF16)],
        compiler_params=pltpu.CompilerParams(
            dimension_semantics=("arbitrary",), vmem_limit_bytes=_vmem_limit(est)),
        name="ffn_down_ln",
    )(h, w, xres, g, b)


def _block_diag(w, tile):
    G, d, _ = w.shape
    per = tile // d
    w4 = w.reshape(G // per, per, d, d)
    eye = jnp.eye(per, dtype=w.dtype)
    return jnp.einsum('jipq,ik->jipkq', w4, eye).reshape(G // per, tile, tile)


def kernel(x, w_in, rg_conv_w, rg_conv_b, rg_gate_a_w, rg_gate_a_b, rg_gate_x_w, rg_gate_x_b, rg_lambda,
           lam_q1, lam_k1, lam_q2, lam_k2, subln_g, w_out, ln_mix_g, ln_mix_b, w_up, ffn_conv_w,
           ffn_conv_b, w_down, ln_ffn_g, ln_ffn_b):
    B, S, D = x.shape
    depth = w_in.shape[0]
    rg_w = rg_conv_w.shape[2]
    head_dim = lam_q1.shape[1]
    v_dim = subln_g.shape[1]
    mix_w = w_out.shape[1]
    attn_w = mix_w - rg_w
    n_heads = attn_w // v_dim
    qkv_w = w_in.shape[2] - 2 * rg_w
    d_ff = w_down.shape[1]
    T = B * S
    alpha = (2.0 * depth) ** 0.25

    vec3 = lambda a: a.reshape(depth, 1, a.shape[-1])
    n_blocks, blk = rg_gate_a_w.shape[1], rg_gate_a_w.shape[2]
    gate_tiles = lambda w: _block_diag(w.reshape(depth * n_blocks, blk, blk), V7X_MXU_DIM)
    wg = jnp.concatenate([gate_tiles(rg_gate_a_w), gate_tiles(rg_gate_x_w)], axis=-1).astype(BF16)
    wg = wg.reshape(depth, rg_w // V7X_MXU_DIM, V7X_MXU_DIM, 2 * V7X_MXU_DIM)
    lamv = jnp.stack([lam_q1, lam_k1, lam_q2, lam_k2], axis=1)
    rg_cb, rg_ba, rg_bx, rg_lam = vec3(rg_conv_b), vec3(rg_gate_a_b), vec3(rg_gate_x_b), vec3(rg_lambda)
    sub_g, mix_g, mix_b = vec3(subln_g), vec3(ln_mix_g), vec3(ln_mix_b)
    ffn_cb, ffn_g, ffn_b = vec3(ffn_conv_b), vec3(ln_ffn_g), vec3(ln_ffn_b)

    x32 = x.reshape(T, D)
    x16 = x32.astype(BF16)
    for i in range(depth):
        lam_init = 0.8 - 0.6 * math.exp(-0.3 * i)
        rg_out, qkv = _inproj_rglru(x16, w_in, i, rg_conv_w, rg_cb, wg, rg_ba, rg_bx, rg_lam,
                                    batch=B, seq=S, width=rg_w, qkv_width=qkv_w, out_dtype=BF16)
        at_out = _diff_attn(qkv, lamv, sub_g, i, batch=B, seq=S, n_heads=n_heads,
                            head_dim=head_dim, v_dim=v_dim, lam_init=lam_init, out_dtype=BF16,
                            tq=min(512, S), tk=min(512, S))
        x32, x16 = _outproj_ln(rg_out, at_out, w_out, i, x32, mix_g, mix_b, alpha=alpha, tm=512, sub=256)
        hmid = _ffn_up(x16, w_up, i, ffn_conv_w, ffn_cb, seq=S, d_ff=d_ff, tm=S, tn=512, out_dtype=BF16)
        x32, x16 = _ffn_down_ln(hmid, w_down, i, x32, ffn_g, ffn_b, alpha=alpha, tm=256, kc=512)
    return x32.reshape(B, S, D)
```

```python
import functools
import math

import jax
import jax.numpy as jnp
from jax import lax
from jax.experimental import pallas as pl
from jax.experimental.pallas import tpu as pltpu

F32 = jnp.float32
BF16 = jnp.bfloat16

V7X_LANES = 128
V7X_SUBLANES = 8
V7X_MXU_DIM = 256
V7X_VMEM_BYTES = 64 * 1024 * 1024
VMEM_TEMP_BYTES = 8 * 1024 * 1024
VMEM_UNSCOPED_BYTES = 4 * 1024 * 1024
VMEM_MIN_REQUEST_BYTES = 16 * 1024 * 1024

RG_CHUNK_ROWS = 128
PROJ_SUB_ROWS = 256
LN_ROWS = 64

RG_C = 8.0
LN_EPS = 1e-5
RMS_EPS = 1e-5
MASK_VALUE = -1e30


def _vmem_limit(nbytes):
    return int(min(V7X_VMEM_BYTES - VMEM_UNSCOPED_BYTES, max(nbytes + VMEM_TEMP_BYTES, VMEM_MIN_REQUEST_BYTES)))


def _gelu_tanh_times(x, half_y):
    c = math.sqrt(2.0 / math.pi)
    t = jnp.tanh(x * (c + (c * 0.044715) * (x * x)))
    return (x + x * t) * half_y


def _causal_conv_rows(pad_ref, cw, cb, r0, rows):
    conv_k = cw.shape[0]
    pad = V7X_SUBLANES
    strips = []
    for s in range(pad_ref.shape[0]):
        lanes = slice(s * V7X_LANES, (s + 1) * V7X_LANES)
        out = cb[:, lanes] + cw[conv_k - 1:conv_k, lanes] * pad_ref[s, pl.ds(r0 + pad, rows), :]
        for k in range(conv_k - 1):
            start = r0 + pad - (conv_k - 1) + k
            out = out + cw[k:k + 1, lanes] * pad_ref[s, pl.ds(start, rows, stride=1), :]
        strips.append(out)
    return strips[0] if len(strips) == 1 else jnp.concatenate(strips, axis=1)


def _store_slabs(pad_ref, r0, value):
    rows = value.shape[0]
    for s in range(pad_ref.shape[0]):
        pad_ref[s, V7X_SUBLANES + r0:V7X_SUBLANES + r0 + rows, :] = value[:, s * V7X_LANES:(s + 1) * V7X_LANES]


def _inproj_rglru_kernel(x16_ref, wx32_ref, wgt32_ref, wq32a_ref, wq32b_ref, wq32c_ref, cw_ref, cb_ref, wg_ref,
                         ba_ref, bx_ref, lam_ref, o_ref, qkv_ref, xpad_ref, gate_ref, hs_ref,
                         wx_ref, wgt_ref, wq_ref, *, rows, sub_rows):
    S, C = o_ref.shape
    pad = V7X_SUBLANES
    xpad_ref[:, 0:pad, :] = jnp.zeros((xpad_ref.shape[0], pad, V7X_LANES), F32)
    wx_ref[...] = wx32_ref[...].astype(wx_ref.dtype)
    wgt_ref[...] = wgt32_ref[...].astype(wgt_ref.dtype)
    for part, ref in enumerate((wq32a_ref, wq32b_ref, wq32c_ref)):
        wq_ref[:, part * C:(part + 1) * C] = ref[...].astype(wq_ref.dtype)

    lam = lam_ref[...]
    log_sig = jnp.minimum(lam, 0.0) - jnp.log1p(jnp.exp(-jnp.abs(lam)))
    cw = cw_ref[...]
    cb = cb_ref[...]
    ba = ba_ref[...]
    bx = bx_ref[...]
    wg = wg_ref[0]
    n_groups = rows // V7X_SUBLANES
    sub3 = lax.broadcasted_iota(jnp.int32, (n_groups, V7X_SUBLANES, C), 1)
    n_sub = S // sub_rows

    def chunk(t0, h_prev):
        u = _causal_conv_rows(xpad_ref, cw, cb, t0, rows)
        pre = jnp.dot(u.astype(BF16), wg, preferred_element_type=F32)
        r = jax.nn.sigmoid(pre[:, :C] + ba)
        ig = jax.nn.sigmoid(pre[:, C:] + bx)
        log_a = RG_C * r * log_sig
        a = jnp.exp(log_a)
        th = jnp.tanh(log_a)
        m2 = -2.0 * th / (1.0 - th)
        mult = jnp.where(m2 > 0.0, m2 * lax.rsqrt(m2), 0.0)
        b = mult * (ig * u)
        a = a.reshape(n_groups, V7X_SUBLANES, C)
        b = b.reshape(n_groups, V7X_SUBLANES, C)
        for s in (1, 2, 4):
            keep = sub3 >= s
            a_sh = jnp.where(keep, pltpu.roll(a, s, axis=1), 1.0)
            b_sh = jnp.where(keep, pltpu.roll(b, s, axis=1), 0.0)
            b = a * b_sh + b
            a = a * a_sh
        a = a.reshape(rows, C)
        b = b.reshape(rows, C)
        h = h_prev
        for g in range(n_groups):
            lo = g * V7X_SUBLANES
            hg = a[lo:lo + V7X_SUBLANES, :] * h + b[lo:lo + V7X_SUBLANES, :]
            h = hg[V7X_SUBLANES - 1:V7X_SUBLANES, :]
            hs_ref[t0 + lo:t0 + lo + V7X_SUBLANES, :] = hg
        gate = gate_ref[t0:t0 + rows, :]
        o_ref[t0:t0 + rows, :] = _gelu_tanh_times(gate, 0.5 * hs_ref[t0:t0 + rows, :]).astype(o_ref.dtype)
        return h

    h = jnp.zeros((1, C), F32)
    for si in range(n_sub):
        r0 = si * sub_rows
        xs = x16_ref[r0:r0 + sub_rows, :]
        _store_slabs(xpad_ref, r0, jnp.dot(xs, wx_ref[...], preferred_element_type=F32))
        gate_ref[r0:r0 + sub_rows, :] = jnp.dot(xs, wgt_ref[...], preferred_element_type=F32)
        qkv_ref[r0:r0 + sub_rows, :] = jnp.dot(xs, wq_ref[...],
                                               preferred_element_type=F32).astype(qkv_ref.dtype)
        for ci in range(sub_rows // rows):
            h = chunk(r0 + ci * rows, h)


def _inproj_rglru(x16, w_in, layer, cw, cb, wg, ba, bx, lam, *, batch, seq, width, qkv_width, out_dtype):
    C = wg.shape[2]
    nct = width // C
    T, D = x16.shape
    nqc = qkv_width // nct
    q_parts = nqc // C
    q_off = 2 * width // C
    conv_k = cw.shape[1]
    assert qkv_width % nct == 0 and nqc == 3 * C
    vec = lambda: pl.BlockSpec((None, 1, C), lambda b, c: (layer, 0, c))
    wq_spec = lambda part: pl.BlockSpec((None, D, C), lambda b, c: (layer, 0, q_off + q_parts * c + part))
    est = 2 * (seq * D * 2 + 2 * D * C * 4 + D * nqc * 4 + seq * C * 2 + seq * nqc * 2 + C * 2 * C * 2) \
        + 2 * D * C * 2 + D * nqc * 2 + (3 * seq + 8) * C * 4 + PROJ_SUB_ROWS * nqc * 4
    return pl.pallas_call(
        functools.partial(_inproj_rglru_kernel, rows=RG_CHUNK_ROWS, sub_rows=PROJ_SUB_ROWS),
        grid=(batch, nct),
        in_specs=[
            pl.BlockSpec((seq, D), lambda b, c: (b, 0)),
            pl.BlockSpec((None, D, C), lambda b, c: (layer, 0, c)),
            pl.BlockSpec((None, D, C), lambda b, c: (layer, 0, nct + c)),
            wq_spec(0), wq_spec(1), wq_spec(2),
            pl.BlockSpec((None, conv_k, C), lambda b, c: (layer, 0, c)),
            vec(),
            pl.BlockSpec((None, 1, C, 2 * C), lambda b, c: (layer, c, 0, 0)),
            vec(), vec(), vec(),
        ],
        out_specs=[pl.BlockSpec((seq, C), lambda b, c: (b, c)),
                   pl.BlockSpec((seq, nqc), lambda b, c: (b, c))],
        out_shape=[jax.ShapeDtypeStruct((batch * seq, width), out_dtype),
                   jax.ShapeDtypeStruct((T, qkv_width), BF16)],
        scratch_shapes=[pltpu.VMEM((C // V7X_LANES, seq + V7X_SUBLANES, V7X_LANES), F32),
                        pltpu.VMEM((seq, C), F32), pltpu.VMEM((seq, C), F32),
                        pltpu.VMEM((D, C), BF16), pltpu.VMEM((D, C), BF16), pltpu.VMEM((D, nqc), BF16)],
        compiler_params=pltpu.CompilerParams(
            dimension_semantics=("parallel", "parallel"), vmem_limit_bytes=_vmem_limit(est)),
        name="inproj_rglru",
    )(x16, w_in, w_in, w_in, w_in, w_in, cw, cb, wg, ba, bx, lam)


POS_SPLIT = 256
ONES_ROWS = 16


def _attn_kernel(q_ref, k_ref, v_ref, lamv_ref, g_ref, o_ref, ka_ref, vt_ref, s_ref, p_ref,
                 *, tq, tk, n_heads, head_dim, scale, lam_init):
    h = pl.program_id(1)
    S, d2 = q_ref.shape
    V = v_ref.shape[1]
    n_kv = S // tk
    n_q = S // tq

    def extra_cols(shape, c, lo_val, hi_val):
        lane = lax.broadcasted_iota(jnp.int32, shape, 1)
        base = head_dim * (1 - c)
        return jnp.where(lane == base, lo_val, jnp.where(lane == base + 1, hi_val, 0.0))

    def own_half(shape, c):
        lane = lax.broadcasted_iota(jnp.int32, shape, 1)
        return (lane < head_dim) if c == 0 else (lane >= head_dim)

    for jb in range(n_kv):
        rows = slice(jb * tk, (jb + 1) * tk)
        vt_ref[0:V, rows] = v_ref[rows, :].astype(F32).T.astype(vt_ref.dtype)
        vt_ref[V:V + ONES_ROWS, rows] = jnp.ones((ONES_ROWS, tk), vt_ref.dtype)
        kb = k_ref[rows, :]
        pos = lax.broadcasted_iota(jnp.int32, (tk, d2), 0) + jb * tk
        j_lo = (pos % POS_SPLIT).astype(F32)
        j_hi = (pos // POS_SPLIT).astype(F32)
        for c in range(2):
            ka_ref[c, rows, :] = jnp.where(own_half((tk, d2), c), kb,
                                           extra_cols((tk, d2), c, j_lo, j_hi).astype(kb.dtype))

    slope = jnp.exp2(jnp.full((1, 1), -8.0 / n_heads, F32) * (h + 1).astype(F32))
    lv = lamv_ref[...]
    lam = (jnp.exp(jnp.sum(lv[0:1, :] * lv[1:2, :], axis=-1, keepdims=True))
           - jnp.exp(jnp.sum(lv[2:3, :] * lv[3:4, :], axis=-1, keepdims=True)) + lam_init)
    gain = g_ref[...] * (1.0 - lam_init)

    q_aug = {}

    def augmented_queries(qi):
        if qi not in q_aug:
            q = q_ref[qi * tq:(qi + 1) * tq, :] * jnp.asarray(scale, q_ref.dtype)
            q_aug[qi] = [jnp.where(own_half(q.shape, c), q,
                                   extra_cols(q.shape, c, slope, slope * POS_SPLIT).astype(q.dtype))
                         for c in range(2)]
        return q_aug[qi]

    def on_diagonal(qi, j):
        return tq == tk and j == qi and tk % 2 == 0

    def scores(qi, j, tile):
        qa = augmented_queries(qi)
        half_k = tk // 2
        half_q = tq // 2
        for c in range(2):
            keys = ka_ref[c, j * tk:(j + 1) * tk, :]
            if on_diagonal(qi, j):
                s_ref[tile, 0:half_k, c * tq:(c + 1) * tq] = lax.dot_general(
                    keys[0:half_k], qa[c], (((1,), (1,)), ((), ())), preferred_element_type=F32)
                s_ref[tile, half_k:tk, c * tq:c * tq + half_q] = jnp.full((tk - half_k, half_q), MASK_VALUE, F32)
                s_ref[tile, half_k:tk, c * tq + half_q:(c + 1) * tq] = lax.dot_general(
                    keys[half_k:tk], qa[c][half_q:tq], (((1,), (1,)), ((), ())), preferred_element_type=F32)
            else:
                s_ref[tile, :, c * tq:(c + 1) * tq] = lax.dot_general(
                    keys, qa[c], (((1,), (1,)), ((), ())), preferred_element_type=F32)

    def n_tiles(qi):
        return ((qi + 1) * tq + tk - 1) // tk

    def all_scores(qi):
        for j in range(n_tiles(qi)):
            scores(qi, j, (qi % 2) * n_kv + j)

    def masked_scores(qi, j):
        s = s_ref[(qi % 2) * n_kv + j]
        if (j + 1) * tk - 1 > qi * tq:
            key_pos = lax.broadcasted_iota(jnp.int32, (tk, 2 * tq), 0) + j * tk
            col = lax.broadcasted_iota(jnp.int32, (tk, 2 * tq), 1)
            query_pos = jnp.where(col >= tq, col - tq, col) + qi * tq
            s = jnp.where(key_pos <= query_pos, s, MASK_VALUE)
        return s

    def softmax_weights(qi):
        tiles = [masked_scores(qi, j) for j in range(n_tiles(qi))]
        m = functools.reduce(jnp.maximum, [jnp.max(s, axis=0, keepdims=True) for s in tiles])
        for j, s in enumerate(tiles):
            p_ref[qi % 2, j * tk:(j + 1) * tk, :] = jnp.exp(s - m).astype(p_ref.dtype)

    def weighted_values(qi):
        k_end = n_tiles(qi) * tk
        acc = jnp.dot(vt_ref[:, 0:k_end], p_ref[qi % 2, 0:k_end, :], preferred_element_type=F32)
        ot = (acc[0:V, 0:tq] / acc[V:V + 1, 0:tq]
              - lam * (acc[0:V, tq:2 * tq] / acc[V:V + 1, tq:2 * tq]))
        ms = jnp.mean(ot * ot, axis=0, keepdims=True)
        o = (ot * lax.rsqrt(ms + RMS_EPS)).T * gain
        o_ref[qi * tq:(qi + 1) * tq, :] = o.astype(o_ref.dtype)

    all_scores(0)
    for qi in range(n_q):
        if qi + 1 < n_q:
            all_scores(qi + 1)
        softmax_weights(qi)
        if qi >= 1:
            weighted_values(qi - 1)
    weighted_values(n_q - 1)


def _diff_attn(qkv, lamv, g, layer, *, batch, seq, n_heads, head_dim, v_dim, lam_init, out_dtype, tq, tk):
    qk_w = 2 * head_dim
    assert qk_w == v_dim and seq % tq == 0 and seq % tk == 0 and seq % POS_SPLIT == 0
    assert seq // POS_SPLIT <= POS_SPLIT
    return pl.pallas_call(
        functools.partial(_attn_kernel, tq=tq, tk=tk, n_heads=n_heads, head_dim=head_dim,
                          scale=head_dim ** -0.5, lam_init=lam_init),
        grid=(batch, n_heads),
        in_specs=[
            pl.BlockSpec((seq, qk_w), lambda b, h: (b, h)),
            pl.BlockSpec((seq, qk_w), lambda b, h: (b, n_heads + h)),
            pl.BlockSpec((seq, v_dim), lambda b, h: (b, 2 * n_heads + h)),
            pl.BlockSpec((None,) + lamv.shape[1:], lambda b, h: (layer, 0, 0)),
            pl.BlockSpec((None, 1, v_dim), lambda b, h: (layer, 0, 0)),
        ],
        out_specs=pl.BlockSpec((seq, v_dim), lambda b, h: (b, h)),
        out_shape=jax.ShapeDtypeStruct((batch * seq, n_heads * v_dim), out_dtype),
        scratch_shapes=[
            pltpu.VMEM((2, seq, qk_w), BF16),
            pltpu.VMEM((v_dim + ONES_ROWS, seq), BF16),
            pltpu.VMEM((2 * (seq // tk), tk, 2 * tq), F32),
            pltpu.VMEM((2, seq, 2 * tq), BF16),
        ],
        compiler_params=pltpu.CompilerParams(
            dimension_semantics=("parallel", "parallel"),
            vmem_limit_bytes=V7X_VMEM_BYTES - VMEM_UNSCOPED_BYTES),
        name="diff_attn",
    )(qkv, qkv, qkv, lamv, g)


def _residual_layernorm(acc_ref, xres_ref, g, b, o32_ref, o16_ref, *, alpha, r0, n_rows, rows):
    for c in range(n_rows // rows):
        lo = r0 + c * rows
        y = alpha * xres_ref[lo:lo + rows, :] + acc_ref[lo:lo + rows, :]
        mu = jnp.mean(y, axis=-1, keepdims=True)
        d = y - mu
        var = jnp.mean(d * d, axis=-1, keepdims=True)
        out = d * lax.rsqrt(var + LN_EPS) * g + b
        o32_ref[lo:lo + rows, :] = out
        o16_ref[lo:lo + rows, :] = out.astype(o16_ref.dtype)


def _outproj_kernel(rg_ref, at_ref, w32_ref, xres_ref, g_ref, b_ref, o32_ref, o16_ref, acc_ref, w_ref,
                    *, alpha, sub):
    tm, half = rg_ref.shape

    @pl.when(pl.program_id(0) == 0)
    def _():
        w_ref[...] = w32_ref[...].astype(w_ref.dtype)

    g = g_ref[...]
    b = b_ref[...]
    for t in range(tm // sub):
        r0 = t * sub
        acc_ref[r0:r0 + sub, :] = (
            jnp.dot(rg_ref[r0:r0 + sub, :], w_ref[0:half, :], preferred_element_type=F32)
            + jnp.dot(at_ref[r0:r0 + sub, :], w_ref[half:, :], preferred_element_type=F32))
        _residual_layernorm(acc_ref, xres_ref, g, b, o32_ref, o16_ref, alpha=alpha, r0=r0, n_rows=sub,
                            rows=LN_ROWS)


def _outproj_ln(rg_out, at_out, w, layer, xres, g, b, *, alpha, tm, sub):
    T, half = rg_out.shape
    _, Kw, D = w.shape
    est = 2 * (2 * tm * half * 2 + tm * D * 4 + tm * D * 4 + tm * D * 2) + Kw * D * (4 + 2) \
        + 2 * tm * D * 4
    row = lambda width: pl.BlockSpec((tm, width), lambda m: (m, 0))
    full = lambda a: pl.BlockSpec((None,) + a.shape[1:], lambda m: (layer, 0, 0))
    return pl.pallas_call(
        functools.partial(_outproj_kernel, alpha=alpha, sub=sub),
        grid=(T // tm,),
        in_specs=[row(half), row(half),
                  pl.BlockSpec((None, Kw, D), lambda m: (layer, 0, 0), pipeline_mode=pl.Buffered(1)),
                  row(D), full(g), full(b)],
        out_specs=[row(D), row(D)],
        out_shape=[jax.ShapeDtypeStruct((T, D), F32), jax.ShapeDtypeStruct((T, D), BF16)],
        scratch_shapes=[pltpu.VMEM((tm, D), F32), pltpu.VMEM((Kw, D), BF16)],
        compiler_params=pltpu.CompilerParams(
            dimension_semantics=("arbitrary",), vmem_limit_bytes=_vmem_limit(est)),
        name="outproj_ln",
    )(rg_out, at_out, w, xres, g, b)


def _ffn_up_kernel(x_ref, wg32_ref, wu32_ref, cwg_ref, cwu_ref, cbg_ref, cbu_ref, o_ref, pg_ref, pu_ref,
                   wg_ref, wu_ref, *, tiles_per_seq, rows):
    tm, tn = o_ref.shape
    pad = V7X_SUBLANES
    m = pl.program_id(1)

    @pl.when(m == 0)
    def _():
        wg_ref[...] = wg32_ref[...].astype(wg_ref.dtype)
        wu_ref[...] = wu32_ref[...].astype(wu_ref.dtype)

    @pl.when(m % tiles_per_seq == 0)
    def _():
        pg_ref[:, 0:pad, :] = jnp.zeros((pg_ref.shape[0], pad, V7X_LANES), F32)
        pu_ref[:, 0:pad, :] = jnp.zeros((pu_ref.shape[0], pad, V7X_LANES), F32)

    cwg = cwg_ref[...]
    cbg = cbg_ref[...]
    cwu = 0.5 * cwu_ref[...]
    cbu = 0.5 * cbu_ref[...]
    for c in range(tm // rows):
        r0 = c * rows
        xs = x_ref[r0:r0 + rows, :]
        _store_slabs(pg_ref, r0, jnp.dot(xs, wg_ref[...], preferred_element_type=F32))
        _store_slabs(pu_ref, r0, jnp.dot(xs, wu_ref[...], preferred_element_type=F32))
        gv = _causal_conv_rows(pg_ref, cwg, cbg, r0, rows)
        uv = _causal_conv_rows(pu_ref, cwu, cbu, r0, rows)
        o_ref[r0:r0 + rows, :] = _gelu_tanh_times(gv, uv).astype(o_ref.dtype)
    pg_ref[:, 0:pad, :] = pg_ref[:, tm:tm + pad, :]
    pu_ref[:, 0:pad, :] = pu_ref[:, tm:tm + pad, :]


def _ffn_up(x, w, layer, cw, cb, *, seq, d_ff, tm, tn, out_dtype):
    T, K = x.shape
    nt = d_ff // tn
    conv_k = cw.shape[1]
    assert seq % tm == 0 and d_ff % tn == 0
    est = 2 * (tm * K * 2 + 2 * K * tn * 4 + tm * tn * 2) + 2 * K * tn * 2 + 2 * (tm + 8) * tn * 4 \
        + 2 * tm * tn * 4
    vecg = lambda r: pl.BlockSpec((None, r, tn), lambda n, m: (layer, 0, n))
    vecu = lambda r: pl.BlockSpec((None, r, tn), lambda n, m: (layer, 0, n + nt))
    return pl.pallas_call(
        functools.partial(_ffn_up_kernel, tiles_per_seq=seq // tm, rows=min(PROJ_SUB_ROWS, tm)),
        grid=(nt, T // tm),
        in_specs=[
            pl.BlockSpec((tm, K), lambda n, m: (m, 0)),
            pl.BlockSpec((None, K, tn), lambda n, m: (layer, 0, n)),
            pl.BlockSpec((None, K, tn), lambda n, m: (layer, 0, n + nt)),
            vecg(conv_k), vecu(conv_k), vecg(1), vecu(1),
        ],
        out_specs=pl.BlockSpec((tm, tn), lambda n, m: (m, n)),
        out_shape=jax.ShapeDtypeStruct((T, d_ff), out_dtype),
        scratch_shapes=[pltpu.VMEM((tn // V7X_LANES, tm + V7X_SUBLANES, V7X_LANES), F32),
                        pltpu.VMEM((tn // V7X_LANES, tm + V7X_SUBLANES, V7X_LANES), F32),
                        pltpu.VMEM((K, tn), BF16), pltpu.VMEM((K, tn), BF16)],
        compiler_params=pltpu.CompilerParams(
            dimension_semantics=("parallel", "arbitrary"), vmem_limit_bytes=_vmem_limit(est)),
        name="ffn_up",
    )(x, w, w, cw, cw, cb, cb)


def _ffn_down_kernel(h_ref, w32_ref, xres_ref, g_ref, b_ref, o32_ref, o16_ref, acc_ref, w_ref, *, alpha, n_pre):
    s = pl.program_id(0)
    kc = w32_ref.shape[0]

    @pl.when(s < n_pre)
    def _():
        w_ref[pl.ds(pl.multiple_of(s * kc, kc), kc), :] = w32_ref[...].astype(w_ref.dtype)

    @pl.when(s >= n_pre)
    def _():
        tm = h_ref.shape[0]
        acc_ref[...] = jnp.dot(h_ref[...], w_ref[...], preferred_element_type=F32)
        _residual_layernorm(acc_ref, xres_ref, g_ref[...], b_ref[...], o32_ref, o16_ref, alpha=alpha, r0=0,
                            n_rows=tm, rows=LN_ROWS)


def _ffn_down_ln(h, w, layer, xres, g, b, *, alpha, tm, kc):
    T, K = h.shape
    D = w.shape[2]
    assert T % tm == 0 and K % kc == 0
    n_pre = K // kc
    est = 2 * (tm * K * 2 + kc * D * 4 + tm * D * 4 + tm * D * 4 + tm * D * 2) + K * D * 2 + 2 * tm * D * 4
    tile = lambda s: jnp.maximum(s - n_pre, 0)
    row = lambda width: pl.BlockSpec((tm, width), lambda s: (tile(s), 0))
    full = lambda a: pl.BlockSpec((None,) + a.shape[1:], lambda s: (layer, 0, 0))
    return pl.pallas_call(
        functools.partial(_ffn_down_kernel, alpha=alpha, n_pre=n_pre),
        grid=(n_pre + T // tm,),
        in_specs=[row(K),
                  pl.BlockSpec((None, kc, D), lambda s: (layer, jnp.minimum(s, n_pre - 1), 0)),
                  row(D), full(g), full(b)],
        out_specs=[row(D), row(D)],
        out_shape=[jax.ShapeDtypeStruct((T, D), F32), jax.ShapeDtypeStruct((T, D), BF16)],
        scratch_shapes=[pltpu.VMEM((tm, D), F32), pltpu.VMEM((K, D), BF16)],
        compiler_params=pltpu.CompilerParams(
            dimension_semantics=("arbitrary",), vmem_limit_bytes=_vmem_limit(est)),
        name="ffn_down_ln",
    )(h, w, xres, g, b)


def _block_diag(w, tile):
    G, d, _ = w.shape
    per = tile // d
    w4 = w.reshape(G // per, per, d, d)
    eye = jnp.eye(per, dtype=w.dtype)
    return jnp.einsum('jipq,ik->jipkq', w4, eye).reshape(G // per, tile, tile)


def kernel(x, w_in, rg_conv_w, rg_conv_b, rg_gate_a_w, rg_gate_a_b, rg_gate_x_w, rg_gate_x_b, rg_lambda,
           lam_q1, lam_k1, lam_q2, lam_k2, subln_g, w_out, ln_mix_g, ln_mix_b, w_up, ffn_conv_w,
           ffn_conv_b, w_down, ln_ffn_g, ln_ffn_b):
    B, S, D = x.shape
    depth = w_in.shape[0]
    rg_w = rg_conv_w.shape[2]
    head_dim = lam_q1.shape[1]
    v_dim = subln_g.shape[1]
    mix_w = w_out.shape[1]
    attn_w = mix_w - rg_w
    n_heads = attn_w // v_dim
    qkv_w = w_in.shape[2] - 2 * rg_w
    d_ff = w_down.shape[1]
    T = B * S
    alpha = (2.0 * depth) ** 0.25

    vec3 = lambda a: a.reshape(depth, 1, a.shape[-1])
    n_blocks, blk = rg_gate_a_w.shape[1], rg_gate_a_w.shape[2]
    gate_tiles = lambda w: _block_diag(w.reshape(depth * n_blocks, blk, blk), V7X_MXU_DIM)
    wg = jnp.concatenate([gate_tiles(rg_gate_a_w), gate_tiles(rg_gate_x_w)], axis=-1).astype(BF16)
    wg = wg.reshape(depth, rg_w // V7X_MXU_DIM, V7X_MXU_DIM, 2 * V7X_MXU_DIM)
    lamv = jnp.stack([lam_q1, lam_k1, lam_q2, lam_k2], axis=1)
    rg_cb, rg_ba, rg_bx, rg_lam = vec3(rg_conv_b), vec3(rg_gate_a_b), vec3(rg_gate_x_b), vec3(rg_lambda)
    sub_g, mix_g, mix_b = vec3(subln_g), vec3(ln_mix_g), vec3(ln_mix_b)
    ffn_cb, ffn_g, ffn_b = vec3(ffn_conv_b), vec3(ln_ffn_g), vec3(ln_ffn_b)

    x32 = x.reshape(T, D)
    x16 = x32.astype(BF16)
    for i in range(depth):
        lam_init = 0.8 - 0.6 * math.exp(-0.3 * i)
        rg_out, qkv = _inproj_rglru(x16, w_in, i, rg_conv_w, rg_cb, wg, rg_ba, rg_bx, rg_lam,
                                    batch=B, seq=S, width=rg_w, qkv_width=qkv_w, out_dtype=BF16)
        at_out = _diff_attn(qkv, lamv, sub_g, i, batch=B, seq=S, n_heads=n_heads,
                            head_dim=head_dim, v_dim=v_dim, lam_init=lam_init, out_dtype=BF16,
                            tq=min(512, S), tk=min(512, S))
        x32, x16 = _outproj_ln(rg_out, at_out, w_out, i, x32, mix_g, mix_b, alpha=alpha, tm=512, sub=256)
        hmid = _ffn_up(x16, w_up, i, ffn_conv_w, ffn_cb, seq=S, d_ff=d_ff, tm=S, tn=512, out_dtype=BF16)
        x32, x16 = _ffn_down_ln(hmid, w_down, i, x32, ffn_g, ffn_b, alpha=alpha, tm=256, kc=512)
    return x32.reshape(B, S, D)
```

```python
import functools
import math

import jax
import jax.numpy as jnp
from jax import lax
from jax.experimental import pallas as pl
from jax.experimental.pallas import tpu as pltpu

F32 = jnp.float32
BF16 = jnp.bfloat16

V7X_LANES = 128
V7X_SUBLANES = 8
V7X_MXU_DIM = 256
V7X_VMEM_BYTES = 64 * 1024 * 1024
VMEM_TEMP_BYTES = 8 * 1024 * 1024
VMEM_UNSCOPED_BYTES = 4 * 1024 * 1024
VMEM_MIN_REQUEST_BYTES = 16 * 1024 * 1024

RG_CHUNK_ROWS = 128
PROJ_SUB_ROWS = 256
LN_ROWS = 64

RG_C = 8.0
LN_EPS = 1e-5
RMS_EPS = 1e-5
MASK_VALUE = -1e30


def _vmem_limit(nbytes):
    return int(min(V7X_VMEM_BYTES - VMEM_UNSCOPED_BYTES, max(nbytes + VMEM_TEMP_BYTES, VMEM_MIN_REQUEST_BYTES)))


def _gelu_tanh_times(x, half_y):
    c = math.sqrt(2.0 / math.pi)
    t = jnp.tanh(x * (c + (c * 0.044715) * (x * x)))
    return (x + x * t) * half_y


def _causal_conv_rows(pad_ref, cw, cb, r0, rows):
    conv_k = cw.shape[0]
    pad = V7X_SUBLANES
    strips = []
    for s in range(pad_ref.shape[0]):
        lanes = slice(s * V7X_LANES, (s + 1) * V7X_LANES)
        out = cb[:, lanes] + cw[conv_k - 1:conv_k, lanes] * pad_ref[s, pl.ds(r0 + pad, rows), :]
        for k in range(conv_k - 1):
            start = r0 + pad - (conv_k - 1) + k
            out = out + cw[k:k + 1, lanes] * pad_ref[s, pl.ds(start, rows, stride=1), :]
        strips.append(out)
    return strips[0] if len(strips) == 1 else jnp.concatenate(strips, axis=1)


def _store_slabs(pad_ref, r0, value):
    rows = value.shape[0]
    for s in range(pad_ref.shape[0]):
        pad_ref[s, V7X_SUBLANES + r0:V7X_SUBLANES + r0 + rows, :] = value[:, s * V7X_LANES:(s + 1) * V7X_LANES]


def _inproj_rglru_kernel(x16_ref, wx32_ref, wgt32_ref, wq32a_ref, wq32b_ref, wq32c_ref, cw_ref, cb_ref, wg_ref,
                         ba_ref, bx_ref, lam_ref, o_ref, qkv_ref, xpad_ref, gate_ref, hs_ref,
                         wx_ref, wgt_ref, wq_ref, *, rows, sub_rows):
    S, C = o_ref.shape
    pad = V7X_SUBLANES
    xpad_ref[:, 0:pad, :] = jnp.zeros((xpad_ref.shape[0], pad, V7X_LANES), F32)
    wx_ref[...] = wx32_ref[...].astype(wx_ref.dtype)
    wgt_ref[...] = wgt32_ref[...].astype(wgt_ref.dtype)
    for part, ref in enumerate((wq32a_ref, wq32b_ref, wq32c_ref)):
        wq_ref[:, part * C:(part + 1) * C] = ref[...].astype(wq_ref.dtype)

    lam = lam_ref[...]
    log_sig = jnp.minimum(lam, 0.0) - jnp.log1p(jnp.exp(-jnp.abs(lam)))
    cw = cw_ref[...]
    cb = cb_ref[...]
    ba = ba_ref[...]
    bx = bx_ref[...]
    wg = wg_ref[0]
    n_groups = rows // V7X_SUBLANES
    sub3 = lax.broadcasted_iota(jnp.int32, (n_groups, V7X_SUBLANES, C), 1)
    n_sub = S // sub_rows

    def chunk(t0, h_prev):
        u = _causal_conv_rows(xpad_ref, cw, cb, t0, rows)
        pre = jnp.dot(u.astype(BF16), wg, preferred_element_type=F32)
        r = jax.nn.sigmoid(pre[:, :C] + ba)
        ig = jax.nn.sigmoid(pre[:, C:] + bx)
        log_a = RG_C * r * log_sig
        a = jnp.exp(log_a)
        th = jnp.tanh(log_a)
        m2 = -2.0 * th / (1.0 - th)
        mult = jnp.where(m2 > 0.0, m2 * lax.rsqrt(m2), 0.0)
        b = mult * (ig * u)
        a = a.reshape(n_groups, V7X_SUBLANES, C)
        b = b.reshape(n_groups, V7X_SUBLANES, C)
        for s in (1, 2, 4):
            keep = sub3 >= s
            a_sh = jnp.where(keep, pltpu.roll(a, s, axis=1), 1.0)
            b_sh = jnp.where(keep, pltpu.roll(b, s, axis=1), 0.0)
            b = a * b_sh + b
            a = a * a_sh
        a = a.reshape(rows, C)
        b = b.reshape(rows, C)
        h = h_prev
        for g in range(n_groups):
            lo = g * V7X_SUBLANES
            hg = a[lo:lo + V7X_SUBLANES, :] * h + b[lo:lo + V7X_SUBLANES, :]
            h = hg[V7X_SUBLANES - 1:V7X_SUBLANES, :]
            hs_ref[t0 + lo:t0 + lo + V7X_SUBLANES, :] = hg
        gate = gate_ref[t0:t0 + rows, :]
        o_ref[t0:t0 + rows, :] = _gelu_tanh_times(gate, 0.5 * hs_ref[t0:t0 + rows, :]).astype(o_ref.dtype)
        return h

    h = jnp.zeros((1, C), F32)
    for si in range(n_sub):
        r0 = si * sub_rows
        xs = x16_ref[r0:r0 + sub_rows, :]
        _store_slabs(xpad_ref, r0, jnp.dot(xs, wx_ref[...], preferred_element_type=F32))
        gate_ref[r0:r0 + sub_rows, :] = jnp.dot(xs, wgt_ref[...], preferred_element_type=F32)
        qkv_ref[r0:r0 + sub_rows, :] = jnp.dot(xs, wq_ref[...],
                                               preferred_element_type=F32).astype(qkv_ref.dtype)
        for ci in range(sub_rows // rows):
            h = chunk(r0 + ci * rows, h)


def _inproj_rglru(x16, w_in, layer, cw, cb, wg, ba, bx, lam, *, batch, seq, width, qkv_width, out_dtype):
    C = wg.shape[2]
    nct = width // C
    T, D = x16.shape
    nqc = qkv_width // nct
    q_parts = nqc // C
    q_off = 2 * width // C
    conv_k = cw.shape[1]
    assert qkv_width % nct == 0 and nqc == 3 * C
    vec = lambda: pl.BlockSpec((None, 1, C), lambda b, c: (layer, 0, c))
    wq_spec = lambda part: pl.BlockSpec((None, D, C), lambda b, c: (layer, 0, q_off + q_parts * c + part))
    est = 2 * (seq * D * 2 + 2 * D * C * 4 + D * nqc * 4 + seq * C * 2 + seq * nqc * 2 + C * 2 * C * 2) \
        + 2 * D * C * 2 + D * nqc * 2 + (3 * seq + 8) * C * 4 + PROJ_SUB_ROWS * nqc * 4
    return pl.pallas_call(
        functools.partial(_inproj_rglru_kernel, rows=RG_CHUNK_ROWS, sub_rows=PROJ_SUB_ROWS),
        grid=(batch, nct),
        in_specs=[
            pl.BlockSpec((seq, D), lambda b, c: (b, 0)),
            pl.BlockSpec((None, D, C), lambda b, c: (layer, 0, c)),
            pl.BlockSpec((None, D, C), lambda b, c: (layer, 0, nct + c)),
            wq_spec(0), wq_spec(1), wq_spec(2),
            pl.BlockSpec((None, conv_k, C), lambda b, c: (layer, 0, c)),
            vec(),
            pl.BlockSpec((None, 1, C, 2 * C), lambda b, c: (layer, c, 0, 0)),
            vec(), vec(), vec(),
        ],
        out_specs=[pl.BlockSpec((seq, C), lambda b, c: (b, c)),
                   pl.BlockSpec((seq, nqc), lambda b, c: (b, c))],
        out_shape=[jax.ShapeDtypeStruct((batch * seq, width), out_dtype),
                   jax.ShapeDtypeStruct((T, qkv_width), BF16)],
        scratch_shapes=[pltpu.VMEM((C // V7X_LANES, seq + V7X_SUBLANES, V7X_LANES), F32),
                        pltpu.VMEM((seq, C), F32), pltpu.VMEM((seq, C), F32),
                        pltpu.VMEM((D, C), BF16), pltpu.VMEM((D, C), BF16), pltpu.VMEM((D, nqc), BF16)],
        compiler_params=pltpu.CompilerParams(
            dimension_semantics=("parallel", "parallel"), vmem_limit_bytes=_vmem_limit(est)),
        name="inproj_rglru",
    )(x16, w_in, w_in, w_in, w_in, w_in, cw, cb, wg, ba, bx, lam)


POS_SPLIT = 256
ONES_ROWS = 16


def _attn_kernel(q_ref, k_ref, v_ref, lamv_ref, g_ref, o_ref, ka_ref, vt_ref, s_ref, p_ref,
                 *, tq, tk, hps, n_heads, head_dim, scale, lam_init):
    S = q_ref.shape[0]
    d2 = q_ref.shape[1] // hps
    V = v_ref.shape[1] // hps
    n_kv = S // tk
    n_q = S // tq

    def extra_cols(shape, c, lo_val, hi_val):
        lane = lax.broadcasted_iota(jnp.int32, shape, 1)
        base = head_dim * (1 - c)
        return jnp.where(lane == base, lo_val, jnp.where(lane == base + 1, hi_val, 0.0))

    def own_half(shape, c):
        lane = lax.broadcasted_iota(jnp.int32, shape, 1)
        return (lane < head_dim) if c == 0 else (lane >= head_dim)

    for hh in range(hps):
        for jb in range(n_kv):
            rows = slice(jb * tk, (jb + 1) * tk)
            vt_ref[hh, 0:V, rows] = v_ref[rows, hh * V:(hh + 1) * V].astype(F32).T.astype(vt_ref.dtype)
            vt_ref[hh, V:V + ONES_ROWS, rows] = jnp.ones((ONES_ROWS, tk), vt_ref.dtype)
            kb = k_ref[rows, hh * d2:(hh + 1) * d2]
            pos = lax.broadcasted_iota(jnp.int32, (tk, d2), 0) + jb * tk
            j_lo = (pos % POS_SPLIT).astype(F32)
            j_hi = (pos // POS_SPLIT).astype(F32)
            for c in range(2):
                ka_ref[hh, c, rows, :] = jnp.where(own_half((tk, d2), c), kb,
                                                   extra_cols((tk, d2), c, j_lo, j_hi).astype(kb.dtype))

    head0 = pl.program_id(1) * hps
    slopes = [jnp.exp2(jnp.full((1, 1), -8.0 / n_heads, F32) * (head0 + hh + 1).astype(F32))
              for hh in range(hps)]
    lv = lamv_ref[...]
    lam = (jnp.exp(jnp.sum(lv[0:1, :] * lv[1:2, :], axis=-1, keepdims=True))
           - jnp.exp(jnp.sum(lv[2:3, :] * lv[3:4, :], axis=-1, keepdims=True)) + lam_init)
    gain = g_ref[...] * (1.0 - lam_init)

    units = [(hh, qi) for hh in range(hps) for qi in range(n_q)]

    def n_tiles(qi):
        return ((qi + 1) * tq + tk - 1) // tk

    def on_diagonal(qi, j):
        return tq == tk and j == qi and tk % 2 == 0

    def scores(u):
        hh, qi = units[u]
        q = q_ref[qi * tq:(qi + 1) * tq, hh * d2:(hh + 1) * d2] * jnp.asarray(scale, q_ref.dtype)
        half_k = tk // 2
        half_q = tq // 2
        for c in range(2):
            qa = jnp.where(own_half(q.shape, c), q,
                           extra_cols(q.shape, c, slopes[hh], slopes[hh] * POS_SPLIT).astype(q.dtype))
            for j in range(n_tiles(qi)):
                tile = (u % 2) * n_kv + j
                keys = ka_ref[hh, c, j * tk:(j + 1) * tk, :]
                if on_diagonal(qi, j):
                    s_ref[tile, 0:half_k, c * tq:(c + 1) * tq] = lax.dot_general(
                        keys[0:half_k], qa, (((1,), (1,)), ((), ())), preferred_element_type=F32)
                    s_ref[tile, half_k:tk, c * tq:c * tq + half_q] = jnp.full((tk - half_k, half_q), MASK_VALUE,
                                                                              F32)
                    s_ref[tile, half_k:tk, c * tq + half_q:(c + 1) * tq] = lax.dot_general(
                        keys[half_k:tk], qa[half_q:tq], (((1,), (1,)), ((), ())), preferred_element_type=F32)
                else:
                    s_ref[tile, :, c * tq:(c + 1) * tq] = lax.dot_general(
                        keys, qa, (((1,), (1,)), ((), ())), preferred_element_type=F32)

    def masked_scores(u, j):
        qi = units[u][1]
        s = s_ref[(u % 2) * n_kv + j]
        if (j + 1) * tk - 1 > qi * tq:
            key_pos = lax.broadcasted_iota(jnp.int32, (tk, 2 * tq), 0) + j * tk
            col = lax.broadcasted_iota(jnp.int32, (tk, 2 * tq), 1)
            query_pos = jnp.where(col >= tq, col - tq, col) + qi * tq
            s = jnp.where(key_pos <= query_pos, s, MASK_VALUE)
        return s

    def softmax_weights(u):
        qi = units[u][1]
        tiles = [masked_scores(u, j) for j in range(n_tiles(qi))]
        m = functools.reduce(jnp.maximum, [jnp.max(s, axis=0, keepdims=True) for s in tiles])
        for j, s in enumerate(tiles):
            p_ref[u % 2, j * tk:(j + 1) * tk, :] = jnp.exp(s - m).astype(p_ref.dtype)

    def weighted_values(u):
        hh, qi = units[u]
        k_end = n_tiles(qi) * tk
        acc = jnp.dot(vt_ref[hh, :, 0:k_end], p_ref[u % 2, 0:k_end, :], preferred_element_type=F32)
        ot = (acc[0:V, 0:tq] / acc[V:V + 1, 0:tq]
              - lam * (acc[0:V, tq:2 * tq] / acc[V:V + 1, tq:2 * tq]))
        ms = jnp.mean(ot * ot, axis=0, keepdims=True)
        o = (ot * lax.rsqrt(ms + RMS_EPS)).T * gain
        o_ref[qi * tq:(qi + 1) * tq, hh * V:(hh + 1) * V] = o.astype(o_ref.dtype)

    scores(0)
    for u in range(len(units)):
        if u + 1 < len(units):
            scores(u + 1)
        softmax_weights(u)
        if u >= 1:
            weighted_values(u - 1)
    weighted_values(len(units) - 1)


def _diff_attn(qkv, lamv, g, layer, *, batch, seq, n_heads, head_dim, v_dim, lam_init, out_dtype, tq, tk, hps):
    qk_w = 2 * head_dim
    assert qk_w == v_dim and seq % tq == 0 and seq % tk == 0 and seq % POS_SPLIT == 0
    assert seq // POS_SPLIT <= POS_SPLIT and n_heads % hps == 0
    groups = n_heads // hps
    return pl.pallas_call(
        functools.partial(_attn_kernel, tq=tq, tk=tk, hps=hps, n_heads=n_heads, head_dim=head_dim,
                          scale=head_dim ** -0.5, lam_init=lam_init),
        grid=(batch, groups),
        in_specs=[
            pl.BlockSpec((seq, hps * qk_w), lambda b, h: (b, h)),
            pl.BlockSpec((seq, hps * qk_w), lambda b, h: (b, groups + h)),
            pl.BlockSpec((seq, hps * v_dim), lambda b, h: (b, 2 * groups + h)),
            pl.BlockSpec((None,) + lamv.shape[1:], lambda b, h: (layer, 0, 0)),
            pl.BlockSpec((None, 1, v_dim), lambda b, h: (layer, 0, 0)),
        ],
        out_specs=pl.BlockSpec((seq, hps * v_dim), lambda b, h: (b, h)),
        out_shape=jax.ShapeDtypeStruct((batch * seq, n_heads * v_dim), out_dtype),
        scratch_shapes=[
            pltpu.VMEM((hps, 2, seq, qk_w), BF16),
            pltpu.VMEM((hps, v_dim + ONES_ROWS, seq), BF16),
            pltpu.VMEM((2 * (seq // tk), tk, 2 * tq), F32),
            pltpu.VMEM((2, seq, 2 * tq), BF16),
        ],
        compiler_params=pltpu.CompilerParams(
            dimension_semantics=("parallel", "parallel"),
            vmem_limit_bytes=V7X_VMEM_BYTES - VMEM_UNSCOPED_BYTES),
        name="diff_attn",
    )(qkv, qkv, qkv, lamv, g)


def _residual_layernorm(acc_ref, xres_ref, g, b, o32_ref, o16_ref, *, alpha, r0, n_rows, rows):
    for c in range(n_rows // rows):
        lo = r0 + c * rows
        y = alpha * xres_ref[lo:lo + rows, :] + acc_ref[lo:lo + rows, :]
        mu = jnp.mean(y, axis=-1, keepdims=True)
        d = y - mu
        var = jnp.mean(d * d, axis=-1, keepdims=True)
        out = d * lax.rsqrt(var + LN_EPS) * g + b
        o32_ref[lo:lo + rows, :] = out
        o16_ref[lo:lo + rows, :] = out.astype(o16_ref.dtype)


def _outproj_kernel(rg_ref, at_ref, w32_ref, xres_ref, g_ref, b_ref, o32_ref, o16_ref, acc_ref, w_ref,
                    *, alpha, sub):
    tm, half = rg_ref.shape

    @pl.when(pl.program_id(0) == 0)
    def _():
        w_ref[...] = w32_ref[...].astype(w_ref.dtype)

    g = g_ref[...]
    b = b_ref[...]
    for t in range(tm // sub):
        r0 = t * sub
        acc_ref[r0:r0 + sub, :] = (
            jnp.dot(rg_ref[r0:r0 + sub, :], w_ref[0:half, :], preferred_element_type=F32)
            + jnp.dot(at_ref[r0:r0 + sub, :], w_ref[half:, :], preferred_element_type=F32))
        _residual_layernorm(acc_ref, xres_ref, g, b, o32_ref, o16_ref, alpha=alpha, r0=r0, n_rows=sub,
                            rows=LN_ROWS)


def _outproj_ln(rg_out, at_out, w, layer, xres, g, b, *, alpha, tm, sub):
    T, half = rg_out.shape
    _, Kw, D = w.shape
    est = 2 * (2 * tm * half * 2 + tm * D * 4 + tm * D * 4 + tm * D * 2) + Kw * D * (4 + 2) \
        + 2 * tm * D * 4
    row = lambda width: pl.BlockSpec((tm, width), lambda m: (m, 0))
    full = lambda a: pl.BlockSpec((None,) + a.shape[1:], lambda m: (layer, 0, 0))
    return pl.pallas_call(
        functools.partial(_outproj_kernel, alpha=alpha, sub=sub),
        grid=(T // tm,),
        in_specs=[row(half), row(half),
                  pl.BlockSpec((None, Kw, D), lambda m: (layer, 0, 0), pipeline_mode=pl.Buffered(1)),
                  row(D), full(g), full(b)],
        out_specs=[row(D), row(D)],
        out_shape=[jax.ShapeDtypeStruct((T, D), F32), jax.ShapeDtypeStruct((T, D), BF16)],
        scratch_shapes=[pltpu.VMEM((tm, D), F32), pltpu.VMEM((Kw, D), BF16)],
        compiler_params=pltpu.CompilerParams(
            dimension_semantics=("arbitrary",), vmem_limit_bytes=_vmem_limit(est)),
        name="outproj_ln",
    )(rg_out, at_out, w, xres, g, b)


def _ffn_up_kernel(x_ref, wg32_ref, wu32_ref, cwg_ref, cwu_ref, cbg_ref, cbu_ref, o_ref, pg_ref, pu_ref,
                   wg_ref, wu_ref, *, tiles_per_seq, rows):
    tm, tn = o_ref.shape
    pad = V7X_SUBLANES
    m = pl.program_id(1)

    @pl.when(m == 0)
    def _():
        wg_ref[...] = wg32_ref[...].astype(wg_ref.dtype)
        wu_ref[...] = wu32_ref[...].astype(wu_ref.dtype)

    @pl.when(m % tiles_per_seq == 0)
    def _():
        pg_ref[:, 0:pad, :] = jnp.zeros((pg_ref.shape[0], pad, V7X_LANES), F32)
        pu_ref[:, 0:pad, :] = jnp.zeros((pu_ref.shape[0], pad, V7X_LANES), F32)

    cwg = cwg_ref[...]
    cbg = cbg_ref[...]
    cwu = 0.5 * cwu_ref[...]
    cbu = 0.5 * cbu_ref[...]
    for c in range(tm // rows):
        r0 = c * rows
        xs = x_ref[r0:r0 + rows, :]
        _store_slabs(pg_ref, r0, jnp.dot(xs, wg_ref[...], preferred_element_type=F32))
        _store_slabs(pu_ref, r0, jnp.dot(xs, wu_ref[...], preferred_element_type=F32))
        gv = _causal_conv_rows(pg_ref, cwg, cbg, r0, rows)
        uv = _causal_conv_rows(pu_ref, cwu, cbu, r0, rows)
        o_ref[r0:r0 + rows, :] = _gelu_tanh_times(gv, uv).astype(o_ref.dtype)
    pg_ref[:, 0:pad, :] = pg_ref[:, tm:tm + pad, :]
    pu_ref[:, 0:pad, :] = pu_ref[:, tm:tm + pad, :]


def _ffn_up(x, w, layer, cw, cb, *, seq, d_ff, tm, tn, out_dtype):
    T, K = x.shape
    nt = d_ff // tn
    conv_k = cw.shape[1]
    assert seq % tm == 0 and d_ff % tn == 0
    est = 2 * (tm * K * 2 + 2 * K * tn * 4 + tm * tn * 2) + 2 * K * tn * 2 + 2 * (tm + 8) * tn * 4 \
        + 2 * tm * tn * 4
    vecg = lambda r: pl.BlockSpec((None, r, tn), lambda n, m: (layer, 0, n))
    vecu = lambda r: pl.BlockSpec((None, r, tn), lambda n, m: (layer, 0, n + nt))
    return pl.pallas_call(
        functools.partial(_ffn_up_kernel, tiles_per_seq=seq // tm, rows=min(PROJ_SUB_ROWS, tm)),
        grid=(nt, T // tm),
        in_specs=[
            pl.BlockSpec((tm, K), lambda n, m: (m, 0)),
            pl.BlockSpec((None, K, tn), lambda n, m: (layer, 0, n)),
            pl.BlockSpec((None, K, tn), lambda n, m: (layer, 0, n + nt)),
            vecg(conv_k), vecu(conv_k), vecg(1), vecu(1),
        ],
        out_specs=pl.BlockSpec((tm, tn), lambda n, m: (m, n)),
        out_shape=jax.ShapeDtypeStruct((T, d_ff), out_dtype),
        scratch_shapes=[pltpu.VMEM((tn // V7X_LANES, tm + V7X_SUBLANES, V7X_LANES), F32),
                        pltpu.VMEM((tn // V7X_LANES, tm + V7X_SUBLANES, V7X_LANES), F32),
                        pltpu.VMEM((K, tn), BF16), pltpu.VMEM((K, tn), BF16)],
        compiler_params=pltpu.CompilerParams(
            dimension_semantics=("parallel", "arbitrary"), vmem_limit_bytes=_vmem_limit(est)),
        name="ffn_up",
    )(x, w, w, cw, cw, cb, cb)


def _ffn_down_kernel(h_ref, w32_ref, xres_ref, g_ref, b_ref, o32_ref, o16_ref, acc_ref, w_ref, *, alpha, n_pre):
    s = pl.program_id(0)
    kc = w32_ref.shape[0]

    @pl.when(s < n_pre)
    def _():
        w_ref[pl.ds(pl.multiple_of(s * kc, kc), kc), :] = w32_ref[...].astype(w_ref.dtype)

    @pl.when(s >= n_pre)
    def _():
        tm = h_ref.shape[0]
        acc_ref[...] = jnp.dot(h_ref[...], w_ref[...], preferred_element_type=F32)
        _residual_layernorm(acc_ref, xres_ref, g_ref[...], b_ref[...], o32_ref, o16_ref, alpha=alpha, r0=0,
                            n_rows=tm, rows=LN_ROWS)


def _ffn_down_ln(h, w, layer, xres, g, b, *, alpha, tm, kc):
    T, K = h.shape
    D = w.shape[2]
    assert T % tm == 0 and K % kc == 0
    n_pre = K // kc
    est = 2 * (tm * K * 2 + kc * D * 4 + tm * D * 4 + tm * D * 4 + tm * D * 2) + K * D * 2 + 2 * tm * D * 4
    tile = lambda s: jnp.maximum(s - n_pre, 0)
    row = lambda width: pl.BlockSpec((tm, width), lambda s: (tile(s), 0))
    full = lambda a: pl.BlockSpec((None,) + a.shape[1:], lambda s: (layer, 0, 0))
    return pl.pallas_call(
        functools.partial(_ffn_down_kernel, alpha=alpha, n_pre=n_pre),
        grid=(n_pre + T // tm,),
        in_specs=[row(K),
                  pl.BlockSpec((None, kc, D), lambda s: (layer, jnp.minimum(s, n_pre - 1), 0)),
                  row(D), full(g), full(b)],
        out_specs=[row(D), row(D)],
        out_shape=[jax.ShapeDtypeStruct((T, D), F32), jax.ShapeDtypeStruct((T, D), BF16)],
        scratch_shapes=[pltpu.VMEM((tm, D), F32), pltpu.VMEM((K, D), BF16)],
        compiler_params=pltpu.CompilerParams(
            dimension_semantics=("arbitrary",), vmem_limit_bytes=_vmem_limit(est)),
        name="ffn_down_ln",
    )(h, w, xres, g, b)


def _block_diag(w, tile):
    G, d, _ = w.shape
    per = tile // d
    w4 = w.reshape(G // per, per, d, d)
    eye = jnp.eye(per, dtype=w.dtype)
    return jnp.einsum('jipq,ik->jipkq', w4, eye).reshape(G // per, tile, tile)


def kernel(x, w_in, rg_conv_w, rg_conv_b, rg_gate_a_w, rg_gate_a_b, rg_gate_x_w, rg_gate_x_b, rg_lambda,
           lam_q1, lam_k1, lam_q2, lam_k2, subln_g, w_out, ln_mix_g, ln_mix_b, w_up, ffn_conv_w,
           ffn_conv_b, w_down, ln_ffn_g, ln_ffn_b):
    B, S, D = x.shape
    depth = w_in.shape[0]
    rg_w = rg_conv_w.shape[2]
    head_dim = lam_q1.shape[1]
    v_dim = subln_g.shape[1]
    mix_w = w_out.shape[1]
    attn_w = mix_w - rg_w
    n_heads = attn_w // v_dim
    qkv_w = w_in.shape[2] - 2 * rg_w
    d_ff = w_down.shape[1]
    T = B * S
    alpha = (2.0 * depth) ** 0.25

    vec3 = lambda a: a.reshape(depth, 1, a.shape[-1])
    n_blocks, blk = rg_gate_a_w.shape[1], rg_gate_a_w.shape[2]
    gate_tiles = lambda w: _block_diag(w.reshape(depth * n_blocks, blk, blk), V7X_MXU_DIM)
    wg = jnp.concatenate([gate_tiles(rg_gate_a_w), gate_tiles(rg_gate_x_w)], axis=-1).astype(BF16)
    wg = wg.reshape(depth, rg_w // V7X_MXU_DIM, V7X_MXU_DIM, 2 * V7X_MXU_DIM)
    lamv = jnp.stack([lam_q1, lam_k1, lam_q2, lam_k2], axis=1)
    rg_cb, rg_ba, rg_bx, rg_lam = vec3(rg_conv_b), vec3(rg_gate_a_b), vec3(rg_gate_x_b), vec3(rg_lambda)
    sub_g, mix_g, mix_b = vec3(subln_g), vec3(ln_mix_g), vec3(ln_mix_b)
    ffn_cb, ffn_g, ffn_b = vec3(ffn_conv_b), vec3(ln_ffn_g), vec3(ln_ffn_b)

    x32 = x.reshape(T, D)
    x16 = x32.astype(BF16)
    for i in range(depth):
        lam_init = 0.8 - 0.6 * math.exp(-0.3 * i)
        rg_out, qkv = _inproj_rglru(x16, w_in, i, rg_conv_w, rg_cb, wg, rg_ba, rg_bx, rg_lam,
                                    batch=B, seq=S, width=rg_w, qkv_width=qkv_w, out_dtype=BF16)
        at_out = _diff_attn(qkv, lamv, sub_g, i, batch=B, seq=S, n_heads=n_heads,
                            head_dim=head_dim, v_dim=v_dim, lam_init=lam_init, out_dtype=BF16,
                            tq=min(512, S), tk=min(512, S), hps=2)
        x32, x16 = _outproj_ln(rg_out, at_out, w_out, i, x32, mix_g, mix_b, alpha=alpha, tm=512, sub=256)
        hmid = _ffn_up(x16, w_up, i, ffn_conv_w, ffn_cb, seq=S, d_ff=d_ff, tm=S, tn=512, out_dtype=BF16)
        x32, x16 = _ffn_down_ln(hmid, w_down, i, x32, ffn_g, ffn_b, alpha=alpha, tm=256, kc=512)
    return x32.reshape(B, S, D)
```

```python
import functools
import math

import jax
import jax.numpy as jnp
from jax import lax
from jax.experimental import pallas as pl
from jax.experimental.pallas import tpu as pltpu

F32 = jnp.float32
BF16 = jnp.bfloat16

V7X_LANES = 128
V7X_SUBLANES = 8
V7X_MXU_DIM = 256
V7X_VMEM_BYTES = 64 * 1024 * 1024
VMEM_TEMP_BYTES = 8 * 1024 * 1024
VMEM_UNSCOPED_BYTES = 4 * 1024 * 1024
VMEM_MIN_REQUEST_BYTES = 16 * 1024 * 1024

RG_CHUNK_ROWS = 128
PROJ_SUB_ROWS = 256
LN_ROWS = 64

RG_C = 8.0
LN_EPS = 1e-5
RMS_EPS = 1e-5
MASK_VALUE = -1e30


def _vmem_limit(nbytes):
    return int(min(V7X_VMEM_BYTES - VMEM_UNSCOPED_BYTES, max(nbytes + VMEM_TEMP_BYTES, VMEM_MIN_REQUEST_BYTES)))


def _gelu_tanh_times(x, half_y):
    c = math.sqrt(2.0 / math.pi)
    t = jnp.tanh(x * (c + (c * 0.044715) * (x * x)))
    return (x + x * t) * half_y


def _causal_conv_rows(pad_ref, cw, cb, r0, rows):
    conv_k = cw.shape[0]
    pad = V7X_SUBLANES
    strips = []
    for s in range(pad_ref.shape[0]):
        lanes = slice(s * V7X_LANES, (s + 1) * V7X_LANES)
        out = cb[:, lanes] + cw[conv_k - 1:conv_k, lanes] * pad_ref[s, pl.ds(r0 + pad, rows), :]
        for k in range(conv_k - 1):
            start = r0 + pad - (conv_k - 1) + k
            out = out + cw[k:k + 1, lanes] * pad_ref[s, pl.ds(start, rows, stride=1), :]
        strips.append(out)
    return strips[0] if len(strips) == 1 else jnp.concatenate(strips, axis=1)


def _store_slabs(pad_ref, r0, value):
    rows = value.shape[0]
    for s in range(pad_ref.shape[0]):
        pad_ref[s, V7X_SUBLANES + r0:V7X_SUBLANES + r0 + rows, :] = value[:, s * V7X_LANES:(s + 1) * V7X_LANES]


def _inproj_rglru_kernel(x16_ref, wx32_ref, wgt32_ref, wq32a_ref, wq32b_ref, wq32c_ref, cw_ref, cb_ref, wg_ref,
                         ba_ref, bx_ref, lam_ref, o_ref, qkv_ref, xpad_ref, gate_ref, hs_ref,
                         wx_ref, wgt_ref, wq_ref, *, rows, sub_rows):
    S, C = o_ref.shape
    pad = V7X_SUBLANES
    xpad_ref[:, 0:pad, :] = jnp.zeros((xpad_ref.shape[0], pad, V7X_LANES), F32)
    wx_ref[...] = wx32_ref[...].astype(wx_ref.dtype)
    wgt_ref[...] = wgt32_ref[...].astype(wgt_ref.dtype)
    for part, ref in enumerate((wq32a_ref, wq32b_ref, wq32c_ref)):
        wq_ref[:, part * C:(part + 1) * C] = ref[...].astype(wq_ref.dtype)

    lam = lam_ref[...]
    log_sig = jnp.minimum(lam, 0.0) - jnp.log1p(jnp.exp(-jnp.abs(lam)))
    cw = cw_ref[...]
    cb = cb_ref[...]
    ba = ba_ref[...]
    bx = bx_ref[...]
    wg = wg_ref[0]
    n_groups = rows // V7X_SUBLANES
    sub3 = lax.broadcasted_iota(jnp.int32, (n_groups, V7X_SUBLANES, C), 1)
    n_sub = S // sub_rows

    def chunk(t0, h_prev):
        u = _causal_conv_rows(xpad_ref, cw, cb, t0, rows)
        pre = jnp.dot(u.astype(BF16), wg, preferred_element_type=F32)
        r = jax.nn.sigmoid(pre[:, :C] + ba)
        ig = jax.nn.sigmoid(pre[:, C:] + bx)
        log_a = RG_C * r * log_sig
        a = jnp.exp(log_a)
        th = jnp.tanh(log_a)
        m2 = -2.0 * th / (1.0 - th)
        mult = jnp.where(m2 > 0.0, m2 * lax.rsqrt(m2), 0.0)
        b = mult * (ig * u)
        a = a.reshape(n_groups, V7X_SUBLANES, C)
        b = b.reshape(n_groups, V7X_SUBLANES, C)
        for s in (1, 2, 4):
            keep = sub3 >= s
            a_sh = jnp.where(keep, pltpu.roll(a, s, axis=1), 1.0)
            b_sh = jnp.where(keep, pltpu.roll(b, s, axis=1), 0.0)
            b = a * b_sh + b
            a = a * a_sh
        a = a.reshape(rows, C)
        b = b.reshape(rows, C)
        h = h_prev
        for g in range(n_groups):
            lo = g * V7X_SUBLANES
            hg = a[lo:lo + V7X_SUBLANES, :] * h + b[lo:lo + V7X_SUBLANES, :]
            h = hg[V7X_SUBLANES - 1:V7X_SUBLANES, :]
            hs_ref[t0 + lo:t0 + lo + V7X_SUBLANES, :] = hg
        gate = gate_ref[t0:t0 + rows, :]
        o_ref[t0:t0 + rows, :] = _gelu_tanh_times(gate, 0.5 * hs_ref[t0:t0 + rows, :]).astype(o_ref.dtype)
        return h

    h = jnp.zeros((1, C), F32)
    for si in range(n_sub):
        r0 = si * sub_rows
        xs = x16_ref[r0:r0 + sub_rows, :]
        _store_slabs(xpad_ref, r0, jnp.dot(xs, wx_ref[...], preferred_element_type=F32))
        gate_ref[r0:r0 + sub_rows, :] = jnp.dot(xs, wgt_ref[...], preferred_element_type=F32)
        qkv_ref[r0:r0 + sub_rows, :] = jnp.dot(xs, wq_ref[...],
                                               preferred_element_type=F32).astype(qkv_ref.dtype)
        for ci in range(sub_rows // rows):
            h = chunk(r0 + ci * rows, h)


def _inproj_rglru(x16, w_in, layer, cw, cb, wg, ba, bx, lam, *, batch, seq, width, qkv_width, out_dtype):
    C = wg.shape[2]
    nct = width // C
    T, D = x16.shape
    nqc = qkv_width // nct
    q_parts = nqc // C
    q_off = 2 * width // C
    conv_k = cw.shape[1]
    assert qkv_width % nct == 0 and nqc == 3 * C
    vec = lambda: pl.BlockSpec((None, 1, C), lambda b, c: (layer, 0, c))
    wq_spec = lambda part: pl.BlockSpec((None, D, C), lambda b, c: (layer, 0, q_off + q_parts * c + part))
    est = 2 * (seq * D * 2 + 2 * D * C * 4 + D * nqc * 4 + seq * C * 2 + seq * nqc * 2 + C * 2 * C * 2) \
        + 2 * D * C * 2 + D * nqc * 2 + (3 * seq + 8) * C * 4 + PROJ_SUB_ROWS * nqc * 4
    return pl.pallas_call(
        functools.partial(_inproj_rglru_kernel, rows=RG_CHUNK_ROWS, sub_rows=PROJ_SUB_ROWS),
        grid=(batch, nct),
        in_specs=[
            pl.BlockSpec((seq, D), lambda b, c: (b, 0)),
            pl.BlockSpec((None, D, C), lambda b, c: (layer, 0, c)),
            pl.BlockSpec((None, D, C), lambda b, c: (layer, 0, nct + c)),
            wq_spec(0), wq_spec(1), wq_spec(2),
            pl.BlockSpec((None, conv_k, C), lambda b, c: (layer, 0, c)),
            vec(),
            pl.BlockSpec((None, 1, C, 2 * C), lambda b, c: (layer, c, 0, 0)),
            vec(), vec(), vec(),
        ],
        out_specs=[pl.BlockSpec((seq, C), lambda b, c: (b, c)),
                   pl.BlockSpec((seq, nqc), lambda b, c: (b, c))],
        out_shape=[jax.ShapeDtypeStruct((batch * seq, width), out_dtype),
                   jax.ShapeDtypeStruct((T, qkv_width), BF16)],
        scratch_shapes=[pltpu.VMEM((C // V7X_LANES, seq + V7X_SUBLANES, V7X_LANES), F32),
                        pltpu.VMEM((seq, C), F32), pltpu.VMEM((seq, C), F32),
                        pltpu.VMEM((D, C), BF16), pltpu.VMEM((D, C), BF16), pltpu.VMEM((D, nqc), BF16)],
        compiler_params=pltpu.CompilerParams(
            dimension_semantics=("parallel", "parallel"), vmem_limit_bytes=_vmem_limit(est)),
        name="inproj_rglru",
    )(x16, w_in, w_in, w_in, w_in, w_in, cw, cb, wg, ba, bx, lam)


POS_SPLIT = 256
ONES_ROWS = 16


def _attn_kernel(q_ref, k_ref, v_ref, lamv_ref, g_ref, o_ref, ka_ref, vt_ref, s_ref, p_ref,
                 *, tq, tk, n_heads, head_dim, scale, lam_init):
    h = pl.program_id(1)
    S, d2 = q_ref.shape
    V = v_ref.shape[1]
    n_kv = S // tk
    n_q = S // tq

    def extra_cols(shape, c, lo_val, hi_val):
        lane = lax.broadcasted_iota(jnp.int32, shape, 1)
        base = head_dim * (1 - c)
        return jnp.where(lane == base, lo_val, jnp.where(lane == base + 1, hi_val, 0.0))

    def own_half(shape, c):
        lane = lax.broadcasted_iota(jnp.int32, shape, 1)
        return (lane < head_dim) if c == 0 else (lane >= head_dim)

    for jb in range(n_kv):
        rows = slice(jb * tk, (jb + 1) * tk)
        vt_ref[0:V, rows] = v_ref[rows, :].astype(F32).T.astype(vt_ref.dtype)
        vt_ref[V:V + ONES_ROWS, rows] = jnp.ones((ONES_ROWS, tk), vt_ref.dtype)
        kb = k_ref[rows, :]
        pos = lax.broadcasted_iota(jnp.int32, (tk, d2), 0) + jb * tk
        j_lo = (pos % POS_SPLIT).astype(F32)
        j_hi = (pos // POS_SPLIT).astype(F32)
        for c in range(2):
            ka_ref[c, rows, :] = jnp.where(own_half((tk, d2), c), kb,
                                           extra_cols((tk, d2), c, j_lo, j_hi).astype(kb.dtype))

    slope = jnp.exp2(jnp.full((1, 1), -8.0 / n_heads, F32) * (h + 1).astype(F32))
    lv = lamv_ref[...]
    lam = (jnp.exp(jnp.sum(lv[0:1, :] * lv[1:2, :], axis=-1, keepdims=True))
           - jnp.exp(jnp.sum(lv[2:3, :] * lv[3:4, :], axis=-1, keepdims=True)) + lam_init)
    gain = g_ref[...] * (1.0 - lam_init)

    q_aug = {}

    def augmented_queries(qi):
        if qi not in q_aug:
            q = q_ref[qi * tq:(qi + 1) * tq, :] * jnp.asarray(scale, q_ref.dtype)
            q_aug[qi] = [jnp.where(own_half(q.shape, c), q,
                                   extra_cols(q.shape, c, slope, slope * POS_SPLIT).astype(q.dtype))
                         for c in range(2)]
        return q_aug[qi]

    def on_diagonal(qi, j):
        return tq == tk and j == qi and tk % 2 == 0

    def scores(qi, j, tile):
        qa = augmented_queries(qi)
        half_k = tk // 2
        half_q = tq // 2
        for c in range(2):
            keys = ka_ref[c, j * tk:(j + 1) * tk, :]
            if on_diagonal(qi, j):
                s_ref[tile, 0:half_k, c * tq:(c + 1) * tq] = lax.dot_general(
                    keys[0:half_k], qa[c], (((1,), (1,)), ((), ())), preferred_element_type=F32)
                s_ref[tile, half_k:tk, c * tq:c * tq + half_q] = jnp.full((tk - half_k, half_q), MASK_VALUE, F32)
                s_ref[tile, half_k:tk, c * tq + half_q:(c + 1) * tq] = lax.dot_general(
                    keys[half_k:tk], qa[c][half_q:tq], (((1,), (1,)), ((), ())), preferred_element_type=F32)
            else:
                s_ref[tile, :, c * tq:(c + 1) * tq] = lax.dot_general(
                    keys, qa[c], (((1,), (1,)), ((), ())), preferred_element_type=F32)

    def n_tiles(qi):
        return ((qi + 1) * tq + tk - 1) // tk

    def all_scores(qi):
        for j in range(n_tiles(qi)):
            scores(qi, j, (qi % 2) * n_kv + j)

    def masked_scores(qi, j):
        s = s_ref[(qi % 2) * n_kv + j]
        if (j + 1) * tk - 1 > qi * tq:
            key_pos = lax.broadcasted_iota(jnp.int32, (tk, 2 * tq), 0) + j * tk
            col = lax.broadcasted_iota(jnp.int32, (tk, 2 * tq), 1)
            query_pos = jnp.where(col >= tq, col - tq, col) + qi * tq
            s = jnp.where(key_pos <= query_pos, s, MASK_VALUE)
        return s

    def softmax_weights(qi):
        tiles = [masked_scores(qi, j) for j in range(n_tiles(qi))]
        m = functools.reduce(jnp.maximum, [jnp.max(s, axis=0, keepdims=True) for s in tiles])
        for j, s in enumerate(tiles):
            p_ref[qi % 2, j * tk:(j + 1) * tk, :] = jnp.exp(s - m).astype(p_ref.dtype)

    def weighted_values(qi):
        k_end = n_tiles(qi) * tk
        acc = jnp.dot(vt_ref[:, 0:k_end], p_ref[qi % 2, 0:k_end, :], preferred_element_type=F32)
        ot = (acc[0:V, 0:tq] / acc[V:V + 1, 0:tq]
              - lam * (acc[0:V, tq:2 * tq] / acc[V:V + 1, tq:2 * tq]))
        ms = jnp.mean(ot * ot, axis=0, keepdims=True)
        o = (ot * lax.rsqrt(ms + RMS_EPS)).T * gain
        o_ref[qi * tq:(qi + 1) * tq, :] = o.astype(o_ref.dtype)

    all_scores(0)
    for qi in range(n_q):
        if qi + 1 < n_q:
            all_scores(qi + 1)
        softmax_weights(qi)
        if qi >= 1:
            weighted_values(qi - 1)
    weighted_values(n_q - 1)


def _diff_attn(qkv, lamv, g, layer, *, batch, seq, n_heads, head_dim, v_dim, lam_init, out_dtype, tq, tk):
    qk_w = 2 * head_dim
    assert qk_w == v_dim and seq % tq == 0 and seq % tk == 0 and seq % POS_SPLIT == 0
    assert seq // POS_SPLIT <= POS_SPLIT
    return pl.pallas_call(
        functools.partial(_attn_kernel, tq=tq, tk=tk, n_heads=n_heads, head_dim=head_dim,
                          scale=head_dim ** -0.5, lam_init=lam_init),
        grid=(batch, n_heads),
        in_specs=[
            pl.BlockSpec((seq, qk_w), lambda b, h: (b, h)),
            pl.BlockSpec((seq, qk_w), lambda b, h: (b, n_heads + h)),
            pl.BlockSpec((seq, v_dim), lambda b, h: (b, 2 * n_heads + h)),
            pl.BlockSpec((None,) + lamv.shape[1:], lambda b, h: (layer, 0, 0)),
            pl.BlockSpec((None, 1, v_dim), lambda b, h: (layer, 0, 0)),
        ],
        out_specs=pl.BlockSpec((seq, v_dim), lambda b, h: (b, h)),
        out_shape=jax.ShapeDtypeStruct((batch * seq, n_heads * v_dim), out_dtype),
        scratch_shapes=[
            pltpu.VMEM((2, seq, qk_w), BF16),
            pltpu.VMEM((v_dim + ONES_ROWS, seq), BF16),
            pltpu.VMEM((2 * (seq // tk), tk, 2 * tq), F32),
            pltpu.VMEM((2, seq, 2 * tq), BF16),
        ],
        compiler_params=pltpu.CompilerParams(
            dimension_semantics=("parallel", "parallel"),
            vmem_limit_bytes=V7X_VMEM_BYTES - VMEM_UNSCOPED_BYTES),
        name="diff_attn",
    )(qkv, qkv, qkv, lamv, g)


def _residual_layernorm(acc_ref, xres_ref, g, b, o32_ref, o16_ref, *, alpha, r0, n_rows, rows):
    for c in range(n_rows // rows):
        lo = r0 + c * rows
        y = alpha * xres_ref[lo:lo + rows, :] + acc_ref[lo:lo + rows, :]
        mu = jnp.mean(y, axis=-1, keepdims=True)
        d = y - mu
        var = jnp.mean(d * d, axis=-1, keepdims=True)
        out = d * lax.rsqrt(var + LN_EPS) * g + b
        o32_ref[lo:lo + rows, :] = out
        o16_ref[lo:lo + rows, :] = out.astype(o16_ref.dtype)


def _outproj_kernel(rg_ref, at_ref, w32_ref, xres_ref, g_ref, b_ref, o32_ref, o16_ref, acc_ref, w_ref,
                    *, alpha, sub):
    tm, half = rg_ref.shape

    @pl.when(pl.program_id(0) == 0)
    def _():
        w_ref[...] = w32_ref[...].astype(w_ref.dtype)

    g = g_ref[...]
    b = b_ref[...]
    for t in range(tm // sub):
        r0 = t * sub
        acc_ref[r0:r0 + sub, :] = (
            jnp.dot(rg_ref[r0:r0 + sub, :], w_ref[0:half, :], preferred_element_type=F32)
            + jnp.dot(at_ref[r0:r0 + sub, :], w_ref[half:, :], preferred_element_type=F32))
        _residual_layernorm(acc_ref, xres_ref, g, b, o32_ref, o16_ref, alpha=alpha, r0=r0, n_rows=sub,
                            rows=LN_ROWS)


def _outproj_ln(rg_out, at_out, w, layer, xres, g, b, *, alpha, tm, sub):
    T, half = rg_out.shape
    _, Kw, D = w.shape
    est = 2 * (2 * tm * half * 2 + tm * D * 4 + tm * D * 4 + tm * D * 2) + Kw * D * (4 + 2) \
        + 2 * tm * D * 4
    row = lambda width: pl.BlockSpec((tm, width), lambda m: (m, 0))
    full = lambda a: pl.BlockSpec((None,) + a.shape[1:], lambda m: (layer, 0, 0))
    return pl.pallas_call(
        functools.partial(_outproj_kernel, alpha=alpha, sub=sub),
        grid=(T // tm,),
        in_specs=[row(half), row(half),
                  pl.BlockSpec((None, Kw, D), lambda m: (layer, 0, 0), pipeline_mode=pl.Buffered(1)),
                  row(D), full(g), full(b)],
        out_specs=[row(D), row(D)],
        out_shape=[jax.ShapeDtypeStruct((T, D), F32), jax.ShapeDtypeStruct((T, D), BF16)],
        scratch_shapes=[pltpu.VMEM((tm, D), F32), pltpu.VMEM((Kw, D), BF16)],
        compiler_params=pltpu.CompilerParams(
            dimension_semantics=("arbitrary",), vmem_limit_bytes=_vmem_limit(est)),
        name="outproj_ln",
    )(rg_out, at_out, w, xres, g, b)


def _ffn_up_kernel(x_ref, wg32_ref, wu32_ref, cwg_ref, cwu_ref, cbg_ref, cbu_ref, o_ref, pg_ref, pu_ref,
                   wg_ref, wu_ref, *, tiles_per_seq, rows):
    tm, tn = o_ref.shape
    pad = V7X_SUBLANES
    m = pl.program_id(1)

    @pl.when(m == 0)
    def _():
        wg_ref[...] = wg32_ref[...].astype(wg_ref.dtype)
        wu_ref[...] = wu32_ref[...].astype(wu_ref.dtype)

    @pl.when(m % tiles_per_seq == 0)
    def _():
        pg_ref[:, 0:pad, :] = jnp.zeros((pg_ref.shape[0], pad, V7X_LANES), F32)
        pu_ref[:, 0:pad, :] = jnp.zeros((pu_ref.shape[0], pad, V7X_LANES), F32)

    cwg = cwg_ref[...]
    cbg = cbg_ref[...]
    cwu = 0.5 * cwu_ref[...]
    cbu = 0.5 * cbu_ref[...]
    for c in range(tm // rows):
        r0 = c * rows
        xs = x_ref[r0:r0 + rows, :]
        _store_slabs(pg_ref, r0, jnp.dot(xs, wg_ref[...], preferred_element_type=F32))
        _store_slabs(pu_ref, r0, jnp.dot(xs, wu_ref[...], preferred_element_type=F32))
        gv = _causal_conv_rows(pg_ref, cwg, cbg, r0, rows)
        uv = _causal_conv_rows(pu_ref, cwu, cbu, r0, rows)
        o_ref[r0:r0 + rows, :] = _gelu_tanh_times(gv, uv).astype(o_ref.dtype)
    pg_ref[:, 0:pad, :] = pg_ref[:, tm:tm + pad, :]
    pu_ref[:, 0:pad, :] = pu_ref[:, tm:tm + pad, :]


def _ffn_up(x, w, layer, cw, cb, *, seq, d_ff, tm, tn, out_dtype):
    T, K = x.shape
    nt = d_ff // tn
    conv_k = cw.shape[1]
    assert seq % tm == 0 and d_ff % tn == 0
    est = 2 * (tm * K * 2 + 2 * K * tn * 4 + tm * tn * 2) + 2 * K * tn * 2 + 2 * (tm + 8) * tn * 4 \
        + 2 * tm * tn * 4
    vecg = lambda r: pl.BlockSpec((None, r, tn), lambda n, m: (layer, 0, n))
    vecu = lambda r: pl.BlockSpec((None, r, tn), lambda n, m: (layer, 0, n + nt))
    return pl.pallas_call(
        functools.partial(_ffn_up_kernel, tiles_per_seq=seq // tm, rows=min(PROJ_SUB_ROWS, tm)),
        grid=(nt, T // tm),
        in_specs=[
            pl.BlockSpec((tm, K), lambda n, m: (m, 0)),
            pl.BlockSpec((None, K, tn), lambda n, m: (layer, 0, n)),
            pl.BlockSpec((None, K, tn), lambda n, m: (layer, 0, n + nt)),
            vecg(conv_k), vecu(conv_k), vecg(1), vecu(1),
        ],
        out_specs=pl.BlockSpec((tm, tn), lambda n, m: (m, n)),
        out_shape=jax.ShapeDtypeStruct((T, d_ff), out_dtype),
        scratch_shapes=[pltpu.VMEM((tn // V7X_LANES, tm + V7X_SUBLANES, V7X_LANES), F32),
                        pltpu.VMEM((tn // V7X_LANES, tm + V7X_SUBLANES, V7X_LANES), F32),
                        pltpu.VMEM((K, tn), BF16), pltpu.VMEM((K, tn), BF16)],
        compiler_params=pltpu.CompilerParams(
            dimension_semantics=("parallel", "arbitrary"), vmem_limit_bytes=_vmem_limit(est)),
        name="ffn_up",
    )(x, w, w, cw, cw, cb, cb)


def _ffn_down_kernel(h_ref, w32_ref, xres_ref, g_ref, b_ref, o32_ref, o16_ref, acc_ref, w_ref,
                     *, alpha, n_pre, n_tiles):
    s = pl.program_id(0)
    kc = w32_ref.shape[0]
    tm = h_ref.shape[0]
    j = s - n_pre

    @pl.when(s < n_pre)
    def _():
        w_ref[pl.ds(pl.multiple_of(s * kc, kc), kc), :] = w32_ref[...].astype(w_ref.dtype)

    def project(bank):
        acc_ref[bank] = jnp.dot(h_ref[...], w_ref[...], preferred_element_type=F32)

    def normalise(bank):
        _residual_layernorm(acc_ref.at[bank], xres_ref, g_ref[...], b_ref[...], o32_ref, o16_ref, alpha=alpha,
                            r0=0, n_rows=tm, rows=LN_ROWS)

    @pl.when(j == 0)
    def _():
        project(0)

    for parity in range(2):
        @pl.when(jnp.logical_and(jnp.logical_and(j >= 1, j < n_tiles), j % 2 == parity))
        def _():
            project(parity)
            normalise(1 - parity)

    @pl.when(j == n_tiles)
    def _():
        normalise((n_tiles - 1) % 2)


def _ffn_down_ln(h, w, layer, xres, g, b, *, alpha, tm, kc):
    T, K = h.shape
    D = w.shape[2]
    assert T % tm == 0 and K % kc == 0
    n_pre = K // kc
    n_tiles = T // tm
    est = 2 * (tm * K * 2 + kc * D * 4 + tm * D * 4 + tm * D * 4 + tm * D * 2) + K * D * 2 + 3 * tm * D * 4
    cur = lambda width: pl.BlockSpec((tm, width), lambda s: (jnp.clip(s - n_pre, 0, n_tiles - 1), 0))
    lag = lambda width: pl.BlockSpec((tm, width), lambda s: (jnp.clip(s - n_pre - 1, 0, n_tiles - 1), 0))
    full = lambda a: pl.BlockSpec((None,) + a.shape[1:], lambda s: (layer, 0, 0))
    return pl.pallas_call(
        functools.partial(_ffn_down_kernel, alpha=alpha, n_pre=n_pre, n_tiles=n_tiles),
        grid=(n_pre + n_tiles + 1,),
        in_specs=[cur(K),
                  pl.BlockSpec((None, kc, D), lambda s: (layer, jnp.minimum(s, n_pre - 1), 0)),
                  lag(D), full(g), full(b)],
        out_specs=[lag(D), lag(D)],
        out_shape=[jax.ShapeDtypeStruct((T, D), F32), jax.ShapeDtypeStruct((T, D), BF16)],
        scratch_shapes=[pltpu.VMEM((2, tm, D), F32), pltpu.VMEM((K, D), BF16)],
        compiler_params=pltpu.CompilerParams(
            dimension_semantics=("arbitrary",), vmem_limit_bytes=_vmem_limit(est)),
        name="ffn_down_ln",
    )(h, w, xres, g, b)


def _block_diag(w, tile):
    G, d, _ = w.shape
    per = tile // d
    w4 = w.reshape(G // per, per, d, d)
    eye = jnp.eye(per, dtype=w.dtype)
    return jnp.einsum('jipq,ik->jipkq', w4, eye).reshape(G // per, tile, tile)


def kernel(x, w_in, rg_conv_w, rg_conv_b, rg_gate_a_w, rg_gate_a_b, rg_gate_x_w, rg_gate_x_b, rg_lambda,
           lam_q1, lam_k1, lam_q2, lam_k2, subln_g, w_out, ln_mix_g, ln_mix_b, w_up, ffn_conv_w,
           ffn_conv_b, w_down, ln_ffn_g, ln_ffn_b):
    B, S, D = x.shape
    depth = w_in.shape[0]
    rg_w = rg_conv_w.shape[2]
    head_dim = lam_q1.shape[1]
    v_dim = subln_g.shape[1]
    mix_w = w_out.shape[1]
    attn_w = mix_w - rg_w
    n_heads = attn_w // v_dim
    qkv_w = w_in.shape[2] - 2 * rg_w
    d_ff = w_down.shape[1]
    T = B * S
    alpha = (2.0 * depth) ** 0.25

    vec3 = lambda a: a.reshape(depth, 1, a.shape[-1])
    n_blocks, blk = rg_gate_a_w.shape[1], rg_gate_a_w.shape[2]
    gate_tiles = lambda w: _block_diag(w.reshape(depth * n_blocks, blk, blk), V7X_MXU_DIM)
    wg = jnp.concatenate([gate_tiles(rg_gate_a_w), gate_tiles(rg_gate_x_w)], axis=-1).astype(BF16)
    wg = wg.reshape(depth, rg_w // V7X_MXU_DIM, V7X_MXU_DIM, 2 * V7X_MXU_DIM)
    lamv = jnp.stack([lam_q1, lam_k1, lam_q2, lam_k2], axis=1)
    rg_cb, rg_ba, rg_bx, rg_lam = vec3(rg_conv_b), vec3(rg_gate_a_b), vec3(rg_gate_x_b), vec3(rg_lambda)
    sub_g, mix_g, mix_b = vec3(subln_g), vec3(ln_mix_g), vec3(ln_mix_b)
    ffn_cb, ffn_g, ffn_b = vec3(ffn_conv_b), vec3(ln_ffn_g), vec3(ln_ffn_b)

    x32 = x.reshape(T, D)
    x16 = x32.astype(BF16)
    for i in range(depth):
        lam_init = 0.8 - 0.6 * math.exp(-0.3 * i)
        rg_out, qkv = _inproj_rglru(x16, w_in, i, rg_conv_w, rg_cb, wg, rg_ba, rg_bx, rg_lam,
                                    batch=B, seq=S, width=rg_w, qkv_width=qkv_w, out_dtype=BF16)
        at_out = _diff_attn(qkv, lamv, sub_g, i, batch=B, seq=S, n_heads=n_heads,
                            head_dim=head_dim, v_dim=v_dim, lam_init=lam_init, out_dtype=BF16,
                            tq=min(512, S), tk=min(512, S))
        x32, x16 = _outproj_ln(rg_out, at_out, w_out, i, x32, mix_g, mix_b, alpha=alpha, tm=512, sub=256)
        hmid = _ffn_up(x16, w_up, i, ffn_conv_w, ffn_cb, seq=S, d_ff=d_ff, tm=S, tn=512, out_dtype=BF16)
        x32, x16 = _ffn_down_ln(hmid, w_down, i, x32, ffn_g, ffn_b, alpha=alpha, tm=256, kc=512)
    return x32.reshape(B, S, D)
```

```python
import functools
import math

import jax
import jax.numpy as jnp
from jax import lax
from jax.experimental import pallas as pl
from jax.experimental.pallas import tpu as pltpu

F32 = jnp.float32
BF16 = jnp.bfloat16

V7X_LANES = 128
V7X_SUBLANES = 8
V7X_MXU_DIM = 256
V7X_VMEM_BYTES = 64 * 1024 * 1024
VMEM_TEMP_BYTES = 8 * 1024 * 1024
VMEM_UNSCOPED_BYTES = 4 * 1024 * 1024
VMEM_MIN_REQUEST_BYTES = 16 * 1024 * 1024

RG_CHUNK_ROWS = 128
PROJ_SUB_ROWS = 256
LN_ROWS = 64

RG_C = 8.0
LN_EPS = 1e-5
RMS_EPS = 1e-5
MASK_VALUE = -1e30


def _vmem_limit(nbytes):
    return int(min(V7X_VMEM_BYTES - VMEM_UNSCOPED_BYTES, max(nbytes + VMEM_TEMP_BYTES, VMEM_MIN_REQUEST_BYTES)))


def _gelu_tanh_times(x, half_y):
    c = math.sqrt(2.0 / math.pi)
    t = jnp.tanh(x * (c + (c * 0.044715) * (x * x)))
    return (x + x * t) * half_y


def _causal_conv_rows(pad_ref, cw, cb, r0, rows):
    conv_k = cw.shape[0]
    pad = V7X_SUBLANES
    strips = []
    for s in range(pad_ref.shape[0]):
        lanes = slice(s * V7X_LANES, (s + 1) * V7X_LANES)
        out = cb[:, lanes] + cw[conv_k - 1:conv_k, lanes] * pad_ref[s, pl.ds(r0 + pad, rows), :]
        for k in range(conv_k - 1):
            start = r0 + pad - (conv_k - 1) + k
            out = out + cw[k:k + 1, lanes] * pad_ref[s, pl.ds(start, rows, stride=1), :]
        strips.append(out)
    return strips[0] if len(strips) == 1 else jnp.concatenate(strips, axis=1)


def _store_slabs(pad_ref, r0, value):
    rows = value.shape[0]
    for s in range(pad_ref.shape[0]):
        pad_ref[s, V7X_SUBLANES + r0:V7X_SUBLANES + r0 + rows, :] = value[:, s * V7X_LANES:(s + 1) * V7X_LANES]


def _inproj_rglru_kernel(x16_ref, wx32_ref, wgt32_ref, wq32a_ref, wq32b_ref, wq32c_ref, cw_ref, cb_ref, wg_ref,
                         ba_ref, bx_ref, lam_ref, o_ref, qkv_ref, xpad_ref, gate_ref, hs_ref,
                         wx_ref, wgt_ref, wq_ref, *, rows, sub_rows):
    S, C = o_ref.shape
    pad = V7X_SUBLANES
    xpad_ref[:, 0:pad, :] = jnp.zeros((xpad_ref.shape[0], pad, V7X_LANES), F32)
    wx_ref[...] = wx32_ref[...].astype(wx_ref.dtype)
    wgt_ref[...] = wgt32_ref[...].astype(wgt_ref.dtype)
    for part, ref in enumerate((wq32a_ref, wq32b_ref, wq32c_ref)):
        wq_ref[:, part * C:(part + 1) * C] = ref[...].astype(wq_ref.dtype)

    lam = lam_ref[...]
    log_sig = jnp.minimum(lam, 0.0) - jnp.log1p(jnp.exp(-jnp.abs(lam)))
    cw = cw_ref[...]
    cb = cb_ref[...]
    ba = ba_ref[...]
    bx = bx_ref[...]
    wg = wg_ref[0]
    n_groups = rows // V7X_SUBLANES
    sub3 = lax.broadcasted_iota(jnp.int32, (n_groups, V7X_SUBLANES, C), 1)
    n_sub = S // sub_rows

    def chunk(t0, h_prev):
        u = _causal_conv_rows(xpad_ref, cw, cb, t0, rows)
        pre = jnp.dot(u.astype(BF16), wg, preferred_element_type=F32)
        r = jax.nn.sigmoid(pre[:, :C] + ba)
        ig = jax.nn.sigmoid(pre[:, C:] + bx)
        log_a = RG_C * r * log_sig
        a = jnp.exp(log_a)
        th = jnp.tanh(log_a)
        m2 = -2.0 * th / (1.0 - th)
        mult = jnp.where(m2 > 0.0, m2 * lax.rsqrt(m2), 0.0)
        b = mult * (ig * u)
        a = a.reshape(n_groups, V7X_SUBLANES, C)
        b = b.reshape(n_groups, V7X_SUBLANES, C)
        for s in (1, 2, 4):
            keep = sub3 >= s
            a_sh = jnp.where(keep, pltpu.roll(a, s, axis=1), 1.0)
            b_sh = jnp.where(keep, pltpu.roll(b, s, axis=1), 0.0)
            b = a * b_sh + b
            a = a * a_sh
        a = a.reshape(rows, C)
        b = b.reshape(rows, C)
        h = h_prev
        for g in range(n_groups):
            lo = g * V7X_SUBLANES
            hg = a[lo:lo + V7X_SUBLANES, :] * h + b[lo:lo + V7X_SUBLANES, :]
            h = hg[V7X_SUBLANES - 1:V7X_SUBLANES, :]
            hs_ref[t0 + lo:t0 + lo + V7X_SUBLANES, :] = hg
        gate = gate_ref[t0:t0 + rows, :]
        o_ref[t0:t0 + rows, :] = _gelu_tanh_times(gate, 0.5 * hs_ref[t0:t0 + rows, :]).astype(o_ref.dtype)
        return h

    h = jnp.zeros((1, C), F32)
    for si in range(n_sub):
        r0 = si * sub_rows
        xs = x16_ref[r0:r0 + sub_rows, :]
        _store_slabs(xpad_ref, r0, jnp.dot(xs, wx_ref[...], preferred_element_type=F32))
        gate_ref[r0:r0 + sub_rows, :] = jnp.dot(xs, wgt_ref[...], preferred_element_type=F32)
        qkv_ref[r0:r0 + sub_rows, :] = jnp.dot(xs, wq_ref[...],
                                               preferred_element_type=F32).astype(qkv_ref.dtype)
        for ci in range(sub_rows // rows):
            h = chunk(r0 + ci * rows, h)


def _inproj_rglru(x16, w_in, layer, cw, cb, wg, ba, bx, lam, *, batch, seq, width, qkv_width, out_dtype):
    C = wg.shape[2]
    nct = width // C
    T, D = x16.shape
    nqc = qkv_width // nct
    q_parts = nqc // C
    q_off = 2 * width // C
    conv_k = cw.shape[1]
    assert qkv_width % nct == 0 and nqc == 3 * C
    vec = lambda: pl.BlockSpec((None, 1, C), lambda b, c: (layer, 0, c))
    wq_spec = lambda part: pl.BlockSpec((None, D, C), lambda b, c: (layer, 0, q_off + q_parts * c + part))
    est = 2 * (seq * D * 2 + 2 * D * C * 4 + D * nqc * 4 + seq * C * 2 + seq * nqc * 2 + C * 2 * C * 2) \
        + 2 * D * C * 2 + D * nqc * 2 + (3 * seq + 8) * C * 4 + PROJ_SUB_ROWS * nqc * 4
    return pl.pallas_call(
        functools.partial(_inproj_rglru_kernel, rows=RG_CHUNK_ROWS, sub_rows=PROJ_SUB_ROWS),
        grid=(batch, nct),
        in_specs=[
            pl.BlockSpec((seq, D), lambda b, c: (b, 0)),
            pl.BlockSpec((None, D, C), lambda b, c: (layer, 0, c)),
            pl.BlockSpec((None, D, C), lambda b, c: (layer, 0, nct + c)),
            wq_spec(0), wq_spec(1), wq_spec(2),
            pl.BlockSpec((None, conv_k, C), lambda b, c: (layer, 0, c)),
            vec(),
            pl.BlockSpec((None, 1, C, 2 * C), lambda b, c: (layer, c, 0, 0)),
            vec(), vec(), vec(),
        ],
        out_specs=[pl.BlockSpec((seq, C), lambda b, c: (b, c)),
                   pl.BlockSpec((seq, nqc), lambda b, c: (b, c))],
        out_shape=[jax.ShapeDtypeStruct((batch * seq, width), out_dtype),
                   jax.ShapeDtypeStruct((T, qkv_width), BF16)],
        scratch_shapes=[pltpu.VMEM((C // V7X_LANES, seq + V7X_SUBLANES, V7X_LANES), F32),
                        pltpu.VMEM((seq, C), F32), pltpu.VMEM((seq, C), F32),
                        pltpu.VMEM((D, C), BF16), pltpu.VMEM((D, C), BF16), pltpu.VMEM((D, nqc), BF16)],
        compiler_params=pltpu.CompilerParams(
            dimension_semantics=("parallel", "parallel"), vmem_limit_bytes=_vmem_limit(est)),
        name="inproj_rglru",
    )(x16, w_in, w_in, w_in, w_in, w_in, cw, cb, wg, ba, bx, lam)


POS_SPLIT = 256
PV_QUERY_COLS = 256
ONES_ROWS = 16


def _attn_kernel(q_ref, k_ref, v_ref, lamv_ref, g_ref, o_ref, ka_ref, vt_ref, s_ref, p_ref,
                 *, tq, tk, n_heads, head_dim, scale, lam_init):
    h = pl.program_id(1)
    S, d2 = q_ref.shape
    V = v_ref.shape[1]
    n_kv = S // tk
    n_q = S // tq

    def extra_cols(shape, c, lo_val, hi_val):
        lane = lax.broadcasted_iota(jnp.int32, shape, 1)
        base = head_dim * (1 - c)
        return jnp.where(lane == base, lo_val, jnp.where(lane == base + 1, hi_val, 0.0))

    def own_half(shape, c):
        lane = lax.broadcasted_iota(jnp.int32, shape, 1)
        return (lane < head_dim) if c == 0 else (lane >= head_dim)

    for jb in range(n_kv):
        rows = slice(jb * tk, (jb + 1) * tk)
        vt_ref[0:V, rows] = v_ref[rows, :].astype(F32).T.astype(vt_ref.dtype)
        vt_ref[V:V + ONES_ROWS, rows] = jnp.ones((ONES_ROWS, tk), vt_ref.dtype)
        kb = k_ref[rows, :]
        pos = lax.broadcasted_iota(jnp.int32, (tk, d2), 0) + jb * tk
        j_lo = (pos % POS_SPLIT).astype(F32)
        j_hi = (pos // POS_SPLIT).astype(F32)
        for c in range(2):
            ka_ref[c, rows, :] = jnp.where(own_half((tk, d2), c), kb,
                                           extra_cols((tk, d2), c, j_lo, j_hi).astype(kb.dtype))

    slope = jnp.exp2(jnp.full((1, 1), -8.0 / n_heads, F32) * (h + 1).astype(F32))
    lv = lamv_ref[...]
    lam = (jnp.exp(jnp.sum(lv[0:1, :] * lv[1:2, :], axis=-1, keepdims=True))
           - jnp.exp(jnp.sum(lv[2:3, :] * lv[3:4, :], axis=-1, keepdims=True)) + lam_init)
    gain = g_ref[...] * (1.0 - lam_init)

    q_aug = {}

    def augmented_queries(qi):
        if qi not in q_aug:
            q = q_ref[qi * tq:(qi + 1) * tq, :] * jnp.asarray(scale, q_ref.dtype)
            q_aug[qi] = [jnp.where(own_half(q.shape, c), q,
                                   extra_cols(q.shape, c, slope, slope * POS_SPLIT).astype(q.dtype))
                         for c in range(2)]
        return q_aug[qi]

    def on_diagonal(qi, j):
        return tq == tk and j == qi and tk % 2 == 0

    def scores(qi, j, tile):
        qa = augmented_queries(qi)
        half_k = tk // 2
        half_q = tq // 2
        for c in range(2):
            keys = ka_ref[c, j * tk:(j + 1) * tk, :]
            if on_diagonal(qi, j):
                s_ref[tile, 0:half_k, c * tq:(c + 1) * tq] = lax.dot_general(
                    keys[0:half_k], qa[c], (((1,), (1,)), ((), ())), preferred_element_type=F32)
                s_ref[tile, half_k:tk, c * tq:c * tq + half_q] = jnp.full((tk - half_k, half_q), MASK_VALUE, F32)
                s_ref[tile, half_k:tk, c * tq + half_q:(c + 1) * tq] = lax.dot_general(
                    keys[half_k:tk], qa[c][half_q:tq], (((1,), (1,)), ((), ())), preferred_element_type=F32)
            else:
                s_ref[tile, :, c * tq:(c + 1) * tq] = lax.dot_general(
                    keys, qa[c], (((1,), (1,)), ((), ())), preferred_element_type=F32)

    def n_tiles(qi):
        return ((qi + 1) * tq + tk - 1) // tk

    def all_scores(qi):
        for j in range(n_tiles(qi)):
            scores(qi, j, (qi % 2) * n_kv + j)

    def masked_scores(qi, j):
        s = s_ref[(qi % 2) * n_kv + j]
        if (j + 1) * tk - 1 > qi * tq:
            key_pos = lax.broadcasted_iota(jnp.int32, (tk, 2 * tq), 0) + j * tk
            col = lax.broadcasted_iota(jnp.int32, (tk, 2 * tq), 1)
            query_pos = jnp.where(col >= tq, col - tq, col) + qi * tq
            s = jnp.where(key_pos <= query_pos, s, MASK_VALUE)
        return s

    def softmax_weights(qi):
        tiles = [masked_scores(qi, j) for j in range(n_tiles(qi))]
        m = functools.reduce(jnp.maximum, [jnp.max(s, axis=0, keepdims=True) for s in tiles])
        for j, s in enumerate(tiles):
            p_ref[qi % 2, j * tk:(j + 1) * tk, :] = jnp.exp(s - m).astype(p_ref.dtype)

    def weighted_values(qi):
        k_end = n_tiles(qi) * tk
        w = min(PV_QUERY_COLS, tq)
        for c0 in range(0, tq, w):
            a1 = jnp.dot(vt_ref[:, 0:k_end], p_ref[qi % 2, 0:k_end, c0:c0 + w], preferred_element_type=F32)
            a2 = jnp.dot(vt_ref[:, 0:k_end], p_ref[qi % 2, 0:k_end, tq + c0:tq + c0 + w],
                         preferred_element_type=F32)
            ot = a1[0:V] / a1[V:V + 1] - lam * (a2[0:V] / a2[V:V + 1])
            ms = jnp.mean(ot * ot, axis=0, keepdims=True)
            o = (ot * lax.rsqrt(ms + RMS_EPS)).T * gain
            o_ref[qi * tq + c0:qi * tq + c0 + w, :] = o.astype(o_ref.dtype)

    all_scores(0)
    for qi in range(n_q):
        if qi + 1 < n_q:
            all_scores(qi + 1)
        softmax_weights(qi)
        if qi >= 1:
            weighted_values(qi - 1)
    weighted_values(n_q - 1)


def _diff_attn(qkv, lamv, g, layer, *, batch, seq, n_heads, head_dim, v_dim, lam_init, out_dtype, tq, tk):
    qk_w = 2 * head_dim
    assert qk_w == v_dim and seq % tq == 0 and seq % tk == 0 and seq % POS_SPLIT == 0
    assert seq // POS_SPLIT <= POS_SPLIT
    return pl.pallas_call(
        functools.partial(_attn_kernel, tq=tq, tk=tk, n_heads=n_heads, head_dim=head_dim,
                          scale=head_dim ** -0.5, lam_init=lam_init),
        grid=(batch, n_heads),
        in_specs=[
            pl.BlockSpec((seq, qk_w), lambda b, h: (b, h)),
            pl.BlockSpec((seq, qk_w), lambda b, h: (b, n_heads + h)),
            pl.BlockSpec((seq, v_dim), lambda b, h: (b, 2 * n_heads + h)),
            pl.BlockSpec((None,) + lamv.shape[1:], lambda b, h: (layer, 0, 0)),
            pl.BlockSpec((None, 1, v_dim), lambda b, h: (layer, 0, 0)),
        ],
        out_specs=pl.BlockSpec((seq, v_dim), lambda b, h: (b, h)),
        out_shape=jax.ShapeDtypeStruct((batch * seq, n_heads * v_dim), out_dtype),
        scratch_shapes=[
            pltpu.VMEM((2, seq, qk_w), BF16),
            pltpu.VMEM((v_dim + ONES_ROWS, seq), BF16),
            pltpu.VMEM((2 * (seq // tk), tk, 2 * tq), F32),
            pltpu.VMEM((2, seq, 2 * tq), BF16),
        ],
        compiler_params=pltpu.CompilerParams(
            dimension_semantics=("parallel", "parallel"),
            vmem_limit_bytes=V7X_VMEM_BYTES - VMEM_UNSCOPED_BYTES),
        name="diff_attn",
    )(qkv, qkv, qkv, lamv, g)


def _residual_layernorm(acc_ref, xres_ref, g, b, o32_ref, o16_ref, *, alpha, r0, n_rows, rows):
    for c in range(n_rows // rows):
        lo = r0 + c * rows
        y = alpha * xres_ref[lo:lo + rows, :] + acc_ref[lo:lo + rows, :]
        mu = jnp.mean(y, axis=-1, keepdims=True)
        d = y - mu
        var = jnp.mean(d * d, axis=-1, keepdims=True)
        out = d * lax.rsqrt(var + LN_EPS) * g + b
        o32_ref[lo:lo + rows, :] = out
        o16_ref[lo:lo + rows, :] = out.astype(o16_ref.dtype)


def _outproj_kernel(rg_ref, at_ref, w32_ref, xres_ref, g_ref, b_ref, o32_ref, o16_ref, acc_ref, w_ref,
                    *, alpha, sub):
    tm, half = rg_ref.shape

    @pl.when(pl.program_id(0) == 0)
    def _():
        w_ref[...] = w32_ref[...].astype(w_ref.dtype)

    g = g_ref[...]
    b = b_ref[...]
    for t in range(tm // sub):
        r0 = t * sub
        acc_ref[r0:r0 + sub, :] = (
            jnp.dot(rg_ref[r0:r0 + sub, :], w_ref[0:half, :], preferred_element_type=F32)
            + jnp.dot(at_ref[r0:r0 + sub, :], w_ref[half:, :], preferred_element_type=F32))
        _residual_layernorm(acc_ref, xres_ref, g, b, o32_ref, o16_ref, alpha=alpha, r0=r0, n_rows=sub,
                            rows=LN_ROWS)


def _outproj_ln(rg_out, at_out, w, layer, xres, g, b, *, alpha, tm, sub):
    T, half = rg_out.shape
    _, Kw, D = w.shape
    est = 2 * (2 * tm * half * 2 + tm * D * 4 + tm * D * 4 + tm * D * 2) + Kw * D * (4 + 2) \
        + 2 * tm * D * 4
    row = lambda width: pl.BlockSpec((tm, width), lambda m: (m, 0))
    full = lambda a: pl.BlockSpec((None,) + a.shape[1:], lambda m: (layer, 0, 0))
    return pl.pallas_call(
        functools.partial(_outproj_kernel, alpha=alpha, sub=sub),
        grid=(T // tm,),
        in_specs=[row(half), row(half),
                  pl.BlockSpec((None, Kw, D), lambda m: (layer, 0, 0), pipeline_mode=pl.Buffered(1)),
                  row(D), full(g), full(b)],
        out_specs=[row(D), row(D)],
        out_shape=[jax.ShapeDtypeStruct((T, D), F32), jax.ShapeDtypeStruct((T, D), BF16)],
        scratch_shapes=[pltpu.VMEM((tm, D), F32), pltpu.VMEM((Kw, D), BF16)],
        compiler_params=pltpu.CompilerParams(
            dimension_semantics=("arbitrary",), vmem_limit_bytes=_vmem_limit(est)),
        name="outproj_ln",
    )(rg_out, at_out, w, xres, g, b)


def _ffn_up_kernel(x_ref, wg32_ref, wu32_ref, cwg_ref, cwu_ref, cbg_ref, cbu_ref, o_ref, pg_ref, pu_ref,
                   wg_ref, wu_ref, *, tiles_per_seq, rows):
    tm, tn = o_ref.shape
    pad = V7X_SUBLANES
    m = pl.program_id(1)

    @pl.when(m == 0)
    def _():
        wg_ref[...] = wg32_ref[...].astype(wg_ref.dtype)
        wu_ref[...] = wu32_ref[...].astype(wu_ref.dtype)

    @pl.when(m % tiles_per_seq == 0)
    def _():
        pg_ref[:, 0:pad, :] = jnp.zeros((pg_ref.shape[0], pad, V7X_LANES), F32)
        pu_ref[:, 0:pad, :] = jnp.zeros((pu_ref.shape[0], pad, V7X_LANES), F32)

    cwg = cwg_ref[...]
    cbg = cbg_ref[...]
    cwu = 0.5 * cwu_ref[...]
    cbu = 0.5 * cbu_ref[...]
    for c in range(tm // rows):
        r0 = c * rows
        xs = x_ref[r0:r0 + rows, :]
        _store_slabs(pg_ref, r0, jnp.dot(xs, wg_ref[...], preferred_element_type=F32))
        _store_slabs(pu_ref, r0, jnp.dot(xs, wu_ref[...], preferred_element_type=F32))
        gv = _causal_conv_rows(pg_ref, cwg, cbg, r0, rows)
        uv = _causal_conv_rows(pu_ref, cwu, cbu, r0, rows)
        o_ref[r0:r0 + rows, :] = _gelu_tanh_times(gv, uv).astype(o_ref.dtype)
    pg_ref[:, 0:pad, :] = pg_ref[:, tm:tm + pad, :]
    pu_ref[:, 0:pad, :] = pu_ref[:, tm:tm + pad, :]


def _ffn_up(x, w, layer, cw, cb, *, seq, d_ff, tm, tn, out_dtype):
    T, K = x.shape
    nt = d_ff // tn
    conv_k = cw.shape[1]
    assert seq % tm == 0 and d_ff % tn == 0
    est = 2 * (tm * K * 2 + 2 * K * tn * 4 + tm * tn * 2) + 2 * K * tn * 2 + 2 * (tm + 8) * tn * 4 \
        + 2 * tm * tn * 4
    vecg = lambda r: pl.BlockSpec((None, r, tn), lambda n, m: (layer, 0, n))
    vecu = lambda r: pl.BlockSpec((None, r, tn), lambda n, m: (layer, 0, n + nt))
    return pl.pallas_call(
        functools.partial(_ffn_up_kernel, tiles_per_seq=seq // tm, rows=min(PROJ_SUB_ROWS, tm)),
        grid=(nt, T // tm),
        in_specs=[
            pl.BlockSpec((tm, K), lambda n, m: (m, 0)),
            pl.BlockSpec((None, K, tn), lambda n, m: (layer, 0, n)),
            pl.BlockSpec((None, K, tn), lambda n, m: (layer, 0, n + nt)),
            vecg(conv_k), vecu(conv_k), vecg(1), vecu(1),
        ],
        out_specs=pl.BlockSpec((tm, tn), lambda n, m: (m, n)),
        out_shape=jax.ShapeDtypeStruct((T, d_ff), out_dtype),
        scratch_shapes=[pltpu.VMEM((tn // V7X_LANES, tm + V7X_SUBLANES, V7X_LANES), F32),
                        pltpu.VMEM((tn // V7X_LANES, tm + V7X_SUBLANES, V7X_LANES), F32),
                        pltpu.VMEM((K, tn), BF16), pltpu.VMEM((K, tn), BF16)],
        compiler_params=pltpu.CompilerParams(
            dimension_semantics=("parallel", "arbitrary"), vmem_limit_bytes=_vmem_limit(est)),
        name="ffn_up",
    )(x, w, w, cw, cw, cb, cb)


def _ffn_down_kernel(h_ref, w32_ref, xres_ref, g_ref, b_ref, o32_ref, o16_ref, acc_ref, w_ref, *, alpha, n_pre):
    s = pl.program_id(0)
    kc = w32_ref.shape[0]

    @pl.when(s < n_pre)
    def _():
        w_ref[pl.ds(pl.multiple_of(s * kc, kc), kc), :] = w32_ref[...].astype(w_ref.dtype)

    @pl.when(s >= n_pre)
    def _():
        tm = h_ref.shape[0]
        acc_ref[...] = jnp.dot(h_ref[...], w_ref[...], preferred_element_type=F32)
        _residual_layernorm(acc_ref, xres_ref, g_ref[...], b_ref[...], o32_ref, o16_ref, alpha=alpha, r0=0,
                            n_rows=tm, rows=LN_ROWS)


def _ffn_down_ln(h, w, layer, xres, g, b, *, alpha, tm, kc):
    T, K = h.shape
    D = w.shape[2]
    assert T % tm == 0 and K % kc == 0
    n_pre = K // kc
    est = 2 * (tm * K * 2 + kc * D * 4 + tm * D * 4 + tm * D * 4 + tm * D * 2) + K * D * 2 + 2 * tm * D * 4
    tile = lambda s: jnp.maximum(s - n_pre, 0)
    row = lambda width: pl.BlockSpec((tm, width), lambda s: (tile(s), 0))
    full = lambda a: pl.BlockSpec((None,) + a.shape[1:], lambda s: (layer, 0, 0))
    return pl.pallas_call(
        functools.partial(_ffn_down_kernel, alpha=alpha, n_pre=n_pre),
        grid=(n_pre + T // tm,),
        in_specs=[row(K),
                  pl.BlockSpec((None, kc, D), lambda s: (layer, jnp.minimum(s, n_pre - 1), 0)),
                  row(D), full(g), full(b)],
        out_specs=[row(D), row(D)],
        out_shape=[jax.ShapeDtypeStruct((T, D), F32), jax.ShapeDtypeStruct((T, D), BF16)],
        scratch_shapes=[pltpu.VMEM((tm, D), F32), pltpu.VMEM((K, D), BF16)],
        compiler_params=pltpu.CompilerParams(
            dimension_semantics=("arbitrary",), vmem_limit_bytes=_vmem_limit(est)),
        name="ffn_down_ln",
    )(h, w, xres, g, b)


def _block_diag(w, tile):
    G, d, _ = w.shape
    per = tile // d
    w4 = w.reshape(G // per, per, d, d)
    eye = jnp.eye(per, dtype=w.dtype)
    return jnp.einsum('jipq,ik->jipkq', w4, eye).reshape(G // per, tile, tile)


def kernel(x, w_in, rg_conv_w, rg_conv_b, rg_gate_a_w, rg_gate_a_b, rg_gate_x_w, rg_gate_x_b, rg_lambda,
           lam_q1, lam_k1, lam_q2, lam_k2, subln_g, w_out, ln_mix_g, ln_mix_b, w_up, ffn_conv_w,
           ffn_conv_b, w_down, ln_ffn_g, ln_ffn_b):
    B, S, D = x.shape
    depth = w_in.shape[0]
    rg_w = rg_conv_w.shape[2]
    head_dim = lam_q1.shape[1]
    v_dim = subln_g.shape[1]
    mix_w = w_out.shape[1]
    attn_w = mix_w - rg_w
    n_heads = attn_w // v_dim
    qkv_w = w_in.shape[2] - 2 * rg_w
    d_ff = w_down.shape[1]
    T = B * S
    alpha = (2.0 * depth) ** 0.25

    vec3 = lambda a: a.reshape(depth, 1, a.shape[-1])
    n_blocks, blk = rg_gate_a_w.shape[1], rg_gate_a_w.shape[2]
    gate_tiles = lambda w: _block_diag(w.reshape(depth * n_blocks, blk, blk), V7X_MXU_DIM)
    wg = jnp.concatenate([gate_tiles(rg_gate_a_w), gate_tiles(rg_gate_x_w)], axis=-1).astype(BF16)
    wg = wg.reshape(depth, rg_w // V7X_MXU_DIM, V7X_MXU_DIM, 2 * V7X_MXU_DIM)
    lamv = jnp.stack([lam_q1, lam_k1, lam_q2, lam_k2], axis=1)
    rg_cb, rg_ba, rg_bx, rg_lam = vec3(rg_conv_b), vec3(rg_gate_a_b), vec3(rg_gate_x_b), vec3(rg_lambda)
    sub_g, mix_g, mix_b = vec3(subln_g), vec3(ln_mix_g), vec3(ln_mix_b)
    ffn_cb, ffn_g, ffn_b = vec3(ffn_conv_b), vec3(ln_ffn_g), vec3(ln_ffn_b)

    x32 = x.reshape(T, D)
    x16 = x32.astype(BF16)
    for i in range(depth):
        lam_init = 0.8 - 0.6 * math.exp(-0.3 * i)
        rg_out, qkv = _inproj_rglru(x16, w_in, i, rg_conv_w, rg_cb, wg, rg_ba, rg_bx, rg_lam,
                                    batch=B, seq=S, width=rg_w, qkv_width=qkv_w, out_dtype=BF16)
        at_out = _diff_attn(qkv, lamv, sub_g, i, batch=B, seq=S, n_heads=n_heads,
                            head_dim=head_dim, v_dim=v_dim, lam_init=lam_init, out_dtype=BF16,
                            tq=min(512, S), tk=min(512, S))
        x32, x16 = _outproj_ln(rg_out, at_out, w_out, i, x32, mix_g, mix_b, alpha=alpha, tm=512, sub=256)
        hmid = _ffn_up(x16, w_up, i, ffn_conv_w, ffn_cb, seq=S, d_ff=d_ff, tm=S, tn=512, out_dtype=BF16)
        x32, x16 = _ffn_down_ln(hmid, w_down, i, x32, ffn_g, ffn_b, alpha=alpha, tm=256, kc=512)
    return x32.reshape(B, S, D)
```

```python
import functools
import math

import jax
import jax.numpy as jnp
from jax import lax
from jax.experimental import pallas as pl
from jax.experimental.pallas import tpu as pltpu

F32 = jnp.float32
BF16 = jnp.bfloat16

V7X_LANES = 128
V7X_SUBLANES = 8
V7X_MXU_DIM = 256
V7X_VMEM_BYTES = 64 * 1024 * 1024
VMEM_TEMP_BYTES = 8 * 1024 * 1024
VMEM_UNSCOPED_BYTES = 4 * 1024 * 1024
VMEM_MIN_REQUEST_BYTES = 16 * 1024 * 1024

RG_CHUNK_ROWS = 128
PROJ_SUB_ROWS = 256
LN_ROWS = 64

RG_C = 8.0
LN_EPS = 1e-5
RMS_EPS = 1e-5
MASK_VALUE = -1e30


def _vmem_limit(nbytes):
    return int(min(V7X_VMEM_BYTES - VMEM_UNSCOPED_BYTES, max(nbytes + VMEM_TEMP_BYTES, VMEM_MIN_REQUEST_BYTES)))


def _gelu_tanh_times(x, half_y):
    c = math.sqrt(2.0 / math.pi)
    t = jnp.tanh(x * (c + (c * 0.044715) * (x * x)))
    return (x + x * t) * half_y


def _causal_conv_rows(pad_ref, cw, cb, r0, rows):
    conv_k = cw.shape[0]
    pad = V7X_SUBLANES
    strips = []
    for s in range(pad_ref.shape[0]):
        lanes = slice(s * V7X_LANES, (s + 1) * V7X_LANES)
        out = cb[:, lanes] + cw[conv_k - 1:conv_k, lanes] * pad_ref[s, pl.ds(r0 + pad, rows), :]
        for k in range(conv_k - 1):
            start = r0 + pad - (conv_k - 1) + k
            out = out + cw[k:k + 1, lanes] * pad_ref[s, pl.ds(start, rows, stride=1), :]
        strips.append(out)
    return strips[0] if len(strips) == 1 else jnp.concatenate(strips, axis=1)


def _store_slabs(pad_ref, r0, value):
    rows = value.shape[0]
    for s in range(pad_ref.shape[0]):
        pad_ref[s, V7X_SUBLANES + r0:V7X_SUBLANES + r0 + rows, :] = value[:, s * V7X_LANES:(s + 1) * V7X_LANES]


def _inproj_rglru_kernel(x16_ref, wx32_ref, wgt32_ref, wq32a_ref, wq32b_ref, wq32c_ref, cw_ref, cb_ref, wg_ref,
                         ba_ref, bx_ref, lam_ref, o_ref, qkv_ref, xpad_ref, gate_ref, hs_ref,
                         wx_ref, wgt_ref, wq_ref, *, rows, sub_rows):
    S, C = o_ref.shape
    pad = V7X_SUBLANES
    xpad_ref[:, 0:pad, :] = jnp.zeros((xpad_ref.shape[0], pad, V7X_LANES), F32)
    wx_ref[...] = wx32_ref[...].astype(wx_ref.dtype)
    wgt_ref[...] = wgt32_ref[...].astype(wgt_ref.dtype)
    for part, ref in enumerate((wq32a_ref, wq32b_ref, wq32c_ref)):
        wq_ref[:, part * C:(part + 1) * C] = ref[...].astype(wq_ref.dtype)

    lam = lam_ref[...]
    log_sig = jnp.minimum(lam, 0.0) - jnp.log1p(jnp.exp(-jnp.abs(lam)))
    cw = cw_ref[...]
    cb = cb_ref[...]
    ba = ba_ref[...]
    bx = bx_ref[...]
    wg = wg_ref[0]
    n_groups = rows // V7X_SUBLANES
    sub3 = lax.broadcasted_iota(jnp.int32, (n_groups, V7X_SUBLANES, C), 1)
    n_sub = S // sub_rows

    def chunk(t0, h_prev):
        u = _causal_conv_rows(xpad_ref, cw, cb, t0, rows)
        pre = jnp.dot(u.astype(BF16), wg, preferred_element_type=F32)
        r = jax.nn.sigmoid(pre[:, :C] + ba)
        ig = jax.nn.sigmoid(pre[:, C:] + bx)
        log_a = RG_C * r * log_sig
        a = jnp.exp(log_a)
        th = jnp.tanh(log_a)
        m2 = -2.0 * th / (1.0 - th)
        mult = jnp.where(m2 > 0.0, m2 * lax.rsqrt(m2), 0.0)
        b = mult * (ig * u)
        a = a.reshape(n_groups, V7X_SUBLANES, C)
        b = b.reshape(n_groups, V7X_SUBLANES, C)
        for s in (1, 2, 4):
            keep = sub3 >= s
            a_sh = jnp.where(keep, pltpu.roll(a, s, axis=1), 1.0)
            b_sh = jnp.where(keep, pltpu.roll(b, s, axis=1), 0.0)
            b = a * b_sh + b
            a = a * a_sh
        a = a.reshape(rows, C)
        b = b.reshape(rows, C)
        h = h_prev
        for g in range(n_groups):
            lo = g * V7X_SUBLANES
            hg = a[lo:lo + V7X_SUBLANES, :] * h + b[lo:lo + V7X_SUBLANES, :]
            h = hg[V7X_SUBLANES - 1:V7X_SUBLANES, :]
            hs_ref[t0 + lo:t0 + lo + V7X_SUBLANES, :] = hg
        gate = gate_ref[t0:t0 + rows, :]
        o_ref[t0:t0 + rows, :] = _gelu_tanh_times(gate, 0.5 * hs_ref[t0:t0 + rows, :]).astype(o_ref.dtype)
        return h

    h = jnp.zeros((1, C), F32)
    for si in range(n_sub):
        r0 = si * sub_rows
        xs = x16_ref[r0:r0 + sub_rows, :]
        _store_slabs(xpad_ref, r0, jnp.dot(xs, wx_ref[...], preferred_element_type=F32))
        gate_ref[r0:r0 + sub_rows, :] = jnp.dot(xs, wgt_ref[...], preferred_element_type=F32)
        qkv_ref[r0:r0 + sub_rows, :] = jnp.dot(xs, wq_ref[...],
                                               preferred_element_type=F32).astype(qkv_ref.dtype)
        for ci in range(sub_rows // rows):
            h = chunk(r0 + ci * rows, h)


def _inproj_rglru(x16, w_in, layer, cw, cb, wg, ba, bx, lam, *, batch, seq, width, qkv_width, out_dtype):
    C = wg.shape[2]
    nct = width // C
    T, D = x16.shape
    nqc = qkv_width // nct
    q_parts = nqc // C
    q_off = 2 * width // C
    conv_k = cw.shape[1]
    assert qkv_width % nct == 0 and nqc == 3 * C
    vec = lambda: pl.BlockSpec((None, 1, C), lambda b, c: (layer, 0, c))
    wq_spec = lambda part: pl.BlockSpec((None, D, C), lambda b, c: (layer, 0, q_off + q_parts * c + part))
    est = 2 * (seq * D * 2 + 2 * D * C * 4 + D * nqc * 4 + seq * C * 2 + seq * nqc * 2 + C * 2 * C * 2) \
        + 2 * D * C * 2 + D * nqc * 2 + (3 * seq + 8) * C * 4 + PROJ_SUB_ROWS * nqc * 4
    return pl.pallas_call(
        functools.partial(_inproj_rglru_kernel, rows=RG_CHUNK_ROWS, sub_rows=PROJ_SUB_ROWS),
        grid=(batch, nct),
        in_specs=[
            pl.BlockSpec((seq, D), lambda b, c: (b, 0)),
            pl.BlockSpec((None, D, C), lambda b, c: (layer, 0, c)),
            pl.BlockSpec((None, D, C), lambda b, c: (layer, 0, nct + c)),
            wq_spec(0), wq_spec(1), wq_spec(2),
            pl.BlockSpec((None, conv_k, C), lambda b, c: (layer, 0, c)),
            vec(),
            pl.BlockSpec((None, 1, C, 2 * C), lambda b, c: (layer, c, 0, 0)),
            vec(), vec(), vec(),
        ],
        out_specs=[pl.BlockSpec((seq, C), lambda b, c: (b, c)),
                   pl.BlockSpec((seq, nqc), lambda b, c: (b, c))],
        out_shape=[jax.ShapeDtypeStruct((batch * seq, width), out_dtype),
                   jax.ShapeDtypeStruct((T, qkv_width), BF16)],
        scratch_shapes=[pltpu.VMEM((C // V7X_LANES, seq + V7X_SUBLANES, V7X_LANES), F32),
                        pltpu.VMEM((seq, C), F32), pltpu.VMEM((seq, C), F32),
                        pltpu.VMEM((D, C), BF16), pltpu.VMEM((D, C), BF16), pltpu.VMEM((D, nqc), BF16)],
        compiler_params=pltpu.CompilerParams(
            dimension_semantics=("parallel", "parallel"), vmem_limit_bytes=_vmem_limit(est)),
        name="inproj_rglru",
    )(x16, w_in, w_in, w_in, w_in, w_in, cw, cb, wg, ba, bx, lam)


POS_SPLIT = 256
ONES_ROWS = 16


def _attn_kernel(q_ref, k_ref, v_ref, lamv_ref, g_ref, o_ref, ka_ref, vt_ref, s_ref, p_ref,
                 *, tq, tk, n_heads, head_dim, scale, lam_init):
    h = pl.program_id(1)
    S, d2 = q_ref.shape
    V = v_ref.shape[1]
    n_kv = S // tk
    n_q = S // tq

    def extra_cols(shape, c, lo_val, hi_val):
        lane = lax.broadcasted_iota(jnp.int32, shape, 1)
        base = head_dim * (1 - c)
        return jnp.where(lane == base, lo_val, jnp.where(lane == base + 1, hi_val, 0.0))

    def own_half(shape, c):
        lane = lax.broadcasted_iota(jnp.int32, shape, 1)
        return (lane < head_dim) if c == 0 else (lane >= head_dim)

    for jb in range(n_kv):
        rows = slice(jb * tk, (jb + 1) * tk)
        vt_ref[0:V, rows] = v_ref[rows, :].astype(F32).T.astype(vt_ref.dtype)
        vt_ref[V:V + ONES_ROWS, rows] = jnp.ones((ONES_ROWS, tk), vt_ref.dtype)
        kb = k_ref[rows, :]
        pos = lax.broadcasted_iota(jnp.int32, (tk, d2), 0) + jb * tk
        j_lo = (pos % POS_SPLIT).astype(F32)
        j_hi = (pos // POS_SPLIT).astype(F32)
        for c in range(2):
            ka_ref[c, rows, :] = jnp.where(own_half((tk, d2), c), kb,
                                           extra_cols((tk, d2), c, j_lo, j_hi).astype(kb.dtype))

    slope = jnp.exp2(jnp.full((1, 1), -8.0 / n_heads, F32) * (h + 1).astype(F32))
    lv = lamv_ref[...]
    lam = (jnp.exp(jnp.sum(lv[0:1, :] * lv[1:2, :], axis=-1, keepdims=True))
           - jnp.exp(jnp.sum(lv[2:3, :] * lv[3:4, :], axis=-1, keepdims=True)) + lam_init)
    gain = g_ref[...] * (1.0 - lam_init)

    q_aug = {}

    def augmented_queries(qi):
        if qi not in q_aug:
            q = q_ref[qi * tq:(qi + 1) * tq, :] * jnp.asarray(scale, q_ref.dtype)
            q_aug[qi] = [jnp.where(own_half(q.shape, c), q,
                                   extra_cols(q.shape, c, slope, slope * POS_SPLIT).astype(q.dtype))
                         for c in range(2)]
        return q_aug[qi]

    def on_diagonal(qi, j):
        return tq == tk and j == qi and tk % 2 == 0

    def scores(qi, j, tile):
        qa = augmented_queries(qi)
        half_k = tk // 2
        half_q = tq // 2
        for c in range(2):
            keys = ka_ref[c, j * tk:(j + 1) * tk, :]
            if on_diagonal(qi, j):
                s_ref[tile, 0:half_k, c * tq:(c + 1) * tq] = lax.dot_general(
                    keys[0:half_k], qa[c], (((1,), (1,)), ((), ())), preferred_element_type=F32)
                s_ref[tile, half_k:tk, c * tq:c * tq + half_q] = jnp.full((tk - half_k, half_q), MASK_VALUE, F32)
                s_ref[tile, half_k:tk, c * tq + half_q:(c + 1) * tq] = lax.dot_general(
                    keys[half_k:tk], qa[c][half_q:tq], (((1,), (1,)), ((), ())), preferred_element_type=F32)
            else:
                s_ref[tile, :, c * tq:(c + 1) * tq] = lax.dot_general(
                    keys, qa[c], (((1,), (1,)), ((), ())), preferred_element_type=F32)

    def n_tiles(qi):
        return ((qi + 1) * tq + tk - 1) // tk

    def all_scores(qi):
        for j in range(n_tiles(qi)):
            scores(qi, j, (qi % 2) * n_kv + j)

    def masked_scores(qi, j):
        s = s_ref[(qi % 2) * n_kv + j]
        if on_diagonal(qi, j):
            hk, hq = tk // 2, tq // 2
            tri = (lax.broadcasted_iota(jnp.int32, (hk, hq), 0) <= lax.broadcasted_iota(jnp.int32, (hk, hq), 1))
            top, bottom = [], []
            for c in range(2):
                top += [jnp.where(tri, s[0:hk, c * tq:c * tq + hq], MASK_VALUE), s[0:hk, c * tq + hq:(c + 1) * tq]]
                bottom += [s[hk:tk, c * tq:c * tq + hq],
                           jnp.where(tri, s[hk:tk, c * tq + hq:(c + 1) * tq], MASK_VALUE)]
            return jnp.concatenate([jnp.concatenate(top, axis=1), jnp.concatenate(bottom, axis=1)], axis=0)
        if (j + 1) * tk - 1 > qi * tq:
            key_pos = lax.broadcasted_iota(jnp.int32, (tk, 2 * tq), 0) + j * tk
            col = lax.broadcasted_iota(jnp.int32, (tk, 2 * tq), 1)
            query_pos = jnp.where(col >= tq, col - tq, col) + qi * tq
            s = jnp.where(key_pos <= query_pos, s, MASK_VALUE)
        return s

    def softmax_weights(qi):
        tiles = [masked_scores(qi, j) for j in range(n_tiles(qi))]
        m = functools.reduce(jnp.maximum, [jnp.max(s, axis=0, keepdims=True) for s in tiles])
        for j, s in enumerate(tiles):
            p_ref[qi % 2, j * tk:(j + 1) * tk, :] = jnp.exp(s - m).astype(p_ref.dtype)

    def weighted_values(qi):
        k_end = n_tiles(qi) * tk
        acc = jnp.dot(vt_ref[:, 0:k_end], p_ref[qi % 2, 0:k_end, :], preferred_element_type=F32)
        ot = (acc[0:V, 0:tq] / acc[V:V + 1, 0:tq]
              - lam * (acc[0:V, tq:2 * tq] / acc[V:V + 1, tq:2 * tq]))
        ms = jnp.mean(ot * ot, axis=0, keepdims=True)
        o = (ot * lax.rsqrt(ms + RMS_EPS)).T * gain
        o_ref[qi * tq:(qi + 1) * tq, :] = o.astype(o_ref.dtype)

    all_scores(0)
    for qi in range(n_q):
        if qi + 1 < n_q:
            all_scores(qi + 1)
        softmax_weights(qi)
        if qi >= 1:
            weighted_values(qi - 1)
    weighted_values(n_q - 1)


def _diff_attn(qkv, lamv, g, layer, *, batch, seq, n_heads, head_dim, v_dim, lam_init, out_dtype, tq, tk):
    qk_w = 2 * head_dim
    assert qk_w == v_dim and seq % tq == 0 and seq % tk == 0 and seq % POS_SPLIT == 0
    assert seq // POS_SPLIT <= POS_SPLIT
    return pl.pallas_call(
        functools.partial(_attn_kernel, tq=tq, tk=tk, n_heads=n_heads, head_dim=head_dim,
                          scale=head_dim ** -0.5, lam_init=lam_init),
        grid=(batch, n_heads),
        in_specs=[
            pl.BlockSpec((seq, qk_w), lambda b, h: (b, h)),
            pl.BlockSpec((seq, qk_w), lambda b, h: (b, n_heads + h)),
            pl.BlockSpec((seq, v_dim), lambda b, h: (b, 2 * n_heads + h)),
            pl.BlockSpec((None,) + lamv.shape[1:], lambda b, h: (layer, 0, 0)),
            pl.BlockSpec((None, 1, v_dim), lambda b, h: (layer, 0, 0)),
        ],
        out_specs=pl.BlockSpec((seq, v_dim), lambda b, h: (b, h)),
        out_shape=jax.ShapeDtypeStruct((batch * seq, n_heads * v_dim), out_dtype),
        scratch_shapes=[
            pltpu.VMEM((2, seq, qk_w), BF16),
            pltpu.VMEM((v_dim + ONES_ROWS, seq), BF16),
            pltpu.VMEM((2 * (seq // tk), tk, 2 * tq), F32),
            pltpu.VMEM((2, seq, 2 * tq), BF16),
        ],
        compiler_params=pltpu.CompilerParams(
            dimension_semantics=("parallel", "parallel"),
            vmem_limit_bytes=V7X_VMEM_BYTES - VMEM_UNSCOPED_BYTES),
        name="diff_attn",
    )(qkv, qkv, qkv, lamv, g)


def _residual_layernorm(acc_ref, xres_ref, g, b, o32_ref, o16_ref, *, alpha, r0, n_rows, rows):
    for c in range(n_rows // rows):
        lo = r0 + c * rows
        y = alpha * xres_ref[lo:lo + rows, :] + acc_ref[lo:lo + rows, :]
        mu = jnp.mean(y, axis=-1, keepdims=True)
        d = y - mu
        var = jnp.mean(d * d, axis=-1, keepdims=True)
        out = d * lax.rsqrt(var + LN_EPS) * g + b
        o32_ref[lo:lo + rows, :] = out
        o16_ref[lo:lo + rows, :] = out.astype(o16_ref.dtype)


def _outproj_kernel(rg_ref, at_ref, w32_ref, xres_ref, g_ref, b_ref, o32_ref, o16_ref, acc_ref, w_ref,
                    *, alpha, sub):
    tm, half = rg_ref.shape

    @pl.when(pl.program_id(0) == 0)
    def _():
        w_ref[...] = w32_ref[...].astype(w_ref.dtype)

    g = g_ref[...]
    b = b_ref[...]
    for t in range(tm // sub):
        r0 = t * sub
        acc_ref[r0:r0 + sub, :] = (
            jnp.dot(rg_ref[r0:r0 + sub, :], w_ref[0:half, :], preferred_element_type=F32)
            + jnp.dot(at_ref[r0:r0 + sub, :], w_ref[half:, :], preferred_element_type=F32))
        _residual_layernorm(acc_ref, xres_ref, g, b, o32_ref, o16_ref, alpha=alpha, r0=r0, n_rows=sub,
                            rows=LN_ROWS)


def _outproj_ln(rg_out, at_out, w, layer, xres, g, b, *, alpha, tm, sub):
    T, half = rg_out.shape
    _, Kw, D = w.shape
    est = 2 * (2 * tm * half * 2 + tm * D * 4 + tm * D * 4 + tm * D * 2) + Kw * D * (4 + 2) \
        + 2 * tm * D * 4
    row = lambda width: pl.BlockSpec((tm, width), lambda m: (m, 0))
    full = lambda a: pl.BlockSpec((None,) + a.shape[1:], lambda m: (layer, 0, 0))
    return pl.pallas_call(
        functools.partial(_outproj_kernel, alpha=alpha, sub=sub),
        grid=(T // tm,),
        in_specs=[row(half), row(half),
                  pl.BlockSpec((None, Kw, D), lambda m: (layer, 0, 0), pipeline_mode=pl.Buffered(1)),
                  row(D), full(g), full(b)],
        out_specs=[row(D), row(D)],
        out_shape=[jax.ShapeDtypeStruct((T, D), F32), jax.ShapeDtypeStruct((T, D), BF16)],
        scratch_shapes=[pltpu.VMEM((tm, D), F32), pltpu.VMEM((Kw, D), BF16)],
        compiler_params=pltpu.CompilerParams(
            dimension_semantics=("arbitrary",), vmem_limit_bytes=_vmem_limit(est)),
        name="outproj_ln",
    )(rg_out, at_out, w, xres, g, b)


def _ffn_up_kernel(x_ref, wg32_ref, wu32_ref, cwg_ref, cwu_ref, cbg_ref, cbu_ref, o_ref, pg_ref, pu_ref,
                   wg_ref, wu_ref, *, tiles_per_seq, rows):
    tm, tn = o_ref.shape
    pad = V7X_SUBLANES
    m = pl.program_id(1)

    @pl.when(m == 0)
    def _():
        wg_ref[...] = wg32_ref[...].astype(wg_ref.dtype)
        wu_ref[...] = wu32_ref[...].astype(wu_ref.dtype)

    @pl.when(m % tiles_per_seq == 0)
    def _():
        pg_ref[:, 0:pad, :] = jnp.zeros((pg_ref.shape[0], pad, V7X_LANES), F32)
        pu_ref[:, 0:pad, :] = jnp.zeros((pu_ref.shape[0], pad, V7X_LANES), F32)

    cwg = cwg_ref[...]
    cbg = cbg_ref[...]
    cwu = 0.5 * cwu_ref[...]
    cbu = 0.5 * cbu_ref[...]
    for c in range(tm // rows):
        r0 = c * rows
        xs = x_ref[r0:r0 + rows, :]
        _store_slabs(pg_ref, r0, jnp.dot(xs, wg_ref[...], preferred_element_type=F32))
        _store_slabs(pu_ref, r0, jnp.dot(xs, wu_ref[...], preferred_element_type=F32))
        gv = _causal_conv_rows(pg_ref, cwg, cbg, r0, rows)
        uv = _causal_conv_rows(pu_ref, cwu, cbu, r0, rows)
        o_ref[r0:r0 + rows, :] = _gelu_tanh_times(gv, uv).astype(o_ref.dtype)
    pg_ref[:, 0:pad, :] = pg_ref[:, tm:tm + pad, :]
    pu_ref[:, 0:pad, :] = pu_ref[:, tm:tm + pad, :]


def _ffn_up(x, w, layer, cw, cb, *, seq, d_ff, tm, tn, out_dtype):
    T, K = x.shape
    nt = d_ff // tn
    conv_k = cw.shape[1]
    assert seq % tm == 0 and d_ff % tn == 0
    est = 2 * (tm * K * 2 + 2 * K * tn * 4 + tm * tn * 2) + 2 * K * tn * 2 + 2 * (tm + 8) * tn * 4 \
        + 2 * tm * tn * 4
    vecg = lambda r: pl.BlockSpec((None, r, tn), lambda n, m: (layer, 0, n))
    vecu = lambda r: pl.BlockSpec((None, r, tn), lambda n, m: (layer, 0, n + nt))
    return pl.pallas_call(
        functools.partial(_ffn_up_kernel, tiles_per_seq=seq // tm, rows=min(PROJ_SUB_ROWS, tm)),
        grid=(nt, T // tm),
        in_specs=[
            pl.BlockSpec((tm, K), lambda n, m: (m, 0)),
            pl.BlockSpec((None, K, tn), lambda n, m: (layer, 0, n)),
            pl.BlockSpec((None, K, tn), lambda n, m: (layer, 0, n + nt)),
            vecg(conv_k), vecu(conv_k), vecg(1), vecu(1),
        ],
        out_specs=pl.BlockSpec((tm, tn), lambda n, m: (m, n)),
        out_shape=jax.ShapeDtypeStruct((T, d_ff), out_dtype),
        scratch_shapes=[pltpu.VMEM((tn // V7X_LANES, tm + V7X_SUBLANES, V7X_LANES), F32),
                        pltpu.VMEM((tn // V7X_LANES, tm + V7X_SUBLANES, V7X_LANES), F32),
                        pltpu.VMEM((K, tn), BF16), pltpu.VMEM((K, tn), BF16)],
        compiler_params=pltpu.CompilerParams(
            dimension_semantics=("parallel", "arbitrary"), vmem_limit_bytes=_vmem_limit(est)),
        name="ffn_up",
    )(x, w, w, cw, cw, cb, cb)


def _ffn_down_kernel(h_ref, w32_ref, xres_ref, g_ref, b_ref, o32_ref, o16_ref, acc_ref, w_ref, *, alpha, n_pre):
    s = pl.program_id(0)
    kc = w32_ref.shape[0]

    @pl.when(s < n_pre)
    def _():
        w_ref[pl.ds(pl.multiple_of(s * kc, kc), kc), :] = w32_ref[...].astype(w_ref.dtype)

    @pl.when(s >= n_pre)
    def _():
        tm = h_ref.shape[0]
        acc_ref[...] = jnp.dot(h_ref[...], w_ref[...], preferred_element_type=F32)
        _residual_layernorm(acc_ref, xres_ref, g_ref[...], b_ref[...], o32_ref, o16_ref, alpha=alpha, r0=0,
                            n_rows=tm, rows=LN_ROWS)


def _ffn_down_ln(h, w, layer, xres, g, b, *, alpha, tm, kc):
    T, K = h.shape
    D = w.shape[2]
    assert T % tm == 0 and K % kc == 0
    n_pre = K // kc
    est = 2 * (tm * K * 2 + kc * D * 4 + tm * D * 4 + tm * D * 4 + tm * D * 2) + K * D * 2 + 2 * tm * D * 4
    tile = lambda s: jnp.maximum(s - n_pre, 0)
    row = lambda width: pl.BlockSpec((tm, width), lambda s: (tile(s), 0))
    full = lambda a: pl.BlockSpec((None,) + a.shape[1:], lambda s: (layer, 0, 0))
    return pl.pallas_call(
        functools.partial(_ffn_down_kernel, alpha=alpha, n_pre=n_pre),
        grid=(n_pre + T // tm,),
        in_specs=[row(K),
                  pl.BlockSpec((None, kc, D), lambda s: (layer, jnp.minimum(s, n_pre - 1), 0)),
                  row(D), full(g), full(b)],
        out_specs=[row(D), row(D)],
        out_shape=[jax.ShapeDtypeStruct((T, D), F32), jax.ShapeDtypeStruct((T, D), BF16)],
        scratch_shapes=[pltpu.VMEM((tm, D), F32), pltpu.VMEM((K, D), BF16)],
        compiler_params=pltpu.CompilerParams(
            dimension_semantics=("arbitrary",), vmem_limit_bytes=_vmem_limit(est)),
        name="ffn_down_ln",
    )(h, w, xres, g, b)


def _block_diag(w, tile):
    G, d, _ = w.shape
    per = tile // d
    w4 = w.reshape(G // per, per, d, d)
    eye = jnp.eye(per, dtype=w.dtype)
    return jnp.einsum('jipq,ik->jipkq', w4, eye).reshape(G // per, tile, tile)


def kernel(x, w_in, rg_conv_w, rg_conv_b, rg_gate_a_w, rg_gate_a_b, rg_gate_x_w, rg_gate_x_b, rg_lambda,
           lam_q1, lam_k1, lam_q2, lam_k2, subln_g, w_out, ln_mix_g, ln_mix_b, w_up, ffn_conv_w,
           ffn_conv_b, w_down, ln_ffn_g, ln_ffn_b):
    B, S, D = x.shape
    depth = w_in.shape[0]
    rg_w = rg_conv_w.shape[2]
    head_dim = lam_q1.shape[1]
    v_dim = subln_g.shape[1]
    mix_w = w_out.shape[1]
    attn_w = mix_w - rg_w
    n_heads = attn_w // v_dim
    qkv_w = w_in.shape[2] - 2 * rg_w
    d_ff = w_down.shape[1]
    T = B * S
    alpha = (2.0 * depth) ** 0.25

    vec3 = lambda a: a.reshape(depth, 1, a.shape[-1])
    n_blocks, blk = rg_gate_a_w.shape[1], rg_gate_a_w.shape[2]
    gate_tiles = lambda w: _block_diag(w.reshape(depth * n_blocks, blk, blk), V7X_MXU_DIM)
    wg = jnp.concatenate([gate_tiles(rg_gate_a_w), gate_tiles(rg_gate_x_w)], axis=-1).astype(BF16)
    wg = wg.reshape(depth, rg_w // V7X_MXU_DIM, V7X_MXU_DIM, 2 * V7X_MXU_DIM)
    lamv = jnp.stack([lam_q1, lam_k1, lam_q2, lam_k2], axis=1)
    rg_cb, rg_ba, rg_bx, rg_lam = vec3(rg_conv_b), vec3(rg_gate_a_b), vec3(rg_gate_x_b), vec3(rg_lambda)
    sub_g, mix_g, mix_b = vec3(subln_g), vec3(ln_mix_g), vec3(ln_mix_b)
    ffn_cb, ffn_g, ffn_b = vec3(ffn_conv_b), vec3(ln_ffn_g), vec3(ln_ffn_b)

    x32 = x.reshape(T, D)
    x16 = x32.astype(BF16)
    for i in range(depth):
        lam_init = 0.8 - 0.6 * math.exp(-0.3 * i)
        rg_out, qkv = _inproj_rglru(x16, w_in, i, rg_conv_w, rg_cb, wg, rg_ba, rg_bx, rg_lam,
                                    batch=B, seq=S, width=rg_w, qkv_width=qkv_w, out_dtype=BF16)
        at_out = _diff_attn(qkv, lamv, sub_g, i, batch=B, seq=S, n_heads=n_heads,
                            head_dim=head_dim, v_dim=v_dim, lam_init=lam_init, out_dtype=BF16,
                            tq=min(512, S), tk=min(512, S))
        x32, x16 = _outproj_ln(rg_out, at_out, w_out, i, x32, mix_g, mix_b, alpha=alpha, tm=512, sub=256)
        hmid = _ffn_up(x16, w_up, i, ffn_conv_w, ffn_cb, seq=S, d_ff=d_ff, tm=S, tn=512, out_dtype=BF16)
        x32, x16 = _ffn_down_ln(hmid, w_down, i, x32, ffn_g, ffn_b, alpha=alpha, tm=256, kc=512)
    return x32.reshape(B, S, D)
```
